```python
import math, functools
import jax, jax.numpy as jnp
from jax import lax
import numpy as np

D_MODEL = 1024
BATCH = 8
SEQ = 4096
DEPTH = 1

PLE_DIM = 256
MIX_W = D_MODEL
GDN_HEADS = 8
GDN_DK = 64
GDN_DV = 64
GDN_QK_W = GDN_HEADS * GDN_DK
GDN_V_W = GDN_HEADS * GDN_DV
GDN_CONV = 4
GDN_CHUNK = 64
MLA_HEADS = 8
MLA_NOPE = 64
MLA_ROPE = 32
MLA_V = 64
MLA_W = MLA_HEADS * MLA_V
MLA_Q_RANK = 256
MLA_KV_RANK = 128
ROPE_THETA = 10000.0
Q_BLOCK = 128
D_FF = 4 * D_MODEL
EPS = 1e-6
IN_SIZES = (2 * GDN_QK_W + GDN_V_W,
            GDN_V_W,
            GDN_HEADS,
            GDN_HEADS,
            MLA_Q_RANK,
            MLA_KV_RANK,
            MLA_ROPE)
IN_W = sum(IN_SIZES)

kernel_name = "hybrid_gdn_mla_parallel_heads_block"


def rmsnorm(x, w):
    xf = x.astype(jnp.float32)
    y = xf * lax.rsqrt(jnp.mean(xf * xf, axis=-1, keepdims=True) + EPS)
    return (y * w.astype(jnp.float32)).astype(x.dtype)


def l2norm(x):
    return x * lax.rsqrt(jnp.sum(x * x, axis=-1, keepdims=True) + EPS)


def rope(x, positions):
    r = x.shape[-1]
    inv_freq = ROPE_THETA ** (-jnp.arange(0, r, 2, dtype=jnp.float32) / r)
    ang = positions.astype(jnp.float32)[:, None, :, None] * inv_freq
    cos, sin = jnp.cos(ang), jnp.sin(ang)
    xf = x.astype(jnp.float32)
    x1, x2 = xf[..., : r // 2], xf[..., r // 2:]
    return jnp.concatenate([x1 * cos - x2 * sin, x2 * cos + x1 * sin], axis=-1).astype(x.dtype)


def causal_depthwise_conv(x, w):
    k, c = w.shape
    return lax.conv_general_dilated(
        x, w[:, None, :].astype(x.dtype), window_strides=(1,), padding=[(k - 1, 0)],
        dimension_numbers=("NWC", "WIO", "NWC"), feature_group_count=c)


def gated_delta_rule_chunked(q, k, v, g, beta):
    B, H, S, dk = k.shape
    dv = v.shape[-1]
    C = GDN_CHUNK
    N = S // C
    q = q * (dk ** -0.5)
    to_chunks = lambda t: t.reshape(B, H, N, C, t.shape[-1])
    q, k, v = to_chunks(q), to_chunks(k), to_chunks(v)
    g = jnp.cumsum(g.reshape(B, H, N, C), axis=-1)
    beta = beta.reshape(B, H, N, C)
    incl = jnp.tril(jnp.ones((C, C), dtype=bool))
    strict = jnp.tril(jnp.ones((C, C), dtype=bool), -1)
    decay = jnp.exp(jnp.where(incl, g[..., :, None] - g[..., None, :], -jnp.inf))
    kk = jnp.einsum('bhnid,bhnjd->bhnij', k, k)
    lower = jnp.where(strict, kk * decay * beta[..., :, None], 0.0)
    eye = jnp.eye(C, dtype=q.dtype)
    T = lax.linalg.triangular_solve(eye + lower, jnp.broadcast_to(eye, lower.shape),
                                    left_side=True, lower=True)
    u = T @ (v * beta[..., None])
    w = T @ (k * (beta * jnp.exp(g))[..., None])
    qk = jnp.einsum('bhnid,bhnjd->bhnij', q, k) * decay
    q_dec = q * jnp.exp(g)[..., None]
    g_last = g[..., -1:]
    k_dec = k * jnp.exp(g_last - g)[..., None]
    state_decay = jnp.exp(g_last[..., 0])

    def step(state, xs):
        qd, a, uc, wc, kd, sd = xs
        v_new = uc - wc @ state
        o = qd @ state + a @ v_new
        state = state * sd[..., None, None] + jnp.swapaxes(kd, -1, -2) @ v_new
        return state, o

    state0 = jnp.zeros((B, H, dk, dv), q.dtype)
    xs = tuple(jnp.moveaxis(t, 2, 0) for t in (q_dec, qk, u, w, k_dec, state_decay))
    _, o = lax.scan(step, state0, xs)
    return jnp.moveaxis(o, 0, 2).reshape(B, H, S, dv)


def gdn_mixer(qkv, z, b, a, conv_w, A_log, dt_bias, norm_w):
    B, S, _ = qkv.shape
    dt = qkv.dtype
    qkv = jax.nn.silu(causal_depthwise_conv(qkv, conv_w))
    q, k, v = qkv[..., :GDN_QK_W], qkv[..., GDN_QK_W:2 * GDN_QK_W], qkv[..., 2 * GDN_QK_W:]
    heads = lambda t, d: t.astype(jnp.float32).reshape(B, S, GDN_HEADS, d).transpose(0, 2, 1, 3)
    q = l2norm(heads(q, GDN_DK))
    k = l2norm(heads(k, GDN_DK))
    v = heads(v, GDN_DV)
    beta = jax.nn.sigmoid(b.astype(jnp.float32)).transpose(0, 2, 1)
    g = -(jnp.exp(A_log.astype(jnp.float32)) *
          jax.nn.softplus(a.astype(jnp.float32) + dt_bias.astype(jnp.float32))).transpose(0, 2, 1)
    o = gated_delta_rule_chunked(q, k, v, g, beta).transpose(0, 2, 1, 3)
    zf = z.astype(jnp.float32).reshape(B, S, GDN_HEADS, GDN_DV)
    o = rmsnorm(o, norm_w) * jax.nn.silu(zf)
    return o.reshape(B, S, GDN_V_W).astype(dt)


def blocked_causal_attention(q, k, v):
    B, H, S, dq = q.shape
    dv = v.shape[-1]
    nb = S // Q_BLOCK
    qb = q.reshape(B, H, nb, Q_BLOCK, dq).transpose(2, 0, 1, 3, 4)
    key_idx = jnp.arange(S)
    scale = dq ** -0.5

    def block(args):
        q_blk, start = args
        s = jnp.einsum('bhqd,bhkd->bhqk', q_blk, k, preferred_element_type=jnp.float32) * scale
        q_idx = start + jnp.arange(Q_BLOCK)
        s = jnp.where(key_idx[None, :] <= q_idx[:, None], s, -jnp.inf)
        pr = jax.nn.softmax(s, axis=-1).astype(v.dtype)
        return jnp.einsum('bhqk,bhkd->bhqd', pr, v)

    out = lax.map(block, (qb, jnp.arange(nb, dtype=jnp.int32) * Q_BLOCK))
    return out.transpose(1, 2, 0, 3, 4).reshape(B, H, S, dv)


def mla_mixer(c_q, c_kv, k_pe, positions, q_norm_w, w_q_b, kv_norm_w, w_kv_b, out_norm_w):
    B, S, _ = c_q.shape
    q = (rmsnorm(c_q, q_norm_w) @ w_q_b).reshape(B, S, MLA_HEADS, MLA_NOPE + MLA_ROPE)
    q = q.transpose(0, 2, 1, 3)
    q_nope, q_pe = q[..., :MLA_NOPE], q[..., MLA_NOPE:]
    kv = (rmsnorm(c_kv, kv_norm_w) @ w_kv_b).reshape(B, S, MLA_HEADS, MLA_NOPE + MLA_V)
    kv = kv.transpose(0, 2, 1, 3)
    k_nope, v = kv[..., :MLA_NOPE], kv[..., MLA_NOPE:]
    q_pe = rope(q_pe, positions)
    k_pe = rope(k_pe[:, None], positions)
    qf = jnp.concatenate([q_nope, q_pe], axis=-1)
    kf = jnp.concatenate([k_nope, jnp.broadcast_to(k_pe, (B, MLA_HEADS, S, MLA_ROPE))], axis=-1)
    o = blocked_causal_attention(qf, kf, v).transpose(0, 2, 1, 3).reshape(B, S, MLA_W)
    return rmsnorm(o, out_norm_w)


def setup_inputs(seed: int = 0) -> dict:
    key = jax.random.key(seed)
    ks = jax.random.split(key, 32)
    f32 = jnp.float32

    def dense(k, fan_in, fan_out):
        return jax.random.normal(k, (DEPTH, fan_in, fan_out), f32) * fan_in ** -0.5

    def gain(k, n):
        return jnp.ones((DEPTH, n), f32) + 0.01 * jax.random.normal(k, (DEPTH, n), f32)

    x = jax.random.normal(ks[0], (BATCH, SEQ, D_MODEL), f32)
    p = jax.random.normal(ks[1], (DEPTH, BATCH, SEQ, PLE_DIM), f32)
    offset = jax.random.randint(ks[2], (BATCH, 1), 0, 1024, dtype=jnp.int32)
    positions = offset + jnp.arange(SEQ, dtype=jnp.int32)[None, :]
    dt0 = jnp.exp(jax.random.uniform(ks[3], (DEPTH, GDN_HEADS), f32, math.log(1e-3), math.log(1e-1)))
    return {
        "x": x,
        "p": p,
        "positions": positions,
        "mix_norm_w": gain(ks[4], D_MODEL),
        "w_in": dense(ks[5], D_MODEL, IN_W),
        "conv_w": jax.random.normal(ks[6], (DEPTH, GDN_CONV, 2 * GDN_QK_W + GDN_V_W), f32) * GDN_CONV ** -0.5,
        "A_log": jnp.log(jax.random.uniform(ks[7], (DEPTH, GDN_HEADS), f32, 1.0, 16.0)),
        "dt_bias": jnp.log(jnp.expm1(dt0)),
        "gdn_norm_w": gain(ks[8], GDN_DV),
        "q_norm_w": gain(ks[9], MLA_Q_RANK),
        "w_q_b": dense(ks[10], MLA_Q_RANK, MLA_HEADS * (MLA_NOPE + MLA_ROPE)),
        "kv_norm_w": gain(ks[11], MLA_KV_RANK),
        "w_kv_b": dense(ks[12], MLA_KV_RANK, MLA_HEADS * (MLA_NOPE + MLA_V)),
        "mla_out_norm_w": gain(ks[13], MLA_W),
        "w_out": dense(ks[14], MIX_W, D_MODEL),
        "mlp_norm_w": gain(ks[15], D_MODEL),
        "w_up": dense(ks[16], D_MODEL, D_FF),
        "w_down": dense(ks[17], D_FF, D_MODEL),
        "w_ple_proj": dense(ks[18], PLE_DIM, D_MODEL),
        "ple_post_norm_w": gain(ks[19], D_MODEL),
        "ple_gate_norm_w": gain(ks[20], D_MODEL),
        "w_ple_gate": dense(ks[21], D_MODEL, D_MODEL),
        "final_norm_w": gain(ks[22], D_MODEL)[0],
    }


def reference(x, p, positions, mix_norm_w, w_in, conv_w, A_log, dt_bias, gdn_norm_w,
              q_norm_w, w_q_b, kv_norm_w, w_kv_b, mla_out_norm_w, w_out, mlp_norm_w,
              w_up, w_down, w_ple_proj, ple_post_norm_w, ple_gate_norm_w, w_ple_gate,
              final_norm_w):
    h = x
    for i in range(DEPTH):
        u = rmsnorm(h, mix_norm_w[i])
        proj = u @ w_in[i]
        cols = []
        start = 0
        for size in IN_SIZES:
            cols.append(proj[..., start:start + size])
            start += size
        qkv, z, b, a, c_q, c_kv, k_pe = cols
        y_gdn = gdn_mixer(qkv, z, b, a, conv_w[i], A_log[i], dt_bias[i], gdn_norm_w[i])
        y_mla = mla_mixer(c_q, c_kv, k_pe, positions, q_norm_w[i], w_q_b[i], kv_norm_w[i],
                          w_kv_b[i], mla_out_norm_w[i])
        h = h + jnp.concatenate([y_gdn, y_mla], axis=-1) @ w_out[i]
        u = rmsnorm(h, mlp_norm_w[i])
        h = h + jnp.square(jax.nn.relu(u @ w_up[i])) @ w_down[i]
        e = rmsnorm(p[i].astype(h.dtype) @ w_ple_proj[i], ple_post_norm_w[i])
        gate = jax.nn.sigmoid(rmsnorm(h, ple_gate_norm_w[i]) @ w_ple_gate[i])
        h = h + gate * e
    return rmsnorm(h, final_norm_w)
```

```python
import functools

import jax
import jax.numpy as jnp
from jax import lax
from jax.experimental import pallas as pl
from jax.experimental.pallas import tpu as pltpu

F32 = jnp.float32
BF16 = jnp.bfloat16

D_MODEL = 1024
PLE_DIM = 256
GDN_HEADS = 8
GDN_DK = 64
GDN_DV = 64
GDN_QK_W = GDN_HEADS * GDN_DK
GDN_V_W = GDN_HEADS * GDN_DV
GDN_CONV = 4
GDN_CHUNK = 64
MLA_HEADS = 8
MLA_NOPE = 64
MLA_ROPE = 32
MLA_V = 64
MLA_W = MLA_HEADS * MLA_V
MLA_Q_RANK = 256
MLA_KV_RANK = 128
ROPE_THETA = 10000.0
D_FF = 4 * D_MODEL
EPS = 1e-6

LANES = 128
HEAD_PAIRS = GDN_HEADS // 2
QKV_W = 2 * GDN_QK_W + GDN_V_W
LAT_W = 512
IN_W_PAD = QKV_W + GDN_V_W + LAT_W
KPE_OFF, B_OFF, A_OFF = 0, MLA_ROPE, MLA_ROPE + GDN_HEADS
VMEM_LIMIT = 56 * 1024 * 1024

TOK_TILE = 512
GDN_ROWS = 256
ATT_TQ = 512
ATT_TK = 512
NEG_BIG = -1e30


def _dot(a, b):
    return jnp.dot(a, b, preferred_element_type=F32)


def _dot_nt(a, b):
    return lax.dot_general(a, b, (((1,), (1,)), ((), ())), preferred_element_type=F32)


def _dot_tn(a, b):
    return lax.dot_general(a, b, (((0,), (0,)), ((), ())), preferred_element_type=F32)


def _split2(x):
    hi = x.astype(BF16)
    lo = (x - hi.astype(F32)).astype(BF16)
    return hi, lo


def _split3(x):
    hi = x.astype(BF16)
    r1 = x - hi.astype(F32)
    mid = r1.astype(BF16)
    lo = (r1 - mid.astype(F32)).astype(BF16)
    return hi, mid, lo


def _dot_exact_rhs(a_bf16, x):
    hi, mid, lo = _split3(x)
    return _dot(a_bf16, hi) + _dot(a_bf16, mid) + _dot(a_bf16, lo)


def _dot_exact_lhs(x, b_bf16):
    hi, mid, lo = _split3(x)
    return _dot(hi, b_bf16) + _dot(mid, b_bf16) + _dot(lo, b_bf16)


def _dot_hi(a, b):
    a_hi, a_lo = _split2(a)
    b_hi, b_lo = _split2(b)
    return _dot(a_hi, b_hi) + _dot(a_hi, b_lo) + _dot(a_lo, b_hi)


def _rms(x, w):
    ms = jnp.mean(x * x, axis=-1, keepdims=True)
    return x * lax.rsqrt(ms + EPS) * w


def _sigmoid(x):
    return 1.0 / (1.0 + jnp.exp(-x))


def _softplus(x):
    return jnp.maximum(x, 0.0) + jnp.log(1.0 + jnp.exp(-jnp.abs(x)))


def _inproj_kernel(x_ref, nw_ref, w_ref, qkv_ref, z_ref, lat_ref):
    u = _rms(x_ref[...], nw_ref[...]).astype(BF16)
    qkv_ref[...] = _dot(u, w_ref[:, 0:QKV_W])
    z_ref[...] = _dot(u, w_ref[:, QKV_W:QKV_W + GDN_V_W])
    lat_ref[...] = _dot(u, w_ref[:, QKV_W + GDN_V_W:IN_W_PAD])


def _inproj(x2, mix_norm_w, w_in_p):
    T = x2.shape[0]
    tm = min(TOK_TILE, T)
    return pl.pallas_call(
        _inproj_kernel,
        grid=(T // tm,),
        in_specs=[
            pl.BlockSpec((tm, D_MODEL), lambda i: (i, 0)),
            pl.BlockSpec((1, D_MODEL), lambda i: (0, 0)),
            pl.BlockSpec((D_MODEL, IN_W_PAD), lambda i: (0, 0)),
        ],
        out_specs=[
            pl.BlockSpec((tm, QKV_W), lambda i: (i, 0)),
            pl.BlockSpec((tm, GDN_V_W), lambda i: (i, 0)),
            pl.BlockSpec((tm, LAT_W), lambda i: (i, 0)),
        ],
        out_shape=[
            jax.ShapeDtypeStruct((T, QKV_W), F32),
            jax.ShapeDtypeStruct((T, GDN_V_W), F32),
            jax.ShapeDtypeStruct((T, LAT_W), F32),
        ],
        compiler_params=pltpu.CompilerParams(
            dimension_semantics=("arbitrary",), vmem_limit_bytes=VMEM_LIMIT),
        name="inproj",
    )(x2, mix_norm_w, w_in_p)


def _gdn_kernel(alog_ref, dtb_ref, q_ref, k_ref, v_ref, z_ref, lat_ref, cwq_ref, cwk_ref, cwv_ref,
                nw_ref, o_ref, qn_s, kn_s, vv_s, beta_s, gc_s, u_s, w_s, qd_s, kd_s, a_s):
    C = GDN_CHUNK
    S = q_ref.shape[0]
    R = min(GDN_ROWS, S)
    hp = pl.program_id(1)

    lane = lax.broadcasted_iota(jnp.int32, (1, LANES), 1)
    head1 = lane >= C
    r128 = lax.broadcasted_iota(jnp.int32, (LANES, LANES), 0)
    c128 = lax.broadcasted_iota(jnp.int32, (LANES, LANES), 1)
    bdmask = (r128 // C) == (c128 // C)
    ones_bd = jnp.where(bdmask, 1.0, 0.0).astype(BF16)
    rC = lax.broadcasted_iota(jnp.int32, (C, LANES), 0)
    cC = lax.broadcasted_iota(jnp.int32, (C, LANES), 1) % C
    eye2 = rC == cC
    tril2 = rC >= cC
    strict2 = rC > cC
    eye2f = jnp.where(eye2, 1.0, 0.0)
    ones_cc = jnp.ones((C, C), BF16)
    rR = lax.broadcasted_iota(jnp.int32, (R, R), 0)
    cR = lax.broadcasted_iota(jnp.int32, (R, R), 1)
    tril_bd = jnp.where(((rR // C) == (cR // C)) & (cR <= rR), 1.0, 0.0).astype(BF16)
    sel_r = lax.broadcasted_iota(jnp.int32, (LANES, 2 * LANES), 0)
    sel_c = lax.broadcasted_iota(jnp.int32, (LANES, 2 * LANES), 1)
    sel_src = jnp.where(sel_c < LANES, B_OFF, A_OFF) + 2 * hp + ((sel_c % LANES) // C)
    sel = jnp.where(sel_r == sel_src, 1.0, 0.0).astype(BF16)

    a_log = jnp.where(head1, alog_ref[2 * hp + 1], alog_ref[2 * hp])
    dt_b = jnp.where(head1, dtb_ref[2 * hp + 1], dtb_ref[2 * hp])
    a_coef = jnp.exp(a_log)

    def headsum(x2):
        hi, lo = _split2(x2)
        return _dot(hi, ones_bd) + _dot(lo, ones_bd)

    def bd(m):
        return jnp.concatenate([jnp.where(head1, 0.0, m), jnp.where(head1, m, 0.0)], axis=0)

    def phase_a(t, carry):
        r0 = pl.multiple_of(t * R, R)
        rp = pl.multiple_of(jnp.maximum(r0 - 8, 0), 8)
        rows = pl.ds(r0, R)

        def conv_silu(ref, cw_ref):
            cur = ref[rows, :]
            prev = jnp.where(t > 0, ref[pl.ds(rp, 8), :], 0.0)
            ext = jnp.concatenate([prev, cur], axis=0)
            cw = cw_ref[...]
            acc = cur * cw[GDN_CONV - 1:GDN_CONV, :]
            for kk in range(GDN_CONV - 1):
                shifted = pltpu.roll(ext, GDN_CONV - 1 - kk, axis=0)[8:8 + R, :]
                acc = acc + shifted * cw[kk:kk + 1, :]
            return acc * _sigmoid(acc)

        q = conv_silu(q_ref, cwq_ref)
        k = conv_silu(k_ref, cwk_ref)
        v = conv_silu(v_ref, cwv_ref)
        qn_s[rows, :] = q * lax.rsqrt(headsum(q * q) + EPS) * (GDN_DK ** -0.5)
        kn_s[rows, :] = k * lax.rsqrt(headsum(k * k) + EPS)
        vv_s[rows, :] = v

        raw = _dot_exact_lhs(lat_ref[rows, :], sel)
        beta_s[rows, :] = _sigmoid(raw[:, :LANES])
        g_raw = -(a_coef * _softplus(raw[:, LANES:] + dt_b))
        gc_s[rows, :] = _dot_exact_rhs(tril_bd, g_raw)
        return carry

    lax.fori_loop(0, S // R, phase_a, 0)

    def chunk_factors(n):
        rows = pl.ds(pl.multiple_of(n * C, C), C)
        q = qn_s[rows, :]
        k = kn_s[rows, :]
        v = vv_s[rows, :]
        beta = beta_s[rows, :]
        gc = gc_s[rows, :]
        eg = jnp.exp(gc)
        kdec = jnp.exp(gc[C - 1:C, :] - gc)
        k_rhs = bd(k).astype(BF16)
        kq = jnp.concatenate([k, q], axis=0).astype(BF16)
        kkqk = _dot_nt(kq, k_rhs)
        kk = kkqk[:C]
        qk = kkqk[C:]
        g_row = _dot_exact_rhs(ones_cc, jnp.where(eye2, gc, 0.0))
        dm = jnp.where(tril2, jnp.exp(jnp.where(tril2, gc - g_row, 0.0)), 0.0)
        low = jnp.where(strict2, kk * dm * beta, 0.0)
        a_s[rows, :] = (qk * dm).astype(BF16)
        x = -low
        p = eye2f + x
        y = _dot_hi(x, bd(x))
        for lvl in range(1, 6):
            if lvl < 5:
                res = _dot_hi(y, jnp.concatenate([bd(p), bd(y)], axis=1))
                p = p + res[:, :LANES]
                y = res[:, LANES:]
            else:
                p = p + _dot_hi(y, bd(p))
        rhs = jnp.concatenate([bd(v * beta), bd(k * beta * eg)], axis=1).astype(BF16)
        uw = _dot(p.astype(BF16), rhs)
        u_s[rows, :] = uw[:, :LANES]
        w_s[rows, :] = uw[:, LANES:].astype(BF16)
        qd_s[rows, :] = (q * eg).astype(BF16)
        kd_s[rows, :] = (k * kdec).astype(BF16)

    GB = 2

    def phase_b(i, carry):
        for g in range(GB):
            chunk_factors(i * GB + g)
        return carry

    lax.fori_loop(0, S // C // GB, phase_b, 0)

    def phase_c(n, state):
        r0 = pl.multiple_of(n * C, C)
        rows = pl.ds(r0, C)
        wq = jnp.concatenate([w_s[rows, :], qd_s[rows, :]], axis=0)
        res = _dot(wq, state.astype(BF16))
        v_new = u_s[rows, :] - res[:C]
        o = res[C:] + _dot(a_s[rows, :], bd(v_new).astype(BF16))
        sd = jnp.exp(gc_s[pl.ds(r0 + C - 8, 8), :][7:8, :])
        upd = _dot_tn(kd_s[rows, :], v_new.astype(BF16))
        qn_s[rows, :] = o
        return state * sd + jnp.where(bdmask, upd, 0.0)

    lax.fori_loop(0, S // C, phase_c, jnp.zeros((LANES, LANES), F32))

    def phase_d(t, carry):
        rows = pl.ds(pl.multiple_of(t * R, R), R)
        o = qn_s[rows, :]
        ms = headsum(o * o) * (1.0 / GDN_DV)
        zf = z_ref[rows, :]
        y = o * lax.rsqrt(ms + EPS) * nw_ref[...] * (zf * _sigmoid(zf))
        o_ref[rows, :] = y.astype(o_ref.dtype)
        return carry

    lax.fori_loop(0, S // R, phase_d, 0)


def _gdn(qkv3, z3, lat3, conv_w, a_log, dt_bias, norm_w2):
    B, S, _ = qkv3.shape
    HP = HEAD_PAIRS
    seq_blk = lambda off: pl.BlockSpec((None, S, LANES), lambda b, h, off=off: (b, 0, off + h))
    cw_blk = lambda off: pl.BlockSpec((GDN_CONV, LANES), lambda b, h, off=off: (0, off + h))
    smem = pl.BlockSpec(memory_space=pltpu.SMEM)
    return pl.pallas_call(
        _gdn_kernel,
        grid=(B, HP),
        in_specs=[
            smem, smem,
            seq_blk(0), seq_blk(HP), seq_blk(2 * HP),
            seq_blk(0),
            pl.BlockSpec((None, S, LANES), lambda b, h: (b, 0, LAT_W // LANES - 1)),
            cw_blk(0), cw_blk(HP), cw_blk(2 * HP),
            pl.BlockSpec((1, LANES), lambda b, h: (0, 0)),
        ],
        out_specs=pl.BlockSpec((None, S, LANES), lambda b, h: (b, 0, h)),
        out_shape=jax.ShapeDtypeStruct((B, S, GDN_V_W), BF16),
        scratch_shapes=[
            pltpu.VMEM((S, LANES), F32),
            pltpu.VMEM((S, LANES), F32),
            pltpu.VMEM((S, LANES), F32),
            pltpu.VMEM((S, LANES), F32),
            pltpu.VMEM((S, LANES), F32),
            pltpu.VMEM((S, LANES), F32),
            pltpu.VMEM((S, LANES), BF16),
            pltpu.VMEM((S, LANES), BF16),
            pltpu.VMEM((S, LANES), BF16),
            pltpu.VMEM((S, LANES), BF16),
        ],
        compiler_params=pltpu.CompilerParams(
            dimension_semantics=("arbitrary", "arbitrary"), vmem_limit_bytes=VMEM_LIMIT),
        name="gdn",
    )(a_log, dt_bias, qkv3, qkv3, qkv3, z3, lat3, conv_w, conv_w, conv_w, norm_w2)


def _mla_prep_kernel(pos_ref, freq_ref, lat_ref, qnw_ref, kvnw_ref, wq_ref, wk_ref, wv_ref,
                     q_ref, k_ref, v_ref):
    tm = lat_ref.shape[0]
    half = MLA_ROPE // 2
    scale = (MLA_NOPE + MLA_ROPE) ** -0.5
    ang = freq_ref[...] * pos_ref[...].astype(F32)
    cos_t = jnp.cos(ang)
    sin_t = jnp.sin(ang)
    zeros = lambda n: jnp.zeros((n, tm), F32)
    cos_tab = jnp.concatenate([zeros(MLA_NOPE), cos_t, cos_t, zeros(LANES - MLA_NOPE - MLA_ROPE)], axis=0).T
    sin_lo = jnp.concatenate([zeros(MLA_NOPE), -sin_t, zeros(LANES - MLA_NOPE - half)], axis=0).T
    sin_hi = jnp.concatenate([zeros(MLA_NOPE + half), sin_t, zeros(LANES - MLA_NOPE - MLA_ROPE)], axis=0).T
    lane = lax.broadcasted_iota(jnp.int32, (1, LANES), 1)
    nope = jnp.where(lane < MLA_NOPE, 1.0, 0.0)

    def rot(xh, c):
        return xh * c + pltpu.roll(xh, LANES - half, axis=1) * sin_lo + pltpu.roll(xh, half, axis=1) * sin_hi

    lat = lat_ref[...]
    cq = _rms(lat[:, 0:MLA_Q_RANK], qnw_ref[...]).astype(BF16)
    ckv = _rms(lat[:, MLA_Q_RANK:MLA_Q_RANK + MLA_KV_RANK], kvnw_ref[...]).astype(BF16)
    q = _dot(cq, wq_ref[...])
    k_nope = _dot(ckv, wk_ref[...])
    v_ref[...] = _dot(ckv, wv_ref[...]).astype(v_ref.dtype)

    kpe = pltpu.roll(lat[:, LAT_W - LANES:], MLA_NOPE - KPE_OFF, axis=1)
    k_rope = rot(kpe, cos_tab)
    c_q = cos_tab + nope
    for h in range(MLA_HEADS):
        blk = slice(h * LANES, (h + 1) * LANES)
        q_ref[:, blk] = (rot(q[:, blk], c_q) * scale).astype(q_ref.dtype)
        k_ref[:, blk] = (k_nope[:, blk] + k_rope).astype(k_ref.dtype)


def _mla_prep(pos_row, inv_freq, lat, q_norm_w, kv_norm_w, wq_p, wk_p, wv_p):
    T = lat.shape[0]
    tm = min(TOK_TILE, T)
    full = lambda a: pl.BlockSpec(a.shape, lambda i: (0,) * a.ndim)
    return pl.pallas_call(
        _mla_prep_kernel,
        grid=(T // tm,),
        in_specs=[
            pl.BlockSpec((1, tm), lambda i: (0, i)),
            full(inv_freq),
            pl.BlockSpec((tm, LAT_W), lambda i: (i, 0)),
            full(q_norm_w), full(kv_norm_w), full(wq_p), full(wk_p), full(wv_p),
        ],
        out_specs=[
            pl.BlockSpec((tm, MLA_HEADS * LANES), lambda i: (i, 0)),
            pl.BlockSpec((tm, MLA_HEADS * LANES), lambda i: (i, 0)),
            pl.BlockSpec((tm, MLA_W), lambda i: (i, 0)),
        ],
        out_shape=[
            jax.ShapeDtypeStruct((T, MLA_HEADS * LANES), BF16),
            jax.ShapeDtypeStruct((T, MLA_HEADS * LANES), BF16),
            jax.ShapeDtypeStruct((T, MLA_W), BF16),
        ],
        compiler_params=pltpu.CompilerParams(
            dimension_semantics=("arbitrary",), vmem_limit_bytes=VMEM_LIMIT),
        name="mla_prep",
    )(pos_row, inv_freq, lat, q_norm_w, kv_norm_w, wq_p, wk_p, wv_p)


def _attn_kernel(q_ref, k_ref, v_ref, o_ref):
    tq = q_ref.shape[0]
    tk = tq
    qi = pl.program_id(2)
    lane = lax.broadcasted_iota(jnp.int32, (1, LANES), 1)
    head1 = lane >= MLA_V
    q = q_ref[...]
    rows = lax.broadcasted_iota(jnp.int32, (tq, tk), 0)
    cols = lax.broadcasted_iota(jnp.int32, (tq, tk), 1)
    causal = cols <= rows

    def block(j, carry, masked):
        m0, l0, m1, l1, acc = carry
        kv_rows = pl.ds(pl.multiple_of(j * tk, tk), tk)
        kblk = k_ref[kv_rows, :]
        vblk = v_ref[kv_rows, :]
        stats = []
        for h, (m_prev, l_prev) in enumerate(((m0, l0), (m1, l1))):
            s = _dot_nt(q[:, h * LANES:(h + 1) * LANES], kblk[:, h * LANES:(h + 1) * LANES])
            if masked:
                s = jnp.where(causal, s, NEG_BIG)
            m_new = jnp.maximum(m_prev, jnp.max(s, axis=-1, keepdims=True))
            alpha = jnp.exp(m_prev - m_new)
            p = jnp.exp(s - m_new)
            l_new = alpha * l_prev + jnp.sum(p, axis=-1, keepdims=True)
            pv = _dot(p.astype(BF16), vblk)
            stats.append((m_new, l_new, alpha, pv))
        (m0, l0, al0, pv0), (m1, l1, al1, pv1) = stats
        acc = acc * jnp.where(head1, al1, al0) + jnp.where(head1, pv1, pv0)
        return m0, l0, m1, l1, acc

    init = (jnp.full((tq, 1), NEG_BIG, F32), jnp.zeros((tq, 1), F32),
            jnp.full((tq, 1), NEG_BIG, F32), jnp.zeros((tq, 1), F32),
            jnp.zeros((tq, LANES), F32))
    carry = lax.fori_loop(0, qi, lambda j, c: block(j, c, False), init)
    m0, l0, m1, l1, acc = block(qi, carry, True)
    o_ref[...] = (acc / jnp.where(head1, l1, l0)).astype(o_ref.dtype)


def _attn(q3, k3, v3):
    B, S, _ = q3.shape
    tq = min(ATT_TQ, S)
    HP = MLA_HEADS // 2
    return pl.pallas_call(
        _attn_kernel,
        grid=(B, HP, S // tq),
        in_specs=[
            pl.BlockSpec((None, tq, 2 * LANES), lambda b, h, i: (b, i, h)),
            pl.BlockSpec((None, S, 2 * LANES), lambda b, h, i: (b, 0, h)),
            pl.BlockSpec((None, S, LANES), lambda b, h, i: (b, 0, h)),
        ],
        out_specs=pl.BlockSpec((None, tq, LANES), lambda b, h, i: (b, i, h)),
        out_shape=jax.ShapeDtypeStruct((B, S, MLA_W), BF16),
        compiler_params=pltpu.CompilerParams(
            dimension_semantics=("arbitrary", "arbitrary", "arbitrary"), vmem_limit_bytes=VMEM_LIMIT),
        name="attn",
    )(q3, k3, v3)


def _tail_kernel(x_ref, yg_ref, om_ref, p_ref, monw_ref, wout_ref, mlpnw_ref, wup_ref, wdown_ref,
                 wpp_ref, postnw_ref, gatenw_ref, wpg_ref, finnw_ref, o_ref):
    ymla = _rms(om_ref[...].astype(F32), monw_ref[...]).astype(BF16)
    h = x_ref[...] + _dot(yg_ref[...], wout_ref[0:GDN_V_W, :]) + _dot(ymla, wout_ref[GDN_V_W:, :])
    u = _rms(h, mlpnw_ref[...]).astype(BF16)
    ff_blk = 1024
    acc = jnp.zeros_like(h)
    for j in range(D_FF // ff_blk):
        cols = slice(j * ff_blk, (j + 1) * ff_blk)
        hid = jnp.maximum(_dot(u, wup_ref[:, cols]), 0.0)
        acc = acc + _dot((hid * hid).astype(BF16), wdown_ref[cols, :])
    h = h + acc
    e = _rms(_dot(p_ref[...].astype(BF16), wpp_ref[...]), postnw_ref[...])
    gate = _sigmoid(_dot(_rms(h, gatenw_ref[...]).astype(BF16), wpg_ref[...]))
    h = h + gate * e
    o_ref[...] = _rms(h, finnw_ref[...])


def _tail(x2, yg, om, p2, mla_out_norm_w, w_out, mlp_norm_w, w_up, w_down, w_ple_proj,
          ple_post_norm_w, ple_gate_norm_w, w_ple_gate, final_norm_w):
    T = x2.shape[0]
    tm = min(TOK_TILE, T)
    tok = lambda w: pl.BlockSpec((tm, w), lambda i: (i, 0))
    const = lambda a: pl.BlockSpec(a.shape, lambda i: (0, 0), pipeline_mode=pl.Buffered(1))
    return pl.pallas_call(
        _tail_kernel,
        grid=(T // tm,),
        in_specs=[
            tok(D_MODEL), tok(GDN_V_W), tok(MLA_W), tok(PLE_DIM),
            const(mla_out_norm_w), const(w_out), const(mlp_norm_w), const(w_up), const(w_down),
            const(w_ple_proj), const(ple_post_norm_w), const(ple_gate_norm_w), const(w_ple_gate),
            const(final_norm_w),
        ],
        out_specs=tok(D_MODEL),
        out_shape=jax.ShapeDtypeStruct((T, D_MODEL), F32),
        compiler_params=pltpu.CompilerParams(
            dimension_semantics=("arbitrary",), vmem_limit_bytes=VMEM_LIMIT),
        name="tail",
    )(x2, yg, om, p2, mla_out_norm_w, w_out, mlp_norm_w, w_up, w_down, w_ple_proj,
      ple_post_norm_w, ple_gate_norm_w, w_ple_gate, final_norm_w)


def _pack_w_in(w):
    o_b = QKV_W + GDN_V_W
    o_cq = o_b + 2 * GDN_HEADS
    n_lat = MLA_Q_RANK + MLA_KV_RANK + MLA_ROPE
    pad = jnp.zeros((D_MODEL, LAT_W - n_lat - 2 * GDN_HEADS), w.dtype)
    return jnp.concatenate([w[:, :o_b], w[:, o_cq:o_cq + n_lat], w[:, o_b:o_cq], pad], axis=1).astype(BF16)


def _pack_mla_weights(w_q_b, w_kv_b):
    wq = w_q_b.reshape(MLA_Q_RANK, MLA_HEADS, MLA_NOPE + MLA_ROPE)
    wq = jnp.pad(wq, ((0, 0), (0, 0), (0, LANES - MLA_NOPE - MLA_ROPE)))
    wkv = w_kv_b.reshape(MLA_KV_RANK, MLA_HEADS, MLA_NOPE + MLA_V)
    wk = jnp.pad(wkv[:, :, :MLA_NOPE], ((0, 0), (0, 0), (0, LANES - MLA_NOPE)))
    wv = wkv[:, :, MLA_NOPE:]
    return (wq.reshape(MLA_Q_RANK, MLA_HEADS * LANES).astype(BF16),
            wk.reshape(MLA_KV_RANK, MLA_HEADS * LANES).astype(BF16),
            wv.reshape(MLA_KV_RANK, MLA_W).astype(BF16))


def kernel(x, p, positions, mix_norm_w, w_in, conv_w, A_log, dt_bias, gdn_norm_w, q_norm_w, w_q_b,
           kv_norm_w, w_kv_b, mla_out_norm_w, w_out, mlp_norm_w, w_up, w_down, w_ple_proj,
           ple_post_norm_w, ple_gate_norm_w, w_ple_gate, final_norm_w):
    B, S, _ = x.shape
    T = B * S
    assert w_in.shape[0] == 1, "one layer"
    row = lambda a: a.reshape(1, -1).astype(F32)
    x2 = x.reshape(T, D_MODEL)

    qkv, z, lat = _inproj(x2, row(mix_norm_w[0]), _pack_w_in(w_in[0]))

    y_gdn = _gdn(qkv.reshape(B, S, QKV_W), z.reshape(B, S, GDN_V_W), lat.reshape(B, S, LAT_W),
                 conv_w[0], A_log[0], dt_bias[0], row(jnp.tile(gdn_norm_w[0], 2)))

    half = MLA_ROPE // 2
    inv_freq = (ROPE_THETA ** (-jnp.arange(0, MLA_ROPE, 2, dtype=F32) / MLA_ROPE)).reshape(half, 1)
    wq_p, wk_p, wv_p = _pack_mla_weights(w_q_b[0], w_kv_b[0])
    q_att, k_att, v_att = _mla_prep(positions.reshape(1, T), inv_freq, lat, row(q_norm_w[0]),
                                    row(kv_norm_w[0]), wq_p, wk_p, wv_p)
    o_mla = _attn(q_att.reshape(B, S, -1), k_att.reshape(B, S, -1), v_att.reshape(B, S, -1))

    out = _tail(x2, y_gdn.reshape(T, GDN_V_W), o_mla.reshape(T, MLA_W), p[0].reshape(T, PLE_DIM),
                row(mla_out_norm_w[0]), w_out[0].astype(BF16), row(mlp_norm_w[0]),
                w_up[0].astype(BF16), w_down[0].astype(BF16), w_ple_proj[0].astype(BF16),
                row(ple_post_norm_w[0]), row(ple_gate_norm_w[0]), w_ple_gate[0].astype(BF16),
                row(final_norm_w))
    return out.reshape(B, S, D_MODEL)
```

```python
import functools

import jax
import jax.numpy as jnp
from jax import lax
from jax.experimental import pallas as pl
from jax.experimental.pallas import tpu as pltpu

F32 = jnp.float32
BF16 = jnp.bfloat16

D_MODEL = 1024
PLE_DIM = 256
GDN_HEADS = 8
GDN_DK = 64
GDN_DV = 64
GDN_QK_W = GDN_HEADS * GDN_DK
GDN_V_W = GDN_HEADS * GDN_DV
GDN_CONV = 4
GDN_CHUNK = 64
MLA_HEADS = 8
MLA_NOPE = 64
MLA_ROPE = 32
MLA_V = 64
MLA_W = MLA_HEADS * MLA_V
MLA_Q_RANK = 256
MLA_KV_RANK = 128
ROPE_THETA = 10000.0
D_FF = 4 * D_MODEL
EPS = 1e-6

LANES = 128
SUBLANES = 8
HEAD_PAIRS = GDN_HEADS // 2
QKV_W = 2 * GDN_QK_W + GDN_V_W
LAT_W = 512
IN_W_PAD = QKV_W + GDN_V_W + LAT_W
KPE_OFF, B_OFF, A_OFF = 0, MLA_ROPE, MLA_ROPE + GDN_HEADS
VMEM_LIMIT = 56 * 1024 * 1024

TOK_TILE = 512
CONV_COLS = 256
GDN_ROWS = 256
SCAN_ROWS = 1024
ATT_TQ = 512
ATT_TK = 512
NEG_BIG = -1e30


def _dot(a, b):
    return jnp.dot(a, b, preferred_element_type=F32)


def _dot_nt(a, b):
    return lax.dot_general(a, b, (((1,), (1,)), ((), ())), preferred_element_type=F32)


def _dot_tn(a, b):
    return lax.dot_general(a, b, (((0,), (0,)), ((), ())), preferred_element_type=F32)


def _split2(x):
    hi = x.astype(BF16)
    lo = (x - hi.astype(F32)).astype(BF16)
    return hi, lo


def _split3(x):
    hi = x.astype(BF16)
    r1 = x - hi.astype(F32)
    mid = r1.astype(BF16)
    lo = (r1 - mid.astype(F32)).astype(BF16)
    return hi, mid, lo


def _dot_exact_rhs(a_bf16, x):
    hi, mid, lo = _split3(x)
    return _dot(a_bf16, hi) + _dot(a_bf16, mid) + _dot(a_bf16, lo)


def _dot_exact_lhs(x, b_bf16):
    hi, mid, lo = _split3(x)
    return _dot(hi, b_bf16) + _dot(mid, b_bf16) + _dot(lo, b_bf16)


def _rms(x, w):
    ms = jnp.mean(x * x, axis=-1, keepdims=True)
    return x * lax.rsqrt(ms + EPS) * w


def _sigmoid(x):
    return 1.0 / (1.0 + jnp.exp(-x))


def _softplus(x):
    return jnp.maximum(x, 0.0) + jnp.log(1.0 + jnp.exp(-jnp.abs(x)))


def _chunk_tril(n):
    r = lax.broadcasted_iota(jnp.int32, (n, n), 0)
    c = lax.broadcasted_iota(jnp.int32, (n, n), 1)
    return jnp.where(((r // GDN_CHUNK) == (c // GDN_CHUNK)) & (c <= r), 1.0, 0.0).astype(BF16)


def _inproj_kernel(x_ref, nw_ref, w_ref, cw_ref, alog_ref, dtb_ref, qkv_ref, z_ref, lat_ref, halo_ref,
                   *, tiles_per_seq):
    tm = x_ref.shape[0]
    first = (pl.program_id(0) % tiles_per_seq) == 0
    u = _rms(x_ref[...], nw_ref[...]).astype(BF16)

    for c in range(QKV_W // CONV_COLS):
        cols = slice(c * CONV_COLS, (c + 1) * CONV_COLS)
        raw = _dot(u, w_ref[:, cols])
        prev = jnp.where(first, 0.0, halo_ref[:, cols])
        halo_ref[:, cols] = raw[tm - SUBLANES:, :]
        ext = jnp.concatenate([prev, raw], axis=0)
        cw = cw_ref[:, cols]
        acc = raw * cw[GDN_CONV - 1:GDN_CONV, :]
        for kk in range(GDN_CONV - 1):
            shifted = pltpu.roll(ext, GDN_CONV - 1 - kk, axis=0)[SUBLANES:, :]
            acc = acc + shifted * cw[kk:kk + 1, :]
        qkv_ref[:, cols] = acc * _sigmoid(acc)

    zf = _dot(u, w_ref[:, QKV_W:QKV_W + GDN_V_W])
    z_ref[...] = (zf * _sigmoid(zf)).astype(z_ref.dtype)

    lat = _dot(u, w_ref[:, QKV_W + GDN_V_W:IN_W_PAD])
    lat_ref[:, :LAT_W - LANES] = lat[:, :LAT_W - LANES]
    l3 = lat[:, LAT_W - LANES:]
    lane = lax.broadcasted_iota(jnp.int32, (1, LANES), 1)
    is_b = (lane >= B_OFF) & (lane < B_OFF + GDN_HEADS)
    is_a = (lane >= A_OFF) & (lane < A_OFF + GDN_HEADS)
    g_raw = jnp.where(is_a, -(jnp.exp(alog_ref[...]) * _softplus(l3 + dtb_ref[...])), 0.0)
    g_cum = _dot_exact_rhs(_chunk_tril(tm), g_raw)
    lat_ref[:, LAT_W - LANES:] = jnp.where(is_b, _sigmoid(l3), jnp.where(is_a, g_cum, l3))


def _inproj(x2, mix_norm_w, w_in_p, conv_w, alog_vec, dtb_vec, seq_len):
    T = x2.shape[0]
    tm = min(TOK_TILE, seq_len)
    assert seq_len % tm == 0 and tm % GDN_CHUNK == 0
    const = lambda a: pl.BlockSpec(a.shape, lambda i: (0, 0))
    return pl.pallas_call(
        functools.partial(_inproj_kernel, tiles_per_seq=seq_len // tm),
        grid=(T // tm,),
        in_specs=[
            pl.BlockSpec((tm, D_MODEL), lambda i: (i, 0)),
            const(mix_norm_w), const(w_in_p), const(conv_w), const(alog_vec), const(dtb_vec),
        ],
        out_specs=[
            pl.BlockSpec((tm, QKV_W), lambda i: (i, 0)),
            pl.BlockSpec((tm, GDN_V_W), lambda i: (i, 0)),
            pl.BlockSpec((tm, LAT_W), lambda i: (i, 0)),
        ],
        out_shape=[
            jax.ShapeDtypeStruct((T, QKV_W), F32),
            jax.ShapeDtypeStruct((T, GDN_V_W), BF16),
            jax.ShapeDtypeStruct((T, LAT_W), F32),
        ],
        scratch_shapes=[pltpu.VMEM((SUBLANES, QKV_W), F32)],
        compiler_params=pltpu.CompilerParams(
            dimension_semantics=("arbitrary",), vmem_limit_bytes=VMEM_LIMIT),
        name="inproj",
    )(x2, mix_norm_w, w_in_p, conv_w, alog_vec, dtb_vec)


def _gdn_prep_kernel(q_ref, k_ref, v_ref, lat_ref, u_ref, w_ref, qd_ref, kd_ref, a_ref, sd_ref):
    C = GDN_CHUNK
    S = q_ref.shape[0]
    R = min(GDN_ROWS, S)
    G = R // C
    hp = pl.program_id(1)

    lane = lax.broadcasted_iota(jnp.int32, (1, LANES), 1)
    head1 = lane >= C
    m0 = jnp.where(head1, 0.0, 1.0).astype(BF16)
    m1 = jnp.where(head1, 1.0, 0.0).astype(BF16)
    r128 = lax.broadcasted_iota(jnp.int32, (LANES, LANES), 0)
    c128 = lax.broadcasted_iota(jnp.int32, (LANES, LANES), 1)
    ones_bd = jnp.where((r128 // C) == (c128 // C), 1.0, 0.0).astype(BF16)
    rC = lax.broadcasted_iota(jnp.int32, (C, LANES), 0)
    cC = lax.broadcasted_iota(jnp.int32, (C, LANES), 1) % C
    eye2 = rC == cC
    tril2 = rC >= cC
    strict2 = rC > cC
    eye2f = jnp.where(eye2, 1.0, 0.0)
    ones_cc = jnp.ones((C, C), BF16)
    sel_r = lax.broadcasted_iota(jnp.int32, (LANES, 2 * LANES), 0)
    sel_c = lax.broadcasted_iota(jnp.int32, (LANES, 2 * LANES), 1)
    sel_src = jnp.where(sel_c < LANES, B_OFF, A_OFF) + 2 * hp + ((sel_c % LANES) // C)
    sel = jnp.where(sel_r == sel_src, 1.0, 0.0).astype(BF16)

    def bd16(m):
        return jnp.concatenate([m * m0, m * m1], axis=0)

    def bd_parts(m):
        hi, lo = _split2(m)
        return bd16(hi), bd16(lo)

    def dot_hi(a, b_hi, b_lo):
        a_hi, a_lo = _split2(a)
        return _dot(a_hi, b_hi) + _dot(a_hi, b_lo) + _dot(a_lo, b_hi)

    def tile(t, carry):
        rows = pl.ds(pl.multiple_of(t * R, R), R)
        q = q_ref[rows, :]
        k = k_ref[rows, :]
        v = v_ref[rows, :]
        q = q * lax.rsqrt(_dot((q * q).astype(BF16), ones_bd) + EPS) * (GDN_DK ** -0.5)
        k = k * lax.rsqrt(_dot((k * k).astype(BF16), ones_bd) + EPS)
        bg = _dot_exact_lhs(lat_ref[rows, :], sel)
        beta = bg[:, :LANES]
        gc = bg[:, LANES:]
        eg = jnp.exp(gc)
        kb = k * beta
        vb16 = (v * beta).astype(BF16)
        kbg16 = (kb * eg).astype(BF16)
        qd_ref[rows, :] = (q * eg).astype(BF16)
        k16 = k.astype(BF16)
        q16 = q.astype(BF16)

        ch = lambda arr, g: arr[g * C:(g + 1) * C, :]
        rng = range(G)
        kkqk = [_dot_nt(jnp.concatenate([ch(k16, g), ch(q16, g)], axis=0), bd16(ch(k16, g))) for g in rng]
        g_row = [_dot_exact_rhs(ones_cc, jnp.where(eye2, ch(gc, g), 0.0)) for g in rng]
        dm = [jnp.where(tril2, jnp.exp(jnp.where(tril2, ch(gc, g) - g_row[g], 0.0)), 0.0) for g in rng]
        for g in rng:
            a_ref[pl.ds(pl.multiple_of(t * R + g * C, C), C), :] = (kkqk[g][C:] * dm[g]).astype(BF16)
            g_last = ch(gc, g)[C - 1:C, :]
            kd_ref[pl.ds(pl.multiple_of(t * R + g * C, C), C), :] = (
                ch(k, g) * jnp.exp(g_last - ch(gc, g))).astype(BF16)
            sd_ref[pl.ds(pl.multiple_of((t * G + g) * SUBLANES, SUBLANES), SUBLANES), :] = jnp.broadcast_to(
                jnp.exp(g_last), (SUBLANES, LANES))
        x = [-jnp.where(strict2, kkqk[g][:C] * dm[g] * ch(beta, g), 0.0) for g in rng]
        p = [eye2f + x[g] for g in rng]
        y = [dot_hi(x[g], *bd_parts(x[g])) for g in rng]
        for lvl in range(1, 6):
            if lvl < 5:
                rhs = []
                for g in rng:
                    p_hi, p_lo = bd_parts(p[g])
                    y_hi, y_lo = bd_parts(y[g])
                    rhs.append((jnp.concatenate([p_hi, y_hi], axis=1), jnp.concatenate([p_lo, y_lo], axis=1)))
                res = [dot_hi(y[g], *rhs[g]) for g in rng]
                p = [p[g] + res[g][:, :LANES] for g in rng]
                y = [res[g][:, LANES:] for g in rng]
            else:
                p = [p[g] + dot_hi(y[g], *bd_parts(p[g])) for g in rng]
        uw = [_dot(p[g].astype(BF16), jnp.concatenate([bd16(ch(vb16, g)), bd16(ch(kbg16, g))], axis=1))
              for g in rng]
        for g in rng:
            dst = pl.ds(pl.multiple_of(t * R + g * C, C), C)
            u_ref[dst, :] = uw[g][:, :LANES]
            w_ref[dst, :] = uw[g][:, LANES:].astype(BF16)
        return carry

    lax.fori_loop(0, S // R, tile, 0)


def _gdn_prep(qkv3, lat3):
    B, S, _ = qkv3.shape
    HP = HEAD_PAIRS
    seq_blk = lambda off: pl.BlockSpec((None, S, LANES), lambda b, h, off=off: (b, 0, off + h))
    n_sd = S // GDN_CHUNK * SUBLANES
    return pl.pallas_call(
        _gdn_prep_kernel,
        grid=(B, HP),
        in_specs=[
            seq_blk(0), seq_blk(HP), seq_blk(2 * HP),
            pl.BlockSpec((None, S, LANES), lambda b, h: (b, 0, LAT_W // LANES - 1)),
        ],
        out_specs=[seq_blk(0)] * 5 + [pl.BlockSpec((None, n_sd, LANES), lambda b, h: (b, 0, h))],
        out_shape=[
            jax.ShapeDtypeStruct((B, S, GDN_V_W), F32),
            jax.ShapeDtypeStruct((B, S, GDN_V_W), BF16),
            jax.ShapeDtypeStruct((B, S, GDN_V_W), BF16),
            jax.ShapeDtypeStruct((B, S, GDN_V_W), BF16),
            jax.ShapeDtypeStruct((B, S, GDN_V_W), BF16),
            jax.ShapeDtypeStruct((B, n_sd, GDN_V_W), F32),
        ],
        compiler_params=pltpu.CompilerParams(
            dimension_semantics=("arbitrary", "arbitrary"), vmem_limit_bytes=VMEM_LIMIT),
        name="gdn_prep",
    )(qkv3, qkv3, qkv3, lat3)


def _gdn_scan_kernel(u_ref, w_ref, qd_ref, kd_ref, a_ref, sd_ref, z_ref, nw_ref, o_ref, state_ref):
    C = GDN_CHUNK
    rt = u_ref.shape[0]

    @pl.when(pl.program_id(1) == 0)
    def _():
        state_ref[...] = jnp.zeros_like(state_ref)

    lane = lax.broadcasted_iota(jnp.int32, (1, LANES), 1)
    head1 = lane >= C
    m0 = jnp.where(head1, 0.0, 1.0).astype(BF16)
    m1 = jnp.where(head1, 1.0, 0.0).astype(BF16)
    r128 = lax.broadcasted_iota(jnp.int32, (LANES, LANES), 0)
    c128 = lax.broadcasted_iota(jnp.int32, (LANES, LANES), 1)
    bdmask = (r128 // C) == (c128 // C)
    ones_bd = jnp.where(bdmask, 1.0, 0.0).astype(BF16)

    def bd16(m):
        return jnp.concatenate([m * m0, m * m1], axis=0)

    def chunk(n, carry):
        rows = pl.ds(pl.multiple_of(n * C, C), C)
        sd_rows = pl.ds(pl.multiple_of(n * SUBLANES, SUBLANES), SUBLANES)
        pairs = range(HEAD_PAIRS)
        blk = lambda hp: slice(hp * LANES, (hp + 1) * LANES)
        state = [state_ref[hp] for hp in pairs]
        res = [_dot(jnp.concatenate([w_ref[rows, blk(hp)], qd_ref[rows, blk(hp)]], axis=0),
                    state[hp].astype(BF16)) for hp in pairs]
        v_new = [(u_ref[rows, blk(hp)] - res[hp][:C]).astype(BF16) for hp in pairs]
        upd = [_dot_tn(kd_ref[rows, blk(hp)], v_new[hp]) for hp in pairs]
        o = [res[hp][C:] + _dot(a_ref[rows, blk(hp)], bd16(v_new[hp])) for hp in pairs]
        for hp in pairs:
            sd = sd_ref[sd_rows, blk(hp)][0:1, :]
            state_ref[hp] = state[hp] * sd + jnp.where(bdmask, upd[hp], 0.0)
        for hp in pairs:
            ms = _dot((o[hp] * o[hp]).astype(BF16), ones_bd) * (1.0 / GDN_DV)
            y = o[hp] * lax.rsqrt(ms + EPS) * nw_ref[:, blk(hp)] * z_ref[rows, blk(hp)].astype(F32)
            o_ref[rows, blk(hp)] = y.astype(o_ref.dtype)
        return carry

    lax.fori_loop(0, rt // C, chunk, 0)


def _gdn_scan(u, w, qd, kd, a, sd, zg, norm_w8):
    B, S, W = u.shape
    rt = min(SCAN_ROWS, S)
    n_sd = rt // GDN_CHUNK * SUBLANES
    seq = pl.BlockSpec((None, rt, W), lambda b, t: (b, t, 0))
    return pl.pallas_call(
        _gdn_scan_kernel,
        grid=(B, S // rt),
        in_specs=[seq, seq, seq, seq, seq,
                  pl.BlockSpec((None, n_sd, W), lambda b, t: (b, t, 0)),
                  seq,
                  pl.BlockSpec((1, W), lambda b, t: (0, 0))],
        out_specs=seq,
        out_shape=jax.ShapeDtypeStruct((B, S, W), BF16),
        scratch_shapes=[pltpu.VMEM((HEAD_PAIRS, LANES, LANES), F32)],
        compiler_params=pltpu.CompilerParams(
            dimension_semantics=("arbitrary", "arbitrary"), vmem_limit_bytes=VMEM_LIMIT),
        name="gdn_scan",
    )(u, w, qd, kd, a, sd, zg, norm_w8)


def _mla_prep_kernel(pos_ref, freq_ref, lat_ref, qnw_ref, kvnw_ref, wq_ref, wk_ref, wv_ref,
                     q_ref, k_ref, v_ref):
    tm = lat_ref.shape[0]
    half = MLA_ROPE // 2
    scale = (MLA_NOPE + MLA_ROPE) ** -0.5
    ang = freq_ref[...] * pos_ref[...].astype(F32)
    cos_t = jnp.cos(ang)
    sin_t = jnp.sin(ang)
    zeros = lambda n: jnp.zeros((n, tm), F32)
    cos_tab = jnp.concatenate([zeros(MLA_NOPE), cos_t, cos_t, zeros(LANES - MLA_NOPE - MLA_ROPE)], axis=0).T
    sin_lo = jnp.concatenate([zeros(MLA_NOPE), -sin_t, zeros(LANES - MLA_NOPE - half)], axis=0).T
    sin_hi = jnp.concatenate([zeros(MLA_NOPE + half), sin_t, zeros(LANES - MLA_NOPE - MLA_ROPE)], axis=0).T
    lane = lax.broadcasted_iota(jnp.int32, (1, LANES), 1)
    nope = jnp.where(lane < MLA_NOPE, 1.0, 0.0)

    def rot(xh, c):
        return xh * c + pltpu.roll(xh, LANES - half, axis=1) * sin_lo + pltpu.roll(xh, half, axis=1) * sin_hi

    lat = lat_ref[...]
    cq = _rms(lat[:, 0:MLA_Q_RANK], qnw_ref[...]).astype(BF16)
    ckv = _rms(lat[:, MLA_Q_RANK:MLA_Q_RANK + MLA_KV_RANK], kvnw_ref[...]).astype(BF16)
    q = _dot(cq, wq_ref[...])
    k_nope = _dot(ckv, wk_ref[...])
    v_ref[...] = _dot(ckv, wv_ref[...]).astype(v_ref.dtype)

    kpe = pltpu.roll(lat[:, LAT_W - LANES:], MLA_NOPE - KPE_OFF, axis=1)
    k_rope = rot(kpe, cos_tab)
    c_q = cos_tab + nope
    for h in range(MLA_HEADS):
        blk = slice(h * LANES, (h + 1) * LANES)
        q_ref[:, blk] = (rot(q[:, blk], c_q) * scale).astype(q_ref.dtype)
        k_ref[:, blk] = (k_nope[:, blk] + k_rope).astype(k_ref.dtype)


def _mla_prep(pos_row, inv_freq, lat, q_norm_w, kv_norm_w, wq_p, wk_p, wv_p):
    T = lat.shape[0]
    tm = min(TOK_TILE, T)
    full = lambda a: pl.BlockSpec(a.shape, lambda i: (0,) * a.ndim)
    return pl.pallas_call(
        _mla_prep_kernel,
        grid=(T // tm,),
        in_specs=[
            pl.BlockSpec((1, tm), lambda i: (0, i)),
            full(inv_freq),
            pl.BlockSpec((tm, LAT_W), lambda i: (i, 0)),
            full(q_norm_w), full(kv_norm_w), full(wq_p), full(wk_p), full(wv_p),
        ],
        out_specs=[
            pl.BlockSpec((tm, MLA_HEADS * LANES), lambda i: (i, 0)),
            pl.BlockSpec((tm, MLA_HEADS * LANES), lambda i: (i, 0)),
            pl.BlockSpec((tm, MLA_W), lambda i: (i, 0)),
        ],
        out_shape=[
            jax.ShapeDtypeStruct((T, MLA_HEADS * LANES), BF16),
            jax.ShapeDtypeStruct((T, MLA_HEADS * LANES), BF16),
            jax.ShapeDtypeStruct((T, MLA_W), BF16),
        ],
        compiler_params=pltpu.CompilerParams(
            dimension_semantics=("arbitrary",), vmem_limit_bytes=VMEM_LIMIT),
        name="mla_prep",
    )(pos_row, inv_freq, lat, q_norm_w, kv_norm_w, wq_p, wk_p, wv_p)


def _attn_kernel(q_ref, k_ref, v_ref, o_ref):
    tq = q_ref.shape[0]
    tk = tq
    qi = pl.program_id(2)
    lane = lax.broadcasted_iota(jnp.int32, (1, LANES), 1)
    head1 = lane >= MLA_V
    q = q_ref[...]
    rows = lax.broadcasted_iota(jnp.int32, (tq, tk), 0)
    cols = lax.broadcasted_iota(jnp.int32, (tq, tk), 1)
    causal = cols <= rows

    def block(j, carry, masked):
        m0, l0, m1, l1, acc = carry
        kv_rows = pl.ds(pl.multiple_of(j * tk, tk), tk)
        kblk = k_ref[kv_rows, :]
        vblk = v_ref[kv_rows, :]
        stats = []
        for h, (m_prev, l_prev) in enumerate(((m0, l0), (m1, l1))):
            s = _dot_nt(q[:, h * LANES:(h + 1) * LANES], kblk[:, h * LANES:(h + 1) * LANES])
            if masked:
                s = jnp.where(causal, s, NEG_BIG)
            m_new = jnp.maximum(m_prev, jnp.max(s, axis=-1, keepdims=True))
            alpha = jnp.exp(m_prev - m_new)
            p = jnp.exp(s - m_new)
            l_new = alpha * l_prev + jnp.sum(p, axis=-1, keepdims=True)
            pv = _dot(p.astype(BF16), vblk)
            stats.append((m_new, l_new, alpha, pv))
        (m0, l0, al0, pv0), (m1, l1, al1, pv1) = stats
        acc = acc * jnp.where(head1, al1, al0) + jnp.where(head1, pv1, pv0)
        return m0, l0, m1, l1, acc

    init = (jnp.full((tq, 1), NEG_BIG, F32), jnp.zeros((tq, 1), F32),
            jnp.full((tq, 1), NEG_BIG, F32), jnp.zeros((tq, 1), F32),
            jnp.zeros((tq, LANES), F32))
    carry = lax.fori_loop(0, qi, lambda j, c: block(j, c, False), init)
    m0, l0, m1, l1, acc = block(qi, carry, True)
    o_ref[...] = (acc / jnp.where(head1, l1, l0)).astype(o_ref.dtype)


def _attn(q3, k3, v3):
    B, S, _ = q3.shape
    tq = min(ATT_TQ, S)
    HP = MLA_HEADS // 2
    return pl.pallas_call(
        _attn_kernel,
        grid=(B, HP, S // tq),
        in_specs=[
            pl.BlockSpec((None, tq, 2 * LANES), lambda b, h, i: (b, i, h)),
            pl.BlockSpec((None, S, 2 * LANES), lambda b, h, i: (b, 0, h)),
            pl.BlockSpec((None, S, LANES), lambda b, h, i: (b, 0, h)),
        ],
        out_specs=pl.BlockSpec((None, tq, LANES), lambda b, h, i: (b, i, h)),
        out_shape=jax.ShapeDtypeStruct((B, S, MLA_W), BF16),
        compiler_params=pltpu.CompilerParams(
            dimension_semantics=("arbitrary", "arbitrary", "arbitrary"), vmem_limit_bytes=VMEM_LIMIT),
        name="attn",
    )(q3, k3, v3)


def _tail_kernel(x_ref, yg_ref, om_ref, p_ref, monw_ref, wout_ref, mlpnw_ref, wup_ref, wdown_ref,
                 wpp_ref, postnw_ref, gatenw_ref, wpg_ref, finnw_ref, o_ref):
    ymla = _rms(om_ref[...].astype(F32), monw_ref[...]).astype(BF16)
    h = x_ref[...] + _dot(yg_ref[...], wout_ref[0:GDN_V_W, :]) + _dot(ymla, wout_ref[GDN_V_W:, :])
    u = _rms(h, mlpnw_ref[...]).astype(BF16)
    ff_blk = 1024
    acc = jnp.zeros_like(h)
    for j in range(D_FF // ff_blk):
        cols = slice(j * ff_blk, (j + 1) * ff_blk)
        hid = jnp.maximum(_dot(u, wup_ref[:, cols]), 0.0)
        acc = acc + _dot((hid * hid).astype(BF16), wdown_ref[cols, :])
    h = h + acc
    e = _rms(_dot(p_ref[...].astype(BF16), wpp_ref[...]), postnw_ref[...])
    gate = _sigmoid(_dot(_rms(h, gatenw_ref[...]).astype(BF16), wpg_ref[...]))
    h = h + gate * e
    o_ref[...] = _rms(h, finnw_ref[...])


def _tail(x2, yg, om, p2, mla_out_norm_w, w_out, mlp_norm_w, w_up, w_down, w_ple_proj,
          ple_post_norm_w, ple_gate_norm_w, w_ple_gate, final_norm_w):
    T = x2.shape[0]
    tm = min(TOK_TILE, T)
    tok = lambda w: pl.BlockSpec((tm, w), lambda i: (i, 0))
    const = lambda a: pl.BlockSpec(a.shape, lambda i: (0, 0), pipeline_mode=pl.Buffered(1))
    return pl.pallas_call(
        _tail_kernel,
        grid=(T // tm,),
        in_specs=[
            tok(D_MODEL), tok(GDN_V_W), tok(MLA_W), tok(PLE_DIM),
            const(mla_out_norm_w), const(w_out), const(mlp_norm_w), const(w_up), const(w_down),
            const(w_ple_proj), const(ple_post_norm_w), const(ple_gate_norm_w), const(w_ple_gate),
            const(final_norm_w),
        ],
        out_specs=tok(D_MODEL),
        out_shape=jax.ShapeDtypeStruct((T, D_MODEL), F32),
        compiler_params=pltpu.CompilerParams(
            dimension_semantics=("arbitrary",), vmem_limit_bytes=VMEM_LIMIT),
        name="tail",
    )(x2, yg, om, p2, mla_out_norm_w, w_out, mlp_norm_w, w_up, w_down, w_ple_proj,
      ple_post_norm_w, ple_gate_norm_w, w_ple_gate, final_norm_w)


def _pack_w_in(w):
    o_b = QKV_W + GDN_V_W
    o_cq = o_b + 2 * GDN_HEADS
    n_lat = MLA_Q_RANK + MLA_KV_RANK + MLA_ROPE
    pad = jnp.zeros((D_MODEL, LAT_W - n_lat - 2 * GDN_HEADS), w.dtype)
    return jnp.concatenate([w[:, :o_b], w[:, o_cq:o_cq + n_lat], w[:, o_b:o_cq], pad], axis=1).astype(BF16)


def _gate_lane_vector(per_head):
    return jnp.zeros((1, LANES), F32).at[0, A_OFF:A_OFF + GDN_HEADS].set(per_head.astype(F32))


def _pack_mla_weights(w_q_b, w_kv_b):
    wq = w_q_b.reshape(MLA_Q_RANK, MLA_HEADS, MLA_NOPE + MLA_ROPE)
    wq = jnp.pad(wq, ((0, 0), (0, 0), (0, LANES - MLA_NOPE - MLA_ROPE)))
    wkv = w_kv_b.reshape(MLA_KV_RANK, MLA_HEADS, MLA_NOPE + MLA_V)
    wk = jnp.pad(wkv[:, :, :MLA_NOPE], ((0, 0), (0, 0), (0, LANES - MLA_NOPE)))
    wv = wkv[:, :, MLA_NOPE:]
    return (wq.reshape(MLA_Q_RANK, MLA_HEADS * LANES).astype(BF16),
            wk.reshape(MLA_KV_RANK, MLA_HEADS * LANES).astype(BF16),
            wv.reshape(MLA_KV_RANK, MLA_W).astype(BF16))


def kernel(x, p, positions, mix_norm_w, w_in, conv_w, A_log, dt_bias, gdn_norm_w, q_norm_w, w_q_b,
           kv_norm_w, w_kv_b, mla_out_norm_w, w_out, mlp_norm_w, w_up, w_down, w_ple_proj,
           ple_post_norm_w, ple_gate_norm_w, w_ple_gate, final_norm_w):
    B, S, _ = x.shape
    T = B * S
    assert w_in.shape[0] == 1, "one layer"
    row = lambda a: a.reshape(1, -1).astype(F32)
    x2 = x.reshape(T, D_MODEL)

    qkv, zg, lat = _inproj(x2, row(mix_norm_w[0]), _pack_w_in(w_in[0]), conv_w[0].astype(F32),
                           _gate_lane_vector(A_log[0]), _gate_lane_vector(dt_bias[0]), S)

    factors = _gdn_prep(qkv.reshape(B, S, QKV_W), lat.reshape(B, S, LAT_W))
    y_gdn = _gdn_scan(*factors, zg.reshape(B, S, GDN_V_W), row(jnp.tile(gdn_norm_w[0], GDN_HEADS)))

    half = MLA_ROPE // 2
    inv_freq = (ROPE_THETA ** (-jnp.arange(0, MLA_ROPE, 2, dtype=F32) / MLA_ROPE)).reshape(half, 1)
    wq_p, wk_p, wv_p = _pack_mla_weights(w_q_b[0], w_kv_b[0])
    q_att, k_att, v_att = _mla_prep(positions.reshape(1, T), inv_freq, lat, row(q_norm_w[0]),
                                    row(kv_norm_w[0]), wq_p, wk_p, wv_p)
    o_mla = _attn(q_att.reshape(B, S, -1), k_att.reshape(B, S, -1), v_att.reshape(B, S, -1))

    out = _tail(x2, y_gdn.reshape(T, GDN_V_W), o_mla.reshape(T, MLA_W), p[0].reshape(T, PLE_DIM),
                row(mla_out_norm_w[0]), w_out[0].astype(BF16), row(mlp_norm_w[0]),
                w_up[0].astype(BF16), w_down[0].astype(BF16), w_ple_proj[0].astype(BF16),
                row(ple_post_norm_w[0]), row(ple_gate_norm_w[0]), w_ple_gate[0].astype(BF16),
                row(final_norm_w))
    return out.reshape(B, S, D_MODEL)
```

```python
import functools

import jax
import jax.numpy as jnp
from jax import lax
from jax.experimental import pallas as pl
from jax.experimental.pallas import tpu as pltpu

F32 = jnp.float32
BF16 = jnp.bfloat16

D_MODEL = 1024
PLE_DIM = 256
GDN_HEADS = 8
GDN_DK = 64
GDN_DV = 64
GDN_QK_W = GDN_HEADS * GDN_DK
GDN_V_W = GDN_HEADS * GDN_DV
GDN_CONV = 4
GDN_CHUNK = 64
MLA_HEADS = 8
MLA_NOPE = 64
MLA_ROPE = 32
MLA_V = 64
MLA_W = MLA_HEADS * MLA_V
MLA_Q_RANK = 256
MLA_KV_RANK = 128
ROPE_THETA = 10000.0
D_FF = 4 * D_MODEL
EPS = 1e-6

LANES = 128
SUBLANES = 8
HEAD_PAIRS = GDN_HEADS // 2
QKV_W = 2 * GDN_QK_W + GDN_V_W
LAT_W = 512
IN_W_PAD = QKV_W + GDN_V_W + LAT_W
KPE_OFF, B_OFF, A_OFF = 0, MLA_ROPE, MLA_ROPE + GDN_HEADS
VMEM_LIMIT = 56 * 1024 * 1024

TOK_TILE = 512
CONV_COLS = 256
GDN_ROWS = 512
SCAN_ROWS = 1024
ATT_TQ = 512
ATT_TK = 512
NEG_BIG = -1e30
LOG2E = 1.4426950408889634


def _dot(a, b):
    return jnp.dot(a, b, preferred_element_type=F32)


def _dot_nt(a, b):
    return lax.dot_general(a, b, (((1,), (1,)), ((), ())), preferred_element_type=F32)


def _dot_tn(a, b):
    return lax.dot_general(a, b, (((0,), (0,)), ((), ())), preferred_element_type=F32)


def _split2(x):
    hi = x.astype(BF16)
    lo = (x - hi.astype(F32)).astype(BF16)
    return hi, lo


def _split3(x):
    hi = x.astype(BF16)
    r1 = x - hi.astype(F32)
    mid = r1.astype(BF16)
    lo = (r1 - mid.astype(F32)).astype(BF16)
    return hi, mid, lo


def _dot_exact_rhs(a_bf16, x):
    hi, mid, lo = _split3(x)
    return _dot(a_bf16, hi) + _dot(a_bf16, mid) + _dot(a_bf16, lo)


def _dot_exact_lhs(x, b_bf16):
    hi, mid, lo = _split3(x)
    return _dot(hi, b_bf16) + _dot(mid, b_bf16) + _dot(lo, b_bf16)


def _rms(x, w):
    ms = jnp.mean(x * x, axis=-1, keepdims=True)
    return x * lax.rsqrt(ms + EPS) * w


def _sigmoid(x):
    return 1.0 / (1.0 + jnp.exp(-x))


def _silu(x):
    h = 0.5 * x
    return h + h * jnp.tanh(h)


def _softplus(x):
    return jnp.maximum(x, 0.0) + jnp.log(1.0 + jnp.exp(-jnp.abs(x)))


def _chunk_tril(n):
    r = lax.broadcasted_iota(jnp.int32, (n, n), 0)
    c = lax.broadcasted_iota(jnp.int32, (n, n), 1)
    return jnp.where(((r // GDN_CHUNK) == (c // GDN_CHUNK)) & (c <= r), 1.0, 0.0).astype(BF16)


def _inproj_kernel(x_ref, nw_ref, w_ref, cw_ref, alog_ref, dtb_ref, qkv_ref, z_ref, lat_ref, halo_ref,
                   *, tiles_per_seq):
    tm = x_ref.shape[0]
    first = (pl.program_id(0) % tiles_per_seq) == 0
    u = _rms(x_ref[...], nw_ref[...]).astype(BF16)

    for c in range(QKV_W // CONV_COLS):
        cols = slice(c * CONV_COLS, (c + 1) * CONV_COLS)
        raw = _dot(u, w_ref[:, cols])
        prev = jnp.where(first, 0.0, halo_ref[:, cols])
        halo_ref[:, cols] = raw[tm - SUBLANES:, :]
        ext = jnp.concatenate([prev, raw], axis=0)
        cw = cw_ref[:, cols]
        acc = raw * cw[GDN_CONV - 1:GDN_CONV, :]
        for kk in range(GDN_CONV - 1):
            shifted = pltpu.roll(ext, GDN_CONV - 1 - kk, axis=0)[SUBLANES:, :]
            acc = acc + shifted * cw[kk:kk + 1, :]
        qkv_ref[:, cols] = _silu(acc)

    zf = _dot(u, w_ref[:, QKV_W:QKV_W + GDN_V_W])
    z_ref[...] = _silu(zf).astype(z_ref.dtype)

    lat = _dot(u, w_ref[:, QKV_W + GDN_V_W:IN_W_PAD])
    lat_ref[:, :LAT_W - LANES] = lat[:, :LAT_W - LANES]
    l3 = lat[:, LAT_W - LANES:]
    lane = lax.broadcasted_iota(jnp.int32, (1, LANES), 1)
    is_b = (lane >= B_OFF) & (lane < B_OFF + GDN_HEADS)
    is_a = (lane >= A_OFF) & (lane < A_OFF + GDN_HEADS)
    g_raw = jnp.where(is_a, -(jnp.exp(alog_ref[...]) * _softplus(l3 + dtb_ref[...])), 0.0)
    g_cum = _dot_exact_rhs(_chunk_tril(tm), g_raw)
    lat_ref[:, LAT_W - LANES:] = jnp.where(is_b, _sigmoid(l3), jnp.where(is_a, g_cum, l3))


def _inproj(x2, mix_norm_w, w_in_p, conv_w, alog_vec, dtb_vec, seq_len):
    T = x2.shape[0]
    tm = min(TOK_TILE, seq_len)
    assert seq_len % tm == 0 and tm % GDN_CHUNK == 0
    const = lambda a: pl.BlockSpec(a.shape, lambda i: (0, 0))
    return pl.pallas_call(
        functools.partial(_inproj_kernel, tiles_per_seq=seq_len // tm),
        grid=(T // tm,),
        in_specs=[
            pl.BlockSpec((tm, D_MODEL), lambda i: (i, 0)),
            const(mix_norm_w), const(w_in_p), const(conv_w), const(alog_vec), const(dtb_vec),
        ],
        out_specs=[
            pl.BlockSpec((tm, QKV_W), lambda i: (i, 0)),
            pl.BlockSpec((tm, GDN_V_W), lambda i: (i, 0)),
            pl.BlockSpec((tm, LAT_W), lambda i: (i, 0)),
        ],
        out_shape=[
            jax.ShapeDtypeStruct((T, QKV_W), F32),
            jax.ShapeDtypeStruct((T, GDN_V_W), BF16),
            jax.ShapeDtypeStruct((T, LAT_W), F32),
        ],
        scratch_shapes=[pltpu.VMEM((SUBLANES, QKV_W), F32)],
        compiler_params=pltpu.CompilerParams(
            dimension_semantics=("arbitrary",), vmem_limit_bytes=VMEM_LIMIT),
        name="inproj",
    )(x2, mix_norm_w, w_in_p, conv_w, alog_vec, dtb_vec)


def _gdn_prep_kernel(q_ref, k_ref, v_ref, lat_ref, u_ref, w_ref, qd_ref, kd_ref, a_ref, sd_ref):
    C = GDN_CHUNK
    S = q_ref.shape[0]
    R = min(GDN_ROWS, S)
    G = R // C
    hp = pl.program_id(1)

    lane = lax.broadcasted_iota(jnp.int32, (1, LANES), 1)
    head1 = lane >= C
    m0 = jnp.where(head1, 0.0, 1.0).astype(BF16)
    m1 = jnp.where(head1, 1.0, 0.0).astype(BF16)
    r128 = lax.broadcasted_iota(jnp.int32, (LANES, LANES), 0)
    c128 = lax.broadcasted_iota(jnp.int32, (LANES, LANES), 1)
    ones_bd = jnp.where((r128 // C) == (c128 // C), 1.0, 0.0).astype(BF16)
    rC = lax.broadcasted_iota(jnp.int32, (C, LANES), 0)
    cC = lax.broadcasted_iota(jnp.int32, (C, LANES), 1) % C
    eye2 = rC == cC
    tril2 = rC >= cC
    strict2 = rC > cC
    eye2f = jnp.where(eye2, 1.0, 0.0)
    ones_cc = jnp.ones((C, C), BF16)
    sel_r = lax.broadcasted_iota(jnp.int32, (LANES, 2 * LANES), 0)
    sel_c = lax.broadcasted_iota(jnp.int32, (LANES, 2 * LANES), 1)
    sel_src = jnp.where(sel_c < LANES, B_OFF, A_OFF) + 2 * hp + ((sel_c % LANES) // C)
    sel = jnp.where(sel_r == sel_src, 1.0, 0.0).astype(BF16)

    def bd16(m):
        return jnp.concatenate([m * m0, m * m1], axis=0)

    def bd_parts(m):
        hi, lo = _split2(m)
        return bd16(hi), bd16(lo)

    def dot_hi(a, b_hi, b_lo):
        a_hi, a_lo = _split2(a)
        return _dot(a_hi, b_hi) + _dot(a_hi, b_lo) + _dot(a_lo, b_hi)

    def tile(t, carry):
        rows = pl.ds(pl.multiple_of(t * R, R), R)
        q = q_ref[rows, :]
        k = k_ref[rows, :]
        v = v_ref[rows, :]
        q = q * lax.rsqrt(_dot((q * q).astype(BF16), ones_bd) + EPS) * (GDN_DK ** -0.5)
        k = k * lax.rsqrt(_dot((k * k).astype(BF16), ones_bd) + EPS)
        bg = _dot_exact_lhs(lat_ref[rows, :], sel)
        beta = bg[:, :LANES]
        gc = bg[:, LANES:]
        eg = jnp.exp(gc)
        kb = k * beta
        vb16 = (v * beta).astype(BF16)
        kbg16 = (kb * eg).astype(BF16)
        qd_ref[rows, :] = (q * eg).astype(BF16)
        k16 = k.astype(BF16)
        q16 = q.astype(BF16)

        ch = lambda arr, g: arr[g * C:(g + 1) * C, :]
        rng = range(G)
        kkqk = [_dot_nt(jnp.concatenate([ch(k16, g), ch(q16, g)], axis=0), bd16(ch(k16, g))) for g in rng]
        g_row = [_dot_exact_rhs(ones_cc, jnp.where(eye2, ch(gc, g), 0.0)) for g in rng]
        dm = [jnp.where(tril2, jnp.exp(jnp.where(tril2, ch(gc, g) - g_row[g], 0.0)), 0.0) for g in rng]
        for g in rng:
            a_ref[pl.ds(pl.multiple_of(t * R + g * C, C), C), :] = (kkqk[g][C:] * dm[g]).astype(BF16)
            g_last = ch(gc, g)[C - 1:C, :]
            kd_ref[pl.ds(pl.multiple_of(t * R + g * C, C), C), :] = (
                ch(k, g) * jnp.exp(g_last - ch(gc, g))).astype(BF16)
            sd_ref[pl.ds(pl.multiple_of((t * G + g) * SUBLANES, SUBLANES), SUBLANES), :] = jnp.broadcast_to(
                jnp.exp(g_last), (SUBLANES, LANES))
        x = [-jnp.where(strict2, kkqk[g][:C] * dm[g] * ch(beta, g), 0.0) for g in rng]
        p = [eye2f + x[g] for g in rng]
        y = [dot_hi(x[g], *bd_parts(x[g])) for g in rng]
        for lvl in range(1, 6):
            if lvl < 5:
                rhs = []
                for g in rng:
                    p_hi, p_lo = bd_parts(p[g])
                    y_hi, y_lo = bd_parts(y[g])
                    rhs.append((jnp.concatenate([p_hi, y_hi], axis=1), jnp.concatenate([p_lo, y_lo], axis=1)))
                res = [dot_hi(y[g], *rhs[g]) for g in rng]
                p = [p[g] + res[g][:, :LANES] for g in rng]
                y = [res[g][:, LANES:] for g in rng]
            else:
                p = [p[g] + dot_hi(y[g], *bd_parts(p[g])) for g in rng]
        uw = [_dot(p[g].astype(BF16), jnp.concatenate([bd16(ch(vb16, g)), bd16(ch(kbg16, g))], axis=1))
              for g in rng]
        for g in rng:
            dst = pl.ds(pl.multiple_of(t * R + g * C, C), C)
            u_ref[dst, :] = uw[g][:, :LANES]
            w_ref[dst, :] = uw[g][:, LANES:].astype(BF16)
        return carry

    lax.fori_loop(0, S // R, tile, 0)


def _gdn_prep(qkv3, lat3):
    B, S, _ = qkv3.shape
    HP = HEAD_PAIRS
    seq_blk = lambda off: pl.BlockSpec((None, S, LANES), lambda b, h, off=off: (b, 0, off + h))
    n_sd = S // GDN_CHUNK * SUBLANES
    return pl.pallas_call(
        _gdn_prep_kernel,
        grid=(B, HP),
        in_specs=[
            seq_blk(0), seq_blk(HP), seq_blk(2 * HP),
            pl.BlockSpec((None, S, LANES), lambda b, h: (b, 0, LAT_W // LANES - 1)),
        ],
        out_specs=[seq_blk(0)] * 5 + [pl.BlockSpec((None, n_sd, LANES), lambda b, h: (b, 0, h))],
        out_shape=[
            jax.ShapeDtypeStruct((B, S, GDN_V_W), F32),
            jax.ShapeDtypeStruct((B, S, GDN_V_W), BF16),
            jax.ShapeDtypeStruct((B, S, GDN_V_W), BF16),
            jax.ShapeDtypeStruct((B, S, GDN_V_W), BF16),
            jax.ShapeDtypeStruct((B, S, GDN_V_W), BF16),
            jax.ShapeDtypeStruct((B, n_sd, GDN_V_W), F32),
        ],
        compiler_params=pltpu.CompilerParams(
            dimension_semantics=("arbitrary", "arbitrary"), vmem_limit_bytes=VMEM_LIMIT),
        name="gdn_prep",
    )(qkv3, qkv3, qkv3, lat3)


def _gdn_scan_kernel(u_ref, w_ref, qd_ref, kd_ref, a_ref, sd_ref, z_ref, nw_ref, o_ref, state_ref):
    C = GDN_CHUNK
    rt = u_ref.shape[0]

    @pl.when(pl.program_id(1) == 0)
    def _():
        state_ref[...] = jnp.zeros_like(state_ref)

    lane = lax.broadcasted_iota(jnp.int32, (1, LANES), 1)
    head1 = lane >= C
    m0 = jnp.where(head1, 0.0, 1.0).astype(BF16)
    m1 = jnp.where(head1, 1.0, 0.0).astype(BF16)
    r128 = lax.broadcasted_iota(jnp.int32, (LANES, LANES), 0)
    c128 = lax.broadcasted_iota(jnp.int32, (LANES, LANES), 1)
    bdmask = (r128 // C) == (c128 // C)
    ones_bd = jnp.where(bdmask, 1.0, 0.0).astype(BF16)

    def bd16(m):
        return jnp.concatenate([m * m0, m * m1], axis=0)

    def chunk(n, carry):
        rows = pl.ds(pl.multiple_of(n * C, C), C)
        sd_rows = pl.ds(pl.multiple_of(n * SUBLANES, SUBLANES), SUBLANES)
        pairs = range(HEAD_PAIRS)
        blk = lambda hp: slice(hp * LANES, (hp + 1) * LANES)
        state = [state_ref[hp] for hp in pairs]
        res = [_dot(jnp.concatenate([w_ref[rows, blk(hp)], qd_ref[rows, blk(hp)]], axis=0),
                    state[hp].astype(BF16)) for hp in pairs]
        v_new = [(u_ref[rows, blk(hp)] - res[hp][:C]).astype(BF16) for hp in pairs]
        upd = [_dot_tn(kd_ref[rows, blk(hp)], v_new[hp]) for hp in pairs]
        o = [res[hp][C:] + _dot(a_ref[rows, blk(hp)], bd16(v_new[hp])) for hp in pairs]
        for hp in pairs:
            sd = sd_ref[sd_rows, blk(hp)][0:1, :]
            state_ref[hp] = state[hp] * sd + jnp.where(bdmask, upd[hp], 0.0)
        for hp in pairs:
            ms = _dot((o[hp] * o[hp]).astype(BF16), ones_bd) * (1.0 / GDN_DV)
            y = o[hp] * lax.rsqrt(ms + EPS) * nw_ref[:, blk(hp)] * z_ref[rows, blk(hp)].astype(F32)
            o_ref[rows, blk(hp)] = y.astype(o_ref.dtype)
        return carry

    lax.fori_loop(0, rt // C, chunk, 0)


def _gdn_scan(u, w, qd, kd, a, sd, zg, norm_w8):
    B, S, W = u.shape
    rt = min(SCAN_ROWS, S)
    n_sd = rt // GDN_CHUNK * SUBLANES
    seq = pl.BlockSpec((None, rt, W), lambda b, t: (b, t, 0))
    return pl.pallas_call(
        _gdn_scan_kernel,
        grid=(B, S // rt),
        in_specs=[seq, seq, seq, seq, seq,
                  pl.BlockSpec((None, n_sd, W), lambda b, t: (b, t, 0)),
                  seq,
                  pl.BlockSpec((1, W), lambda b, t: (0, 0))],
        out_specs=seq,
        out_shape=jax.ShapeDtypeStruct((B, S, W), BF16),
        scratch_shapes=[pltpu.VMEM((HEAD_PAIRS, LANES, LANES), F32)],
        compiler_params=pltpu.CompilerParams(
            dimension_semantics=("arbitrary", "arbitrary"), vmem_limit_bytes=VMEM_LIMIT),
        name="gdn_scan",
    )(u, w, qd, kd, a, sd, zg, norm_w8)


def _mla_prep_kernel(pos_ref, freq_ref, lat_ref, qnw_ref, kvnw_ref, wq_ref, wk_ref, wv_ref,
                     q_ref, k_ref, v_ref):
    tm = lat_ref.shape[0]
    half = MLA_ROPE // 2
    scale = (MLA_NOPE + MLA_ROPE) ** -0.5 * LOG2E
    ang = freq_ref[...] * pos_ref[...].astype(F32)
    cos_t = jnp.cos(ang)
    sin_t = jnp.sin(ang)
    zeros = lambda n: jnp.zeros((n, tm), F32)
    cos_tab = jnp.concatenate([zeros(MLA_NOPE), cos_t, cos_t, zeros(LANES - MLA_NOPE - MLA_ROPE)], axis=0).T
    sin_lo = jnp.concatenate([zeros(MLA_NOPE), -sin_t, zeros(LANES - MLA_NOPE - half)], axis=0).T
    sin_hi = jnp.concatenate([zeros(MLA_NOPE + half), sin_t, zeros(LANES - MLA_NOPE - MLA_ROPE)], axis=0).T
    lane = lax.broadcasted_iota(jnp.int32, (1, LANES), 1)
    nope = jnp.where(lane < MLA_NOPE, 1.0, 0.0)

    def rot(xh, c):
        return xh * c + pltpu.roll(xh, LANES - half, axis=1) * sin_lo + pltpu.roll(xh, half, axis=1) * sin_hi

    lat = lat_ref[...]
    cq = _rms(lat[:, 0:MLA_Q_RANK], qnw_ref[...]).astype(BF16)
    ckv = _rms(lat[:, MLA_Q_RANK:MLA_Q_RANK + MLA_KV_RANK], kvnw_ref[...]).astype(BF16)
    q = _dot(cq, wq_ref[...])
    k_nope = _dot(ckv, wk_ref[...])
    row_all = lax.broadcasted_iota(jnp.int32, (MLA_HEADS * LANES, 1), 0)
    ones_row = jnp.where(row_all % LANES == MLA_V, 1.0, 0.0)
    v_ref[...] = (_dot_nt(wv_ref[...], ckv) + ones_row).astype(v_ref.dtype)

    kpe = pltpu.roll(lat[:, LAT_W - LANES:], MLA_NOPE - KPE_OFF, axis=1)
    k_rope = rot(kpe, cos_tab)
    c_q = cos_tab + nope
    for h in range(MLA_HEADS):
        blk = slice(h * LANES, (h + 1) * LANES)
        q_ref[:, blk] = (rot(q[:, blk], c_q) * scale).astype(q_ref.dtype)
        k_ref[:, blk] = (k_nope[:, blk] + k_rope).astype(k_ref.dtype)


def _mla_prep(pos_row, inv_freq, lat, q_norm_w, kv_norm_w, wq_p, wk_p, wv_p):
    T = lat.shape[0]
    tm = min(TOK_TILE, T)
    full = lambda a: pl.BlockSpec(a.shape, lambda i: (0,) * a.ndim)
    return pl.pallas_call(
        _mla_prep_kernel,
        grid=(T // tm,),
        in_specs=[
            pl.BlockSpec((1, tm), lambda i: (0, i)),
            full(inv_freq),
            pl.BlockSpec((tm, LAT_W), lambda i: (i, 0)),
            full(q_norm_w), full(kv_norm_w), full(wq_p), full(wk_p), full(wv_p),
        ],
        out_specs=[
            pl.BlockSpec((tm, MLA_HEADS * LANES), lambda i: (i, 0)),
            pl.BlockSpec((tm, MLA_HEADS * LANES), lambda i: (i, 0)),
            pl.BlockSpec((MLA_HEADS * LANES, tm), lambda i: (0, i)),
        ],
        out_shape=[
            jax.ShapeDtypeStruct((T, MLA_HEADS * LANES), BF16),
            jax.ShapeDtypeStruct((T, MLA_HEADS * LANES), BF16),
            jax.ShapeDtypeStruct((MLA_HEADS * LANES, T), BF16),
        ],
        compiler_params=pltpu.CompilerParams(
            dimension_semantics=("arbitrary",), vmem_limit_bytes=VMEM_LIMIT),
        name="mla_prep",
    )(pos_row, inv_freq, lat, q_norm_w, kv_norm_w, wq_p, wk_p, wv_p)


def _attn_kernel(q_ref, k_ref, vt_ref, o_ref, s_ref, *, tk):
    tq = q_ref.shape[0]
    qi = pl.program_id(2)
    n_full = (qi * tq) // tk
    diag_off = qi * tq - n_full * tk
    lane = lax.broadcasted_iota(jnp.int32, (1, LANES), 1)
    q = q_ref[...]
    key_i = lax.broadcasted_iota(jnp.int32, (tk, tq), 0)
    qry_i = lax.broadcasted_iota(jnp.int32, (tk, tq), 1)
    blks = [slice(h * LANES, (h + 1) * LANES) for h in range(2)]

    def scores(j, slot):
        kv = pl.ds(pl.multiple_of(j * tk, tk), tk)
        for h in range(2):
            s_ref[slot, h] = _dot_nt(k_ref[kv, blks[h]], q[:, blks[h]])

    def consume(j, slot, stats, masked):
        kv = pl.ds(pl.multiple_of(j * tk, tk), tk)
        out = []
        for h in range(2):
            m_prev, acc = stats[h]
            s = s_ref[slot, h]
            if masked:
                s = jnp.where(key_i <= qry_i + diag_off, s, NEG_BIG)
            m_new = jnp.maximum(m_prev, jnp.max(s, axis=0, keepdims=True))
            alpha = jnp.exp2(m_prev - m_new)
            p = jnp.exp2(s - m_new).astype(BF16)
            acc = acc * alpha + _dot(vt_ref[blks[h], kv], p)
            out.append((m_new, acc))
        return tuple(out)

    def two_blocks(i, stats):
        j = 2 * i
        scores(j + 1, 1)
        stats = consume(j, 0, stats, False)
        scores(j + 2, 0)
        return consume(j + 1, 1, stats, False)

    def odd_block(stats):
        scores(n_full, 1)
        return consume(n_full - 1, 0, stats, False)

    init = tuple((jnp.full((1, tq), NEG_BIG, F32), jnp.zeros((LANES, tq), F32)) for _ in range(2))
    scores(0, 0)
    stats = lax.fori_loop(0, n_full // 2, two_blocks, init)
    stats = lax.cond(n_full % 2 == 1, odd_block, lambda st: st, stats)
    (_, acc0), (_, acc1) = consume(n_full, n_full % 2, stats, True)
    o0 = (acc0 * (1.0 / acc0[MLA_V:MLA_V + 1, :])).T
    o1 = (acc1 * (1.0 / acc1[MLA_V:MLA_V + 1, :])).T
    o_ref[...] = jnp.where(lane < MLA_V, o0, pltpu.roll(o1, MLA_V, axis=1)).astype(o_ref.dtype)


def _attn(q3, k3, vt):
    B, S, _ = q3.shape
    tq = min(ATT_TQ, S)
    tk = min(ATT_TK, S)
    assert tk % tq == 0 and S % tk == 0
    HP = MLA_HEADS // 2
    return pl.pallas_call(
        functools.partial(_attn_kernel, tk=tk),
        grid=(B, HP, S // tq),
        in_specs=[
            pl.BlockSpec((None, tq, 2 * LANES), lambda b, h, i: (b, i, h)),
            pl.BlockSpec((None, S, 2 * LANES), lambda b, h, i: (b, 0, h)),
            pl.BlockSpec((2 * LANES, S), lambda b, h, i: (h, b)),
        ],
        out_specs=pl.BlockSpec((None, tq, LANES), lambda b, h, i: (b, i, h)),
        out_shape=jax.ShapeDtypeStruct((B, S, MLA_W), BF16),
        scratch_shapes=[pltpu.VMEM((2, 2, tk, tq), F32)],
        compiler_params=pltpu.CompilerParams(
            dimension_semantics=("arbitrary", "arbitrary", "arbitrary"), vmem_limit_bytes=VMEM_LIMIT),
        name="attn",
    )(q3, k3, vt)


def _tail_kernel(x_ref, yg_ref, om_ref, p_ref, monw_ref, wout_ref, mlpnw_ref, wup_ref, wdown_ref,
                 wpp_ref, postnw_ref, gatenw_ref, wpg_ref, finnw_ref, o_ref):
    ymla = _rms(om_ref[...].astype(F32), monw_ref[...]).astype(BF16)
    h = x_ref[...] + _dot(yg_ref[...], wout_ref[0:GDN_V_W, :]) + _dot(ymla, wout_ref[GDN_V_W:, :])
    u = _rms(h, mlpnw_ref[...]).astype(BF16)
    ff_blk = 1024
    acc = jnp.zeros_like(h)
    for j in range(D_FF // ff_blk):
        cols = slice(j * ff_blk, (j + 1) * ff_blk)
        hid = jnp.maximum(_dot(u, wup_ref[:, cols]), 0.0)
        acc = acc + _dot((hid * hid).astype(BF16), wdown_ref[cols, :])
    h = h + acc
    e = _rms(_dot(p_ref[...].astype(BF16), wpp_ref[...]), postnw_ref[...])
    gate = _sigmoid(_dot(_rms(h, gatenw_ref[...]).astype(BF16), wpg_ref[...]))
    h = h + gate * e
    o_ref[...] = _rms(h, finnw_ref[...])


def _tail(x2, yg, om, p2, mla_out_norm_w, w_out, mlp_norm_w, w_up, w_down, w_ple_proj,
          ple_post_norm_w, ple_gate_norm_w, w_ple_gate, final_norm_w):
    T = x2.shape[0]
    tm = min(TOK_TILE, T)
    tok = lambda w: pl.BlockSpec((tm, w), lambda i: (i, 0))
    const = lambda a: pl.BlockSpec(a.shape, lambda i: (0, 0), pipeline_mode=pl.Buffered(1))
    return pl.pallas_call(
        _tail_kernel,
        grid=(T // tm,),
        in_specs=[
            tok(D_MODEL), tok(GDN_V_W), tok(MLA_W), tok(PLE_DIM),
            const(mla_out_norm_w), const(w_out), const(mlp_norm_w), const(w_up), const(w_down),
            const(w_ple_proj), const(ple_post_norm_w), const(ple_gate_norm_w), const(w_ple_gate),
            const(final_norm_w),
        ],
        out_specs=tok(D_MODEL),
        out_shape=jax.ShapeDtypeStruct((T, D_MODEL), F32),
        compiler_params=pltpu.CompilerParams(
            dimension_semantics=("arbitrary",), vmem_limit_bytes=VMEM_LIMIT),
        name="tail",
    )(x2, yg, om, p2, mla_out_norm_w, w_out, mlp_norm_w, w_up, w_down, w_ple_proj,
      ple_post_norm_w, ple_gate_norm_w, w_ple_gate, final_norm_w)


def _pack_w_in(w):
    o_b = QKV_W + GDN_V_W
    o_cq = o_b + 2 * GDN_HEADS
    n_lat = MLA_Q_RANK + MLA_KV_RANK + MLA_ROPE
    pad = jnp.zeros((D_MODEL, LAT_W - n_lat - 2 * GDN_HEADS), w.dtype)
    return jnp.concatenate([w[:, :o_b], w[:, o_cq:o_cq + n_lat], w[:, o_b:o_cq], pad], axis=1).astype(BF16)


def _gate_lane_vector(per_head):
    return jnp.zeros((1, LANES), F32).at[0, A_OFF:A_OFF + GDN_HEADS].set(per_head.astype(F32))


def _pack_mla_weights(w_q_b, w_kv_b):
    wq = w_q_b.reshape(MLA_Q_RANK, MLA_HEADS, MLA_NOPE + MLA_ROPE)
    wq = jnp.pad(wq, ((0, 0), (0, 0), (0, LANES - MLA_NOPE - MLA_ROPE)))
    wkv = w_kv_b.reshape(MLA_KV_RANK, MLA_HEADS, MLA_NOPE + MLA_V)
    wk = jnp.pad(wkv[:, :, :MLA_NOPE], ((0, 0), (0, 0), (0, LANES - MLA_NOPE)))
    wv = jnp.pad(wkv[:, :, MLA_NOPE:], ((0, 0), (0, 0), (0, LANES - MLA_V)))
    return (wq.reshape(MLA_Q_RANK, MLA_HEADS * LANES).astype(BF16),
            wk.reshape(MLA_KV_RANK, MLA_HEADS * LANES).astype(BF16),
            wv.reshape(MLA_KV_RANK, MLA_HEADS * LANES).T.astype(BF16))


def kernel(x, p, positions, mix_norm_w, w_in, conv_w, A_log, dt_bias, gdn_norm_w, q_norm_w, w_q_b,
           kv_norm_w, w_kv_b, mla_out_norm_w, w_out, mlp_norm_w, w_up, w_down, w_ple_proj,
           ple_post_norm_w, ple_gate_norm_w, w_ple_gate, final_norm_w):
    B, S, _ = x.shape
    T = B * S
    assert w_in.shape[0] == 1, "one layer"
    row = lambda a: a.reshape(1, -1).astype(F32)
    x2 = x.reshape(T, D_MODEL)

    qkv, zg, lat = _inproj(x2, row(mix_norm_w[0]), _pack_w_in(w_in[0]), conv_w[0].astype(F32),
                           _gate_lane_vector(A_log[0]), _gate_lane_vector(dt_bias[0]), S)

    factors = _gdn_prep(qkv.reshape(B, S, QKV_W), lat.reshape(B, S, LAT_W))
    y_gdn = _gdn_scan(*factors, zg.reshape(B, S, GDN_V_W), row(jnp.tile(gdn_norm_w[0], GDN_HEADS)))

    half = MLA_ROPE // 2
    inv_freq = (ROPE_THETA ** (-jnp.arange(0, MLA_ROPE, 2, dtype=F32) / MLA_ROPE)).reshape(half, 1)
    wq_p, wk_p, wv_p = _pack_mla_weights(w_q_b[0], w_kv_b[0])
    q_att, k_att, v_att = _mla_prep(positions.reshape(1, T), inv_freq, lat, row(q_norm_w[0]),
                                    row(kv_norm_w[0]), wq_p, wk_p, wv_p)
    o_mla = _attn(q_att.reshape(B, S, -1), k_att.reshape(B, S, -1), v_att)

    out = _tail(x2, y_gdn.reshape(T, GDN_V_W), o_mla.reshape(T, MLA_W), p[0].reshape(T, PLE_DIM),
                row(mla_out_norm_w[0]), w_out[0].astype(BF16), row(mlp_norm_w[0]),
                w_up[0].astype(BF16), w_down[0].astype(BF16), w_ple_proj[0].astype(BF16),
                row(ple_post_norm_w[0]), row(ple_gate_norm_w[0]), w_ple_gate[0].astype(BF16),
                row(final_norm_w))
    return out.reshape(B, S, D_MODEL)
```

```python
import functools

import jax
import jax.numpy as jnp
from jax import lax
from jax.experimental import pallas as pl
from jax.experimental.pallas import tpu as pltpu

F32 = jnp.float32
BF16 = jnp.bfloat16

D_MODEL = 1024
PLE_DIM = 256
GDN_HEADS = 8
GDN_DK = 64
GDN_DV = 64
GDN_QK_W = GDN_HEADS * GDN_DK
GDN_V_W = GDN_HEADS * GDN_DV
GDN_CONV = 4
GDN_CHUNK = 64
MLA_HEADS = 8
MLA_NOPE = 64
MLA_ROPE = 32
MLA_V = 64
MLA_W = MLA_HEADS * MLA_V
MLA_Q_RANK = 256
MLA_KV_RANK = 128
ROPE_THETA = 10000.0
D_FF = 4 * D_MODEL
EPS = 1e-6

LANES = 128
SUBLANES = 8
HEAD_PAIRS = GDN_HEADS // 2
QKV_W = 2 * GDN_QK_W + GDN_V_W
LAT_W = 512
IN_W_PAD = QKV_W + GDN_V_W + LAT_W
KPE_OFF, B_OFF, A_OFF = 0, MLA_ROPE, MLA_ROPE + GDN_HEADS
VMEM_LIMIT = 56 * 1024 * 1024

TOK_TILE = 512
CONV_COLS = 256
GDN_ROWS = 1024
INV_BASE = 8
SCAN_ROWS = 1024
ATT_TQ = 512
ATT_TK = 512
NEG_BIG = -1e30
LOG2E = 1.4426950408889634


def _dot(a, b):
    return jnp.dot(a, b, preferred_element_type=F32)


def _dot_nt(a, b):
    return lax.dot_general(a, b, (((1,), (1,)), ((), ())), preferred_element_type=F32)


def _dot_tn(a, b):
    return lax.dot_general(a, b, (((0,), (0,)), ((), ())), preferred_element_type=F32)


def _split2(x):
    hi = x.astype(BF16)
    lo = (x - hi.astype(F32)).astype(BF16)
    return hi, lo


def _split3(x):
    hi = x.astype(BF16)
    r1 = x - hi.astype(F32)
    mid = r1.astype(BF16)
    lo = (r1 - mid.astype(F32)).astype(BF16)
    return hi, mid, lo


def _dot_exact_rhs(a_bf16, x):
    hi, mid, lo = _split3(x)
    return _dot(a_bf16, hi) + _dot(a_bf16, mid) + _dot(a_bf16, lo)


def _dot_exact_lhs(x, b_bf16):
    hi, mid, lo = _split3(x)
    return _dot(hi, b_bf16) + _dot(mid, b_bf16) + _dot(lo, b_bf16)


def _rms(x, w):
    ms = jnp.mean(x * x, axis=-1, keepdims=True)
    return x * lax.rsqrt(ms + EPS) * w


def _sigmoid(x):
    return 1.0 / (1.0 + jnp.exp(-x))


def _silu(x):
    h = 0.5 * x
    return h + h * jnp.tanh(h)


def _softplus(x):
    return jnp.maximum(x, 0.0) + jnp.log(1.0 + jnp.exp(-jnp.abs(x)))


def _chunk_tril(n):
    r = lax.broadcasted_iota(jnp.int32, (n, n), 0)
    c = lax.broadcasted_iota(jnp.int32, (n, n), 1)
    return jnp.where(((r // GDN_CHUNK) == (c // GDN_CHUNK)) & (c <= r), 1.0, 0.0).astype(BF16)


def _inproj_kernel(x_ref, nw_ref, w_ref, cw_ref, alog_ref, dtb_ref, qkv_ref, z_ref, lat_ref, halo_ref,
                   *, tiles_per_seq):
    tm = x_ref.shape[0]
    first = (pl.program_id(0) % tiles_per_seq) == 0
    u = _rms(x_ref[...], nw_ref[...]).astype(BF16)

    for c in range(QKV_W // CONV_COLS):
        cols = slice(c * CONV_COLS, (c + 1) * CONV_COLS)
        raw = _dot(u, w_ref[:, cols])
        prev = jnp.where(first, 0.0, halo_ref[:, cols])
        halo_ref[:, cols] = raw[tm - SUBLANES:, :]
        ext = jnp.concatenate([prev, raw], axis=0)
        cw = cw_ref[:, cols]
        acc = raw * cw[GDN_CONV - 1:GDN_CONV, :]
        for kk in range(GDN_CONV - 1):
            shifted = pltpu.roll(ext, GDN_CONV - 1 - kk, axis=0)[SUBLANES:, :]
            acc = acc + shifted * cw[kk:kk + 1, :]
        qkv_ref[:, cols] = _silu(acc)

    zf = _dot(u, w_ref[:, QKV_W:QKV_W + GDN_V_W])
    z_ref[...] = _silu(zf).astype(z_ref.dtype)

    lat = _dot(u, w_ref[:, QKV_W + GDN_V_W:IN_W_PAD])
    lat_ref[:, :LAT_W - LANES] = lat[:, :LAT_W - LANES]
    l3 = lat[:, LAT_W - LANES:]
    lane = lax.broadcasted_iota(jnp.int32, (1, LANES), 1)
    is_b = (lane >= B_OFF) & (lane < B_OFF + GDN_HEADS)
    is_a = (lane >= A_OFF) & (lane < A_OFF + GDN_HEADS)
    g_raw = jnp.where(is_a, -(jnp.exp(alog_ref[...]) * _softplus(l3 + dtb_ref[...])), 0.0)
    g_cum = _dot_exact_rhs(_chunk_tril(tm), g_raw)
    lat_ref[:, LAT_W - LANES:] = jnp.where(is_b, _sigmoid(l3), jnp.where(is_a, g_cum, l3))


def _inproj(x2, mix_norm_w, w_in_p, conv_w, alog_vec, dtb_vec, seq_len):
    T = x2.shape[0]
    tm = min(TOK_TILE, seq_len)
    assert seq_len % tm == 0 and tm % GDN_CHUNK == 0
    const = lambda a: pl.BlockSpec(a.shape, lambda i: (0, 0))
    return pl.pallas_call(
        functools.partial(_inproj_kernel, tiles_per_seq=seq_len // tm),
        grid=(T // tm,),
        in_specs=[
            pl.BlockSpec((tm, D_MODEL), lambda i: (i, 0)),
            const(mix_norm_w), const(w_in_p), const(conv_w), const(alog_vec), const(dtb_vec),
        ],
        out_specs=[
            pl.BlockSpec((tm, QKV_W), lambda i: (i, 0)),
            pl.BlockSpec((tm, GDN_V_W), lambda i: (i, 0)),
            pl.BlockSpec((tm, LAT_W), lambda i: (i, 0)),
        ],
        out_shape=[
            jax.ShapeDtypeStruct((T, QKV_W), F32),
            jax.ShapeDtypeStruct((T, GDN_V_W), BF16),
            jax.ShapeDtypeStruct((T, LAT_W), F32),
        ],
        scratch_shapes=[pltpu.VMEM((SUBLANES, QKV_W), F32)],
        compiler_params=pltpu.CompilerParams(
            dimension_semantics=("arbitrary",), vmem_limit_bytes=VMEM_LIMIT),
        name="inproj",
    )(x2, mix_norm_w, w_in_p, conv_w, alog_vec, dtb_vec)


def _gdn_prep_kernel(q_ref, k_ref, v_ref, lat_ref, u_ref, w_ref, qd_ref, kd_ref, a_ref, sd_ref):
    C = GDN_CHUNK
    S = q_ref.shape[0]
    R = min(GDN_ROWS, S)
    G = R // C
    hp = pl.program_id(1)

    lane = lax.broadcasted_iota(jnp.int32, (1, LANES), 1)
    head1 = lane >= C
    m0 = jnp.where(head1, 0.0, 1.0).astype(BF16)
    m1 = jnp.where(head1, 1.0, 0.0).astype(BF16)
    r128 = lax.broadcasted_iota(jnp.int32, (LANES, LANES), 0)
    c128 = lax.broadcasted_iota(jnp.int32, (LANES, LANES), 1)
    ones_bd = jnp.where((r128 // C) == (c128 // C), 1.0, 0.0).astype(BF16)
    rC = lax.broadcasted_iota(jnp.int32, (C, LANES), 0)
    cC = lax.broadcasted_iota(jnp.int32, (C, LANES), 1) % C
    eye2 = rC == cC
    tril2 = rC >= cC
    strict2 = rC > cC
    eye2f = jnp.where(eye2, 1.0, 0.0)
    ones_cc = jnp.ones((C, C), BF16)
    sel_r = lax.broadcasted_iota(jnp.int32, (LANES, 2 * LANES), 0)
    sel_c = lax.broadcasted_iota(jnp.int32, (LANES, 2 * LANES), 1)
    sel_src = jnp.where(sel_c < LANES, B_OFF, A_OFF) + 2 * hp + ((sel_c % LANES) // C)
    sel = jnp.where(sel_r == sel_src, 1.0, 0.0).astype(BF16)

    def bd16(m):
        return jnp.concatenate([m * m0, m * m1], axis=0)

    def bd_parts(m):
        hi, lo = _split2(m)
        return bd16(hi), bd16(lo)

    def dot_hi(a, b_hi, b_lo):
        a_hi, a_lo = _split2(a)
        return _dot(a_hi, b_hi) + _dot(a_hi, b_lo) + _dot(a_lo, b_hi)

    def merge_dot(a, b):
        return _dot(a.astype(BF16), bd16(b.astype(BF16)))

    diag_blk = (rC // INV_BASE) == (cC // INV_BASE)
    off_blks = []
    size = INV_BASE
    while size < C:
        off_blks.append(((rC // (2 * size)) == (cC // (2 * size))) & ((rC // size) % 2 == 1) & ((cC // size) % 2 == 0))
        size *= 2

    def tile(t, carry):
        rows = pl.ds(pl.multiple_of(t * R, R), R)
        q = q_ref[rows, :]
        k = k_ref[rows, :]
        v = v_ref[rows, :]
        q = q * lax.rsqrt(_dot((q * q).astype(BF16), ones_bd) + EPS) * (GDN_DK ** -0.5)
        k = k * lax.rsqrt(_dot((k * k).astype(BF16), ones_bd) + EPS)
        bg = _dot_exact_lhs(lat_ref[rows, :], sel)
        beta = bg[:, :LANES]
        gc = bg[:, LANES:]
        eg = jnp.exp(gc)
        kb = k * beta
        vb16 = (v * beta).astype(BF16)
        kbg16 = (kb * eg).astype(BF16)
        qd_ref[rows, :] = (q * eg).astype(BF16)
        k16 = k.astype(BF16)
        q16 = q.astype(BF16)

        ch = lambda arr, g: arr[g * C:(g + 1) * C, :]
        rng = range(G)
        kkqk = [_dot_nt(jnp.concatenate([ch(k16, g), ch(q16, g)], axis=0), bd16(ch(k16, g))) for g in rng]
        g_row = [_dot_exact_rhs(ones_cc, jnp.where(eye2, ch(gc, g), 0.0)) for g in rng]
        dm = [jnp.where(tril2, jnp.exp(jnp.where(tril2, ch(gc, g) - g_row[g], 0.0)), 0.0) for g in rng]
        for g in rng:
            a_ref[pl.ds(pl.multiple_of(t * R + g * C, C), C), :] = (kkqk[g][C:] * dm[g]).astype(BF16)
            g_last = ch(gc, g)[C - 1:C, :]
            kd_ref[pl.ds(pl.multiple_of(t * R + g * C, C), C), :] = (
                ch(k, g) * jnp.exp(g_last - ch(gc, g))).astype(BF16)
            sd_ref[pl.ds(pl.multiple_of((t * G + g) * SUBLANES, SUBLANES), SUBLANES), :] = jnp.broadcast_to(
                jnp.exp(g_last), (SUBLANES, LANES))
        low = [jnp.where(strict2, kkqk[g][:C] * dm[g] * ch(beta, g), 0.0) for g in rng]
        x = [jnp.where(diag_blk, -low[g], 0.0) for g in rng]
        p = [eye2f + x[g] for g in rng]
        y = [dot_hi(x[g], *bd_parts(x[g])) for g in rng]
        rhs = []
        for g in rng:
            p_hi, p_lo = bd_parts(p[g])
            y_hi, y_lo = bd_parts(y[g])
            rhs.append((jnp.concatenate([p_hi, y_hi], axis=1), jnp.concatenate([p_lo, y_lo], axis=1)))
        res = [dot_hi(y[g], *rhs[g]) for g in rng]
        p = [p[g] + res[g][:, :LANES] for g in rng]
        y = [res[g][:, LANES:] for g in rng]
        p = [p[g] + dot_hi(y[g], *bd_parts(p[g])) for g in rng]
        for off_blk in off_blks:
            m1 = [merge_dot(jnp.where(off_blk, low[g], 0.0), p[g]) for g in rng]
            p = [p[g] - merge_dot(p[g], m1[g]) for g in rng]
        uw = [_dot(p[g].astype(BF16), jnp.concatenate([bd16(ch(vb16, g)), bd16(ch(kbg16, g))], axis=1))
              for g in rng]
        for g in rng:
            dst = pl.ds(pl.multiple_of(t * R + g * C, C), C)
            u_ref[dst, :] = uw[g][:, :LANES]
            w_ref[dst, :] = uw[g][:, LANES:].astype(BF16)
        return carry

    lax.fori_loop(0, S // R, tile, 0)


def _gdn_prep(qkv3, lat3):
    B, S, _ = qkv3.shape
    HP = HEAD_PAIRS
    seq_blk = lambda off: pl.BlockSpec((None, S, LANES), lambda b, h, off=off: (b, 0, off + h))
    n_sd = S // GDN_CHUNK * SUBLANES
    return pl.pallas_call(
        _gdn_prep_kernel,
        grid=(B, HP),
        in_specs=[
            seq_blk(0), seq_blk(HP), seq_blk(2 * HP),
            pl.BlockSpec((None, S, LANES), lambda b, h: (b, 0, LAT_W // LANES - 1)),
        ],
        out_specs=[seq_blk(0)] * 5 + [pl.BlockSpec((None, n_sd, LANES), lambda b, h: (b, 0, h))],
        out_shape=[
            jax.ShapeDtypeStruct((B, S, GDN_V_W), F32),
            jax.ShapeDtypeStruct((B, S, GDN_V_W), BF16),
            jax.ShapeDtypeStruct((B, S, GDN_V_W), BF16),
            jax.ShapeDtypeStruct((B, S, GDN_V_W), BF16),
            jax.ShapeDtypeStruct((B, S, GDN_V_W), BF16),
            jax.ShapeDtypeStruct((B, n_sd, GDN_V_W), F32),
        ],
        compiler_params=pltpu.CompilerParams(
            dimension_semantics=("arbitrary", "arbitrary"), vmem_limit_bytes=VMEM_LIMIT),
        name="gdn_prep",
    )(qkv3, qkv3, qkv3, lat3)


def _gdn_scan_kernel(u_ref, w_ref, qd_ref, kd_ref, a_ref, sd_ref, z_ref, nw_ref, o_ref, state_ref):
    C = GDN_CHUNK
    rt = u_ref.shape[0]

    @pl.when(pl.program_id(1) == 0)
    def _():
        state_ref[...] = jnp.zeros_like(state_ref)

    lane = lax.broadcasted_iota(jnp.int32, (1, LANES), 1)
    head1 = lane >= C
    m0 = jnp.where(head1, 0.0, 1.0).astype(BF16)
    m1 = jnp.where(head1, 1.0, 0.0).astype(BF16)
    r128 = lax.broadcasted_iota(jnp.int32, (LANES, LANES), 0)
    c128 = lax.broadcasted_iota(jnp.int32, (LANES, LANES), 1)
    bdmask = (r128 // C) == (c128 // C)
    ones_bd = jnp.where(bdmask, 1.0, 0.0).astype(BF16)

    def bd16(m):
        return jnp.concatenate([m * m0, m * m1], axis=0)

    def chunk(n, carry):
        rows = pl.ds(pl.multiple_of(n * C, C), C)
        sd_rows = pl.ds(pl.multiple_of(n * SUBLANES, SUBLANES), SUBLANES)
        pairs = range(HEAD_PAIRS)
        blk = lambda hp: slice(hp * LANES, (hp + 1) * LANES)
        state = [state_ref[hp] for hp in pairs]
        res = [_dot(jnp.concatenate([w_ref[rows, blk(hp)], qd_ref[rows, blk(hp)]], axis=0),
                    state[hp].astype(BF16)) for hp in pairs]
        v_new = [(u_ref[rows, blk(hp)] - res[hp][:C]).astype(BF16) for hp in pairs]
        upd = [_dot_tn(kd_ref[rows, blk(hp)], v_new[hp]) for hp in pairs]
        o = [res[hp][C:] + _dot(a_ref[rows, blk(hp)], bd16(v_new[hp])) for hp in pairs]
        for hp in pairs:
            sd = sd_ref[sd_rows, blk(hp)][0:1, :]
            state_ref[hp] = state[hp] * sd + jnp.where(bdmask, upd[hp], 0.0)
        for hp in pairs:
            ms = _dot((o[hp] * o[hp]).astype(BF16), ones_bd) * (1.0 / GDN_DV)
            y = o[hp] * lax.rsqrt(ms + EPS) * nw_ref[:, blk(hp)] * z_ref[rows, blk(hp)].astype(F32)
            o_ref[rows, blk(hp)] = y.astype(o_ref.dtype)
        return carry

    lax.fori_loop(0, rt // C, chunk, 0)


def _gdn_scan(u, w, qd, kd, a, sd, zg, norm_w8):
    B, S, W = u.shape
    rt = min(SCAN_ROWS, S)
    n_sd = rt // GDN_CHUNK * SUBLANES
    seq = pl.BlockSpec((None, rt, W), lambda b, t: (b, t, 0))
    return pl.pallas_call(
        _gdn_scan_kernel,
        grid=(B, S // rt),
        in_specs=[seq, seq, seq, seq, seq,
                  pl.BlockSpec((None, n_sd, W), lambda b, t: (b, t, 0)),
                  seq,
                  pl.BlockSpec((1, W), lambda b, t: (0, 0))],
        out_specs=seq,
        out_shape=jax.ShapeDtypeStruct((B, S, W), BF16),
        scratch_shapes=[pltpu.VMEM((HEAD_PAIRS, LANES, LANES), F32)],
        compiler_params=pltpu.CompilerParams(
            dimension_semantics=("arbitrary", "arbitrary"), vmem_limit_bytes=VMEM_LIMIT),
        name="gdn_scan",
    )(u, w, qd, kd, a, sd, zg, norm_w8)


def _mla_prep_kernel(pos_ref, freq_ref, lat_ref, qnw_ref, kvnw_ref, wq_ref, wqs_ref, wk_ref, wv_ref,
                     q_ref, k_ref, v_ref):
    half = MLA_ROPE // 2
    lo, mid, hi = MLA_NOPE, MLA_NOPE + half, MLA_NOPE + MLA_ROPE
    scale = (MLA_NOPE + MLA_ROPE) ** -0.5 * LOG2E
    ang = freq_ref[...] * pos_ref[...].astype(F32)
    f_i = lax.broadcasted_iota(jnp.int32, (3 * half, LANES), 0) % half
    l_i = lax.broadcasted_iota(jnp.int32, (3 * half, LANES), 1)
    in_lo = (l_i >= lo) & (l_i < mid) & (l_i - lo == f_i)
    in_hi = (l_i >= mid) & (l_i < hi) & (l_i - mid == f_i)
    expand_cos = jnp.where(in_lo | in_hi, 1.0, 0.0).astype(BF16)
    expand_sin = jnp.where(in_lo, -1.0, jnp.where(in_hi, 1.0, 0.0)).astype(BF16)
    cos_tab = _dot_tn(jnp.concatenate(_split3(jnp.cos(ang)), axis=0), expand_cos)
    sin_tab = _dot_tn(jnp.concatenate(_split3(jnp.sin(ang)), axis=0), expand_sin)
    lane = lax.broadcasted_iota(jnp.int32, (1, LANES), 1)
    nope = jnp.where(lane < MLA_NOPE, 1.0, 0.0)

    lat = lat_ref[...]
    cq = _rms(lat[:, 0:MLA_Q_RANK], qnw_ref[...]).astype(BF16)
    ckv = _rms(lat[:, MLA_Q_RANK:MLA_Q_RANK + MLA_KV_RANK], kvnw_ref[...]).astype(BF16)
    q = _dot(cq, wq_ref[...])
    q_sw = _dot(cq, wqs_ref[...])
    k_nope = _dot(ckv, wk_ref[...])
    row_all = lax.broadcasted_iota(jnp.int32, (MLA_HEADS * LANES, 1), 0)
    ones_row = jnp.where(row_all % LANES == MLA_V, 1.0, 0.0)
    v_ref[...] = (_dot_nt(wv_ref[...], ckv) + ones_row).astype(v_ref.dtype)

    kpe = pltpu.roll(lat[:, LAT_W - LANES:], MLA_NOPE - KPE_OFF, axis=1)
    kpe_sw = jnp.where(lane < mid, pltpu.roll(kpe, LANES - half, axis=1), pltpu.roll(kpe, half, axis=1))
    k_rope = kpe * cos_tab + kpe_sw * sin_tab
    c_q = (cos_tab + nope) * scale
    s_q = sin_tab * scale
    for h in range(MLA_HEADS):
        blk = slice(h * LANES, (h + 1) * LANES)
        q_ref[:, blk] = (q[:, blk] * c_q + q_sw[:, blk] * s_q).astype(q_ref.dtype)
        k_ref[:, blk] = (k_nope[:, blk] + k_rope).astype(k_ref.dtype)


def _mla_prep(pos_row, inv_freq, lat, q_norm_w, kv_norm_w, wq_p, wqs_p, wk_p, wv_p):
    T = lat.shape[0]
    tm = min(TOK_TILE, T)
    full = lambda a: pl.BlockSpec(a.shape, lambda i: (0,) * a.ndim)
    return pl.pallas_call(
        _mla_prep_kernel,
        grid=(T // tm,),
        in_specs=[
            pl.BlockSpec((1, tm), lambda i: (0, i)),
            full(inv_freq),
            pl.BlockSpec((tm, LAT_W), lambda i: (i, 0)),
            full(q_norm_w), full(kv_norm_w), full(wq_p), full(wqs_p), full(wk_p), full(wv_p),
        ],
        out_specs=[
            pl.BlockSpec((tm, MLA_HEADS * LANES), lambda i: (i, 0)),
            pl.BlockSpec((tm, MLA_HEADS * LANES), lambda i: (i, 0)),
            pl.BlockSpec((MLA_HEADS * LANES, tm), lambda i: (0, i)),
        ],
        out_shape=[
            jax.ShapeDtypeStruct((T, MLA_HEADS * LANES), BF16),
            jax.ShapeDtypeStruct((T, MLA_HEADS * LANES), BF16),
            jax.ShapeDtypeStruct((MLA_HEADS * LANES, T), BF16),
        ],
        compiler_params=pltpu.CompilerParams(
            dimension_semantics=("arbitrary",), vmem_limit_bytes=VMEM_LIMIT),
        name="mla_prep",
    )(pos_row, inv_freq, lat, q_norm_w, kv_norm_w, wq_p, wqs_p, wk_p, wv_p)


def _attn_kernel(q_ref, k_ref, vt_ref, o_ref, s_ref, *, tk):
    tq = q_ref.shape[0]
    qi = pl.program_id(2)
    n_full = (qi * tq) // tk
    diag_off = qi * tq - n_full * tk
    lane = lax.broadcasted_iota(jnp.int32, (1, LANES), 1)
    q = q_ref[...]
    key_i = lax.broadcasted_iota(jnp.int32, (tk, tq), 0)
    qry_i = lax.broadcasted_iota(jnp.int32, (tk, tq), 1)
    blks = [slice(h * LANES, (h + 1) * LANES) for h in range(2)]

    def scores(j, slot):
        kv = pl.ds(pl.multiple_of(j * tk, tk), tk)
        for h in range(2):
            s_ref[slot, h] = _dot_nt(k_ref[kv, blks[h]], q[:, blks[h]])

    def consume(j, slot, stats, masked):
        kv = pl.ds(pl.multiple_of(j * tk, tk), tk)
        out = []
        for h in range(2):
            m_prev, acc = stats[h]
            s = s_ref[slot, h]
            if masked:
                s = jnp.where(key_i <= qry_i + diag_off, s, NEG_BIG)
            m_new = jnp.maximum(m_prev, jnp.max(s, axis=0, keepdims=True))
            alpha = jnp.exp2(m_prev - m_new)
            p = jnp.exp2(s - m_new).astype(BF16)
            acc = acc * alpha + _dot(vt_ref[blks[h], kv], p)
            out.append((m_new, acc))
        return tuple(out)

    def two_blocks(i, stats):
        j = 2 * i
        scores(j + 1, 1)
        stats = consume(j, 0, stats, False)
        scores(j + 2, 0)
        return consume(j + 1, 1, stats, False)

    def odd_block(stats):
        scores(n_full, 1)
        return consume(n_full - 1, 0, stats, False)

    init = tuple((jnp.full((1, tq), NEG_BIG, F32), jnp.zeros((LANES, tq), F32)) for _ in range(2))
    scores(0, 0)
    stats = lax.fori_loop(0, n_full // 2, two_blocks, init)
    stats = lax.cond(n_full % 2 == 1, odd_block, lambda st: st, stats)
    (_, acc0), (_, acc1) = consume(n_full, n_full % 2, stats, True)
    o0 = (acc0 * (1.0 / acc0[MLA_V:MLA_V + 1, :])).T
    o1 = (acc1 * (1.0 / acc1[MLA_V:MLA_V + 1, :])).T
    o_ref[...] = jnp.where(lane < MLA_V, o0, pltpu.roll(o1, MLA_V, axis=1)).astype(o_ref.dtype)


def _attn(q3, k3, vt):
    B, S, _ = q3.shape
    tq = min(ATT_TQ, S)
    tk = min(ATT_TK, S)
    assert tk % tq == 0 and S % tk == 0
    HP = MLA_HEADS // 2
    return pl.pallas_call(
        functools.partial(_attn_kernel, tk=tk),
        grid=(B, HP, S // tq),
        in_specs=[
            pl.BlockSpec((None, tq, 2 * LANES), lambda b, h, i: (b, i, h)),
            pl.BlockSpec((None, S, 2 * LANES), lambda b, h, i: (b, 0, h)),
            pl.BlockSpec((2 * LANES, S), lambda b, h, i: (h, b)),
        ],
        out_specs=pl.BlockSpec((None, tq, LANES), lambda b, h, i: (b, i, h)),
        out_shape=jax.ShapeDtypeStruct((B, S, MLA_W), BF16),
        scratch_shapes=[pltpu.VMEM((2, 2, tk, tq), F32)],
        compiler_params=pltpu.CompilerParams(
            dimension_semantics=("arbitrary", "arbitrary", "arbitrary"), vmem_limit_bytes=VMEM_LIMIT),
        name="attn",
    )(q3, k3, vt)


def _tail_kernel(x_ref, yg_ref, om_ref, p_ref, monw_ref, wout_ref, mlpnw_ref, wup_ref, wdown_ref,
                 wpp_ref, postnw_ref, gatenw_ref, wpg_ref, finnw_ref, o_ref):
    ymla = _rms(om_ref[...].astype(F32), monw_ref[...]).astype(BF16)
    h = x_ref[...] + _dot(yg_ref[...], wout_ref[0:GDN_V_W, :]) + _dot(ymla, wout_ref[GDN_V_W:, :])
    u = _rms(h, mlpnw_ref[...]).astype(BF16)
    ff_blk = 1024
    acc = jnp.zeros_like(h)
    for j in range(D_FF // ff_blk):
        cols = slice(j * ff_blk, (j + 1) * ff_blk)
        hid = jnp.maximum(_dot(u, wup_ref[:, cols]), 0.0)
        acc = acc + _dot((hid * hid).astype(BF16), wdown_ref[cols, :])
    h = h + acc
    e = _rms(_dot(p_ref[...].astype(BF16), wpp_ref[...]), postnw_ref[...])
    gate = _sigmoid(_dot(_rms(h, gatenw_ref[...]).astype(BF16), wpg_ref[...]))
    h = h + gate * e
    o_ref[...] = _rms(h, finnw_ref[...])


def _tail(x2, yg, om, p2, mla_out_norm_w, w_out, mlp_norm_w, w_up, w_down, w_ple_proj,
          ple_post_norm_w, ple_gate_norm_w, w_ple_gate, final_norm_w):
    T = x2.shape[0]
    tm = min(TOK_TILE, T)
    tok = lambda w: pl.BlockSpec((tm, w), lambda i: (i, 0))
    const = lambda a: pl.BlockSpec(a.shape, lambda i: (0, 0), pipeline_mode=pl.Buffered(1))
    return pl.pallas_call(
        _tail_kernel,
        grid=(T // tm,),
        in_specs=[
            tok(D_MODEL), tok(GDN_V_W), tok(MLA_W), tok(PLE_DIM),
            const(mla_out_norm_w), const(w_out), const(mlp_norm_w), const(w_up), const(w_down),
            const(w_ple_proj), const(ple_post_norm_w), const(ple_gate_norm_w), const(w_ple_gate),
            const(final_norm_w),
        ],
        out_specs=tok(D_MODEL),
        out_shape=jax.ShapeDtypeStruct((T, D_MODEL), F32),
        compiler_params=pltpu.CompilerParams(
            dimension_semantics=("arbitrary",), vmem_limit_bytes=VMEM_LIMIT),
        name="tail",
    )(x2, yg, om, p2, mla_out_norm_w, w_out, mlp_norm_w, w_up, w_down, w_ple_proj,
      ple_post_norm_w, ple_gate_norm_w, w_ple_gate, final_norm_w)


def _pack_w_in(w):
    o_b = QKV_W + GDN_V_W
    o_cq = o_b + 2 * GDN_HEADS
    n_lat = MLA_Q_RANK + MLA_KV_RANK + MLA_ROPE
    pad = jnp.zeros((D_MODEL, LAT_W - n_lat - 2 * GDN_HEADS), w.dtype)
    return jnp.concatenate([w[:, :o_b], w[:, o_cq:o_cq + n_lat], w[:, o_b:o_cq], pad], axis=1).astype(BF16)


def _gate_lane_vector(per_head):
    return jnp.zeros((1, LANES), F32).at[0, A_OFF:A_OFF + GDN_HEADS].set(per_head.astype(F32))


def _pack_mla_weights(w_q_b, w_kv_b):
    wq = w_q_b.reshape(MLA_Q_RANK, MLA_HEADS, MLA_NOPE + MLA_ROPE)
    wq = jnp.pad(wq, ((0, 0), (0, 0), (0, LANES - MLA_NOPE - MLA_ROPE)))
    wkv = w_kv_b.reshape(MLA_KV_RANK, MLA_HEADS, MLA_NOPE + MLA_V)
    wk = jnp.pad(wkv[:, :, :MLA_NOPE], ((0, 0), (0, 0), (0, LANES - MLA_NOPE)))
    wv = jnp.pad(wkv[:, :, MLA_NOPE:], ((0, 0), (0, 0), (0, LANES - MLA_V)))
    half = MLA_ROPE // 2
    zeros = lambda n: jnp.zeros((MLA_Q_RANK, MLA_HEADS, n), wq.dtype)
    wq_sw = jnp.concatenate([zeros(MLA_NOPE), wq[:, :, MLA_NOPE + half:MLA_NOPE + MLA_ROPE],
                             wq[:, :, MLA_NOPE:MLA_NOPE + half], zeros(LANES - MLA_NOPE - MLA_ROPE)], axis=2)
    return (wq.reshape(MLA_Q_RANK, MLA_HEADS * LANES).astype(BF16),
            wq_sw.reshape(MLA_Q_RANK, MLA_HEADS * LANES).astype(BF16),
            wk.reshape(MLA_KV_RANK, MLA_HEADS * LANES).astype(BF16),
            wv.reshape(MLA_KV_RANK, MLA_HEADS * LANES).T.astype(BF16))


def kernel(x, p, positions, mix_norm_w, w_in, conv_w, A_log, dt_bias, gdn_norm_w, q_norm_w, w_q_b,
           kv_norm_w, w_kv_b, mla_out_norm_w, w_out, mlp_norm_w, w_up, w_down, w_ple_proj,
           ple_post_norm_w, ple_gate_norm_w, w_ple_gate, final_norm_w):
    B, S, _ = x.shape
    T = B * S
    assert w_in.shape[0] == 1, "one layer"
    row = lambda a: a.reshape(1, -1).astype(F32)
    x2 = x.reshape(T, D_MODEL)

    qkv, zg, lat = _inproj(x2, row(mix_norm_w[0]), _pack_w_in(w_in[0]), conv_w[0].astype(F32),
                           _gate_lane_vector(A_log[0]), _gate_lane_vector(dt_bias[0]), S)

    factors = _gdn_prep(qkv.reshape(B, S, QKV_W), lat.reshape(B, S, LAT_W))
    y_gdn = _gdn_scan(*factors, zg.reshape(B, S, GDN_V_W), row(jnp.tile(gdn_norm_w[0], GDN_HEADS)))

    half = MLA_ROPE // 2
    inv_freq = (ROPE_THETA ** (-jnp.arange(0, MLA_ROPE, 2, dtype=F32) / MLA_ROPE)).reshape(half, 1)
    q_att, k_att, v_att = _mla_prep(positions.reshape(1, T), inv_freq, lat, row(q_norm_w[0]),
                                    row(kv_norm_w[0]), *_pack_mla_weights(w_q_b[0], w_kv_b[0]))
    o_mla = _attn(q_att.reshape(B, S, -1), k_att.reshape(B, S, -1), v_att)

    out = _tail(x2, y_gdn.reshape(T, GDN_V_W), o_mla.reshape(T, MLA_W), p[0].reshape(T, PLE_DIM),
                row(mla_out_norm_w[0]), w_out[0].astype(BF16), row(mlp_norm_w[0]),
                w_up[0].astype(BF16), w_down[0].astype(BF16), w_ple_proj[0].astype(BF16),
                row(ple_post_norm_w[0]), row(ple_gate_norm_w[0]), w_ple_gate[0].astype(BF16),
                row(final_norm_w))
    return out.reshape(B, S, D_MODEL)
```

```python
import functools

import jax
import jax.numpy as jnp
from jax import lax
from jax.experimental import pallas as pl
from jax.experimental.pallas import tpu as pltpu

F32 = jnp.float32
BF16 = jnp.bfloat16

D_MODEL = 1024
PLE_DIM = 256
GDN_HEADS = 8
GDN_DK = 64
GDN_DV = 64
GDN_QK_W = GDN_HEADS * GDN_DK
GDN_V_W = GDN_HEADS * GDN_DV
GDN_CONV = 4
GDN_CHUNK = 64
MLA_HEADS = 8
MLA_NOPE = 64
MLA_ROPE = 32
MLA_V = 64
MLA_W = MLA_HEADS * MLA_V
MLA_Q_RANK = 256
MLA_KV_RANK = 128
ROPE_THETA = 10000.0
D_FF = 4 * D_MODEL
EPS = 1e-6

LANES = 128
SUBLANES = 8
HEAD_PAIRS = GDN_HEADS // 2
QKV_W = 2 * GDN_QK_W + GDN_V_W
LAT_W = 512
IN_W_PAD = QKV_W + GDN_V_W + LAT_W
KPE_OFF, B_OFF, A_OFF = 0, MLA_ROPE, MLA_ROPE + GDN_HEADS
VMEM_LIMIT = 56 * 1024 * 1024

TOK_TILE = 512
CONV_COLS = 256
GDN_ROWS = 1024
INV_BASE = 8
SCAN_ROWS = 512
SCAN_BATCH = 4
ATT_TQ = 512
ATT_TK = 512
ATT_HEADS = 4
NEG_BIG = -1e30
LOG2E = 1.4426950408889634


def _dot(a, b):
    return jnp.dot(a, b, preferred_element_type=F32)


def _dot_nt(a, b):
    return lax.dot_general(a, b, (((1,), (1,)), ((), ())), preferred_element_type=F32)


def _dot_tn(a, b):
    return lax.dot_general(a, b, (((0,), (0,)), ((), ())), preferred_element_type=F32)


def _split2(x):
    hi = x.astype(BF16)
    lo = (x - hi.astype(F32)).astype(BF16)
    return hi, lo


def _split3(x):
    hi = x.astype(BF16)
    r1 = x - hi.astype(F32)
    mid = r1.astype(BF16)
    lo = (r1 - mid.astype(F32)).astype(BF16)
    return hi, mid, lo


def _dot_exact_rhs(a_bf16, x):
    hi, mid, lo = _split3(x)
    return _dot(a_bf16, hi) + _dot(a_bf16, mid) + _dot(a_bf16, lo)


def _dot_exact_lhs(x, b_bf16):
    hi, mid, lo = _split3(x)
    return _dot(hi, b_bf16) + _dot(mid, b_bf16) + _dot(lo, b_bf16)


def _rms(x, w):
    ms = jnp.mean(x * x, axis=-1, keepdims=True)
    return x * lax.rsqrt(ms + EPS) * w


def _sigmoid(x):
    return 1.0 / (1.0 + jnp.exp(-x))


def _silu(x):
    h = 0.5 * x
    return h + h * jnp.tanh(h)


def _softplus(x):
    return jnp.maximum(x, 0.0) + jnp.log(1.0 + jnp.exp(-jnp.abs(x)))


def _chunk_tril(n):
    r = lax.broadcasted_iota(jnp.int32, (n, n), 0)
    c = lax.broadcasted_iota(jnp.int32, (n, n), 1)
    return jnp.where(((r // GDN_CHUNK) == (c // GDN_CHUNK)) & (c <= r), 1.0, 0.0).astype(BF16)


def _inproj_kernel(x_ref, nw_ref, w_ref, cw_ref, alog_ref, dtb_ref, qkv_ref, z_ref, lat_ref, halo_ref,
                   *, tiles_per_seq):
    tm = x_ref.shape[0]
    first = (pl.program_id(0) % tiles_per_seq) == 0
    u = _rms(x_ref[...], nw_ref[...]).astype(BF16)

    for c in range(QKV_W // CONV_COLS):
        cols = slice(c * CONV_COLS, (c + 1) * CONV_COLS)
        raw = _dot(u, w_ref[:, cols])
        prev = jnp.where(first, 0.0, halo_ref[:, cols])
        halo_ref[:, cols] = raw[tm - SUBLANES:, :]
        ext = jnp.concatenate([prev, raw], axis=0)
        cw = cw_ref[:, cols]
        acc = raw * cw[GDN_CONV - 1:GDN_CONV, :]
        for kk in range(GDN_CONV - 1):
            shifted = pltpu.roll(ext, GDN_CONV - 1 - kk, axis=0)[SUBLANES:, :]
            acc = acc + shifted * cw[kk:kk + 1, :]
        qkv_ref[:, cols] = _silu(acc)

    zf = _dot(u, w_ref[:, QKV_W:QKV_W + GDN_V_W])
    z_ref[...] = _silu(zf).astype(z_ref.dtype)

    lat = _dot(u, w_ref[:, QKV_W + GDN_V_W:IN_W_PAD])
    lat_ref[:, :LAT_W - LANES] = lat[:, :LAT_W - LANES]
    l3 = lat[:, LAT_W - LANES:]
    lane = lax.broadcasted_iota(jnp.int32, (1, LANES), 1)
    is_b = (lane >= B_OFF) & (lane < B_OFF + GDN_HEADS)
    is_a = (lane >= A_OFF) & (lane < A_OFF + GDN_HEADS)
    g_raw = jnp.where(is_a, -(jnp.exp(alog_ref[...]) * _softplus(l3 + dtb_ref[...])), 0.0)
    g_cum = _dot_exact_rhs(_chunk_tril(tm), g_raw)
    lat_ref[:, LAT_W - LANES:] = jnp.where(is_b, _sigmoid(l3), jnp.where(is_a, g_cum, l3))


def _inproj(x2, mix_norm_w, w_in_p, conv_w, alog_vec, dtb_vec, seq_len):
    T = x2.shape[0]
    tm = min(TOK_TILE, seq_len)
    assert seq_len % tm == 0 and tm % GDN_CHUNK == 0
    nt = T // tm
    const = lambda a: pl.BlockSpec(a.shape, lambda i: (0, 0))
    return pl.pallas_call(
        functools.partial(_inproj_kernel, tiles_per_seq=seq_len // tm),
        grid=(nt,),
        in_specs=[
            pl.BlockSpec((tm, D_MODEL), lambda i: (i, 0)),
            const(mix_norm_w), const(w_in_p), const(conv_w), const(alog_vec), const(dtb_vec),
        ],
        out_specs=[
            pl.BlockSpec((tm, QKV_W), lambda i: (i, 0)),
            pl.BlockSpec((tm, GDN_V_W), lambda i: (i, 0)),
            pl.BlockSpec((tm, LAT_W), lambda i: (i, 0)),
        ],
        out_shape=[
            jax.ShapeDtypeStruct((T, QKV_W), F32),
            jax.ShapeDtypeStruct((T, GDN_V_W), BF16),
            jax.ShapeDtypeStruct((T, LAT_W), F32),
        ],
        scratch_shapes=[pltpu.VMEM((SUBLANES, QKV_W), F32)],
        compiler_params=pltpu.CompilerParams(
            dimension_semantics=("arbitrary",), vmem_limit_bytes=VMEM_LIMIT),
        name="inproj",
    )(x2, mix_norm_w, w_in_p, conv_w, alog_vec, dtb_vec)


def _gdn_prep_kernel(q_ref, k_ref, v_ref, lat_ref, u_ref, w_ref, qd_ref, kd_ref, a_ref, sd_ref):
    C = GDN_CHUNK
    S = q_ref.shape[0]
    R = min(GDN_ROWS, S)
    G = R // C
    hp = pl.program_id(1)

    lane = lax.broadcasted_iota(jnp.int32, (1, LANES), 1)
    head1 = lane >= C
    m0 = jnp.where(head1, 0.0, 1.0).astype(BF16)
    m1 = jnp.where(head1, 1.0, 0.0).astype(BF16)
    r128 = lax.broadcasted_iota(jnp.int32, (LANES, LANES), 0)
    c128 = lax.broadcasted_iota(jnp.int32, (LANES, LANES), 1)
    ones_bd = jnp.where((r128 // C) == (c128 // C), 1.0, 0.0).astype(BF16)
    rC = lax.broadcasted_iota(jnp.int32, (C, LANES), 0)
    cC = lax.broadcasted_iota(jnp.int32, (C, LANES), 1) % C
    eye2 = rC == cC
    tril2 = rC >= cC
    strict2 = rC > cC
    eye2f = jnp.where(eye2, 1.0, 0.0)
    ones_cc = jnp.ones((C, C), BF16)
    sel_r = lax.broadcasted_iota(jnp.int32, (LANES, 2 * LANES), 0)
    sel_c = lax.broadcasted_iota(jnp.int32, (LANES, 2 * LANES), 1)
    sel_src = jnp.where(sel_c < LANES, B_OFF, A_OFF) + 2 * hp + ((sel_c % LANES) // C)
    sel = jnp.where(sel_r == sel_src, 1.0, 0.0).astype(BF16)

    def bd16(m):
        return jnp.concatenate([m * m0, m * m1], axis=0)

    def bd_parts(m):
        hi, lo = _split2(m)
        return bd16(hi), bd16(lo)

    def dot_hi(a, b_hi, b_lo):
        a_hi, a_lo = _split2(a)
        return _dot(a_hi, b_hi) + _dot(a_hi, b_lo) + _dot(a_lo, b_hi)

    def merge_dot(a, b):
        return _dot(a.astype(BF16), bd16(b.astype(BF16)))

    diag_blk = (rC // INV_BASE) == (cC // INV_BASE)
    off_blks = []
    size = INV_BASE
    while size < C:
        off_blks.append(((rC // (2 * size)) == (cC // (2 * size))) & ((rC // size) % 2 == 1) & ((cC // size) % 2 == 0))
        size *= 2

    def tile(t, carry):
        rows = pl.ds(pl.multiple_of(t * R, R), R)
        q = q_ref[rows, :]
        k = k_ref[rows, :]
        v = v_ref[rows, :]
        q = q * lax.rsqrt(_dot((q * q).astype(BF16), ones_bd) + EPS) * (GDN_DK ** -0.5)
        k = k * lax.rsqrt(_dot((k * k).astype(BF16), ones_bd) + EPS)
        bg = _dot_exact_lhs(lat_ref[rows, :], sel)
        beta = bg[:, :LANES]
        gc = bg[:, LANES:]
        eg = jnp.exp(gc)
        kb = k * beta
        vb16 = (v * beta).astype(BF16)
        kbg16 = (kb * eg).astype(BF16)
        qd_ref[rows, :] = (q * eg).astype(BF16)
        k16 = k.astype(BF16)
        q16 = q.astype(BF16)

        ch = lambda arr, g: arr[g * C:(g + 1) * C, :]
        rng = range(G)
        kkqk = [_dot_nt(jnp.concatenate([ch(k16, g), ch(q16, g)], axis=0), bd16(ch(k16, g))) for g in rng]
        g_row = [_dot_exact_rhs(ones_cc, jnp.where(eye2, ch(gc, g), 0.0)) for g in rng]
        dm = [jnp.where(tril2, jnp.exp(jnp.where(tril2, ch(gc, g) - g_row[g], 0.0)), 0.0) for g in rng]
        for g in rng:
            a_ref[pl.ds(pl.multiple_of(t * R + g * C, C), C), :] = (kkqk[g][C:] * dm[g]).astype(BF16)
            g_last = ch(gc, g)[C - 1:C, :]
            kd_ref[pl.ds(pl.multiple_of(t * R + g * C, C), C), :] = (
                ch(k, g) * jnp.exp(g_last - ch(gc, g))).astype(BF16)
            sd_ref[pl.ds(pl.multiple_of((t * G + g) * SUBLANES, SUBLANES), SUBLANES), :] = jnp.broadcast_to(
                jnp.exp(g_last), (SUBLANES, LANES))
        low = [jnp.where(strict2, kkqk[g][:C] * dm[g] * ch(beta, g), 0.0) for g in rng]
        x = [jnp.where(diag_blk, -low[g], 0.0) for g in rng]
        p = [eye2f + x[g] for g in rng]
        y = [dot_hi(x[g], *bd_parts(x[g])) for g in rng]
        rhs = []
        for g in rng:
            p_hi, p_lo = bd_parts(p[g])
            y_hi, y_lo = bd_parts(y[g])
            rhs.append((jnp.concatenate([p_hi, y_hi], axis=1), jnp.concatenate([p_lo, y_lo], axis=1)))
        res = [dot_hi(y[g], *rhs[g]) for g in rng]
        p = [p[g] + res[g][:, :LANES] for g in rng]
        y = [res[g][:, LANES:] for g in rng]
        p = [p[g] + dot_hi(y[g], *bd_parts(p[g])) for g in rng]
        for off_blk in off_blks:
            m1 = [merge_dot(jnp.where(off_blk, low[g], 0.0), p[g]) for g in rng]
            p = [p[g] - merge_dot(p[g], m1[g]) for g in rng]
        uw = [_dot(p[g].astype(BF16), jnp.concatenate([bd16(ch(vb16, g)), bd16(ch(kbg16, g))], axis=1))
              for g in rng]
        for g in rng:
            dst = pl.ds(pl.multiple_of(t * R + g * C, C), C)
            u_ref[dst, :] = uw[g][:, :LANES]
            w_ref[dst, :] = uw[g][:, LANES:].astype(BF16)
        return carry

    lax.fori_loop(0, S // R, tile, 0)


def _gdn_prep(qkv3, lat3):
    B, S, _ = qkv3.shape
    HP = HEAD_PAIRS
    seq_blk = lambda off: pl.BlockSpec((None, S, LANES), lambda b, h, off=off: (b, 0, off + h))
    n_sd = S // GDN_CHUNK * SUBLANES
    return pl.pallas_call(
        _gdn_prep_kernel,
        grid=(B, HP),
        in_specs=[
            seq_blk(0), seq_blk(HP), seq_blk(2 * HP),
            pl.BlockSpec((None, S, LANES), lambda b, h: (b, 0, LAT_W // LANES - 1)),
        ],
        out_specs=[seq_blk(0)] * 5 + [pl.BlockSpec((None, n_sd, LANES), lambda b, h: (b, 0, h))],
        out_shape=[
            jax.ShapeDtypeStruct((B, S, GDN_V_W), F32),
            jax.ShapeDtypeStruct((B, S, GDN_V_W), BF16),
            jax.ShapeDtypeStruct((B, S, GDN_V_W), BF16),
            jax.ShapeDtypeStruct((B, S, GDN_V_W), BF16),
            jax.ShapeDtypeStruct((B, S, GDN_V_W), BF16),
            jax.ShapeDtypeStruct((B, n_sd, GDN_V_W), F32),
        ],
        compiler_params=pltpu.CompilerParams(
            dimension_semantics=("arbitrary", "arbitrary"), vmem_limit_bytes=VMEM_LIMIT),
        name="gdn_prep",
    )(qkv3, qkv3, qkv3, lat3)


def _gdn_scan_kernel(u_ref, w_ref, qd_ref, kd_ref, a_ref, sd_ref, z_ref, nw_ref, o_ref, state_ref):
    C = GDN_CHUNK
    nb, rt, _ = u_ref.shape

    @pl.when(pl.program_id(1) == 0)
    def _():
        state_ref[...] = jnp.zeros_like(state_ref)

    lane = lax.broadcasted_iota(jnp.int32, (1, LANES), 1)
    head1 = lane >= C
    m0 = jnp.where(head1, 0.0, 1.0).astype(BF16)
    m1 = jnp.where(head1, 1.0, 0.0).astype(BF16)
    r128 = lax.broadcasted_iota(jnp.int32, (LANES, LANES), 0)
    c128 = lax.broadcasted_iota(jnp.int32, (LANES, LANES), 1)
    bdmask = (r128 // C) == (c128 // C)
    ones_bd = jnp.where(bdmask, 1.0, 0.0).astype(BF16)

    chains = [(bb, hp) for bb in range(nb) for hp in range(HEAD_PAIRS)]
    blk = lambda hp: slice(hp * LANES, (hp + 1) * LANES)

    def bd16(m):
        return jnp.concatenate([m * m0, m * m1], axis=0)

    def chunk(n, carry):
        rows = pl.ds(pl.multiple_of(n * C, C), C)
        sd_rows = pl.ds(pl.multiple_of(n * SUBLANES, SUBLANES), SUBLANES)
        state = [state_ref[i] for i in range(len(chains))]
        res = [_dot(jnp.concatenate([w_ref[bb, rows, blk(hp)], qd_ref[bb, rows, blk(hp)]], axis=0),
                    state[i].astype(BF16)) for i, (bb, hp) in enumerate(chains)]
        v_new = [(u_ref[bb, rows, blk(hp)] - res[i][:C]).astype(BF16) for i, (bb, hp) in enumerate(chains)]
        upd = [_dot_tn(kd_ref[bb, rows, blk(hp)], v_new[i]) for i, (bb, hp) in enumerate(chains)]
        for i, (bb, hp) in enumerate(chains):
            sd = sd_ref[bb, sd_rows, blk(hp)][0:1, :]
            state_ref[i] = state[i] * sd + jnp.where(bdmask, upd[i], 0.0)
        o = [res[i][C:] + _dot(a_ref[bb, rows, blk(hp)], bd16(v_new[i])) for i, (bb, hp) in enumerate(chains)]
        for i, (bb, hp) in enumerate(chains):
            ms = _dot((o[i] * o[i]).astype(BF16), ones_bd) * (1.0 / GDN_DV)
            y = o[i] * lax.rsqrt(ms + EPS) * nw_ref[:, blk(hp)] * z_ref[bb, rows, blk(hp)].astype(F32)
            o_ref[bb, rows, blk(hp)] = y.astype(o_ref.dtype)
        return carry

    lax.fori_loop(0, rt // C, chunk, 0)


def _gdn_scan(u, w, qd, kd, a, sd, zg, norm_w8):
    B, S, W = u.shape
    rt = min(SCAN_ROWS, S)
    nb = min(SCAN_BATCH, B)
    assert B % nb == 0 and S % rt == 0
    n_sd = rt // GDN_CHUNK * SUBLANES
    seq = pl.BlockSpec((nb, rt, W), lambda b, t: (b, t, 0))
    return pl.pallas_call(
        _gdn_scan_kernel,
        grid=(B // nb, S // rt),
        in_specs=[seq, seq, seq, seq, seq,
                  pl.BlockSpec((nb, n_sd, W), lambda b, t: (b, t, 0)),
                  seq,
                  pl.BlockSpec((1, W), lambda b, t: (0, 0))],
        out_specs=seq,
        out_shape=jax.ShapeDtypeStruct((B, S, W), BF16),
        scratch_shapes=[pltpu.VMEM((nb * HEAD_PAIRS, LANES, LANES), F32)],
        compiler_params=pltpu.CompilerParams(
            dimension_semantics=("arbitrary", "arbitrary"), vmem_limit_bytes=VMEM_LIMIT),
        name="gdn_scan",
    )(u, w, qd, kd, a, sd, zg, norm_w8)


def _mla_prep_kernel(pos_ref, freq_ref, lat_ref, qnw_ref, kvnw_ref, wq_ref, wqs_ref, wk_ref, wv_ref,
                     q_ref, k_ref, v_ref):
    half = MLA_ROPE // 2
    lo, mid, hi = MLA_NOPE, MLA_NOPE + half, MLA_NOPE + MLA_ROPE
    scale = (MLA_NOPE + MLA_ROPE) ** -0.5 * LOG2E
    ang = freq_ref[...] * pos_ref[...].astype(F32)
    f_i = lax.broadcasted_iota(jnp.int32, (3 * half, LANES), 0) % half
    l_i = lax.broadcasted_iota(jnp.int32, (3 * half, LANES), 1)
    in_lo = (l_i >= lo) & (l_i < mid) & (l_i - lo == f_i)
    in_hi = (l_i >= mid) & (l_i < hi) & (l_i - mid == f_i)
    expand_cos = jnp.where(in_lo | in_hi, 1.0, 0.0).astype(BF16)
    expand_sin = jnp.where(in_lo, -1.0, jnp.where(in_hi, 1.0, 0.0)).astype(BF16)
    cos_tab = _dot_tn(jnp.concatenate(_split3(jnp.cos(ang)), axis=0), expand_cos)
    sin_tab = _dot_tn(jnp.concatenate(_split3(jnp.sin(ang)), axis=0), expand_sin)
    lane = lax.broadcasted_iota(jnp.int32, (1, LANES), 1)
    nope = jnp.where(lane < MLA_NOPE, 1.0, 0.0)

    lat = lat_ref[...]
    cq = _rms(lat[:, 0:MLA_Q_RANK], qnw_ref[...]).astype(BF16)
    ckv = _rms(lat[:, MLA_Q_RANK:MLA_Q_RANK + MLA_KV_RANK], kvnw_ref[...]).astype(BF16)
    q = _dot(cq, wq_ref[...])
    q_sw = _dot(cq, wqs_ref[...])
    k_nope = _dot(ckv, wk_ref[...])
    row_all = lax.broadcasted_iota(jnp.int32, (MLA_HEADS * LANES, 1), 0)
    ones_row = jnp.where(row_all % LANES == MLA_V, 1.0, 0.0)
    v_ref[...] = (_dot_nt(wv_ref[...], ckv) + ones_row).astype(v_ref.dtype)

    kpe = pltpu.roll(lat[:, LAT_W - LANES:], MLA_NOPE - KPE_OFF, axis=1)
    kpe_sw = jnp.where(lane < mid, pltpu.roll(kpe, LANES - half, axis=1), pltpu.roll(kpe, half, axis=1))
    k_rope = kpe * cos_tab + kpe_sw * sin_tab
    c_q = (cos_tab + nope) * scale
    s_q = sin_tab * scale
    for h in range(MLA_HEADS):
        blk = slice(h * LANES, (h + 1) * LANES)
        q_ref[:, blk] = (q[:, blk] * c_q + q_sw[:, blk] * s_q).astype(q_ref.dtype)
        k_ref[:, blk] = (k_nope[:, blk] + k_rope).astype(k_ref.dtype)


def _mla_prep(pos_row, inv_freq, lat, q_norm_w, kv_norm_w, wq_p, wqs_p, wk_p, wv_p):
    T = lat.shape[0]
    tm = min(TOK_TILE, T)
    full = lambda a: pl.BlockSpec(a.shape, lambda i: (0,) * a.ndim)
    return pl.pallas_call(
        _mla_prep_kernel,
        grid=(T // tm,),
        in_specs=[
            pl.BlockSpec((1, tm), lambda i: (0, i)),
            full(inv_freq),
            pl.BlockSpec((tm, LAT_W), lambda i: (i, 0)),
            full(q_norm_w), full(kv_norm_w), full(wq_p), full(wqs_p), full(wk_p), full(wv_p),
        ],
        out_specs=[
            pl.BlockSpec((tm, MLA_HEADS * LANES), lambda i: (i, 0)),
            pl.BlockSpec((tm, MLA_HEADS * LANES), lambda i: (i, 0)),
            pl.BlockSpec((MLA_HEADS * LANES, tm), lambda i: (0, i)),
        ],
        out_shape=[
            jax.ShapeDtypeStruct((T, MLA_HEADS * LANES), BF16),
            jax.ShapeDtypeStruct((T, MLA_HEADS * LANES), BF16),
            jax.ShapeDtypeStruct((MLA_HEADS * LANES, T), BF16),
        ],
        compiler_params=pltpu.CompilerParams(
            dimension_semantics=("arbitrary",), vmem_limit_bytes=VMEM_LIMIT),
        name="mla_prep",
    )(pos_row, inv_freq, lat, q_norm_w, kv_norm_w, wq_p, wqs_p, wk_p, wv_p)


def _attn_kernel(q_ref, k_ref, vt_ref, o_ref, s_ref, *, tk):
    tq = q_ref.shape[0]
    qi = pl.program_id(2)
    n_full = (qi * tq) // tk
    diag_off = qi * tq - n_full * tk
    lane = lax.broadcasted_iota(jnp.int32, (1, LANES), 1)
    q = q_ref[...]
    key_i = lax.broadcasted_iota(jnp.int32, (tk, tq), 0)
    qry_i = lax.broadcasted_iota(jnp.int32, (tk, tq), 1)
    nh = q_ref.shape[1] // LANES
    blks = [slice(h * LANES, (h + 1) * LANES) for h in range(nh)]

    def scores(j, slot):
        kv = pl.ds(pl.multiple_of(j * tk, tk), tk)
        for h in range(nh):
            s_ref[slot, h] = _dot_nt(k_ref[kv, blks[h]], q[:, blks[h]])

    def consume(j, slot, stats, masked):
        kv = pl.ds(pl.multiple_of(j * tk, tk), tk)
        out = []
        for h in range(nh):
            m_prev, acc = stats[h]
            s = s_ref[slot, h]
            if masked:
                s = jnp.where(key_i <= qry_i + diag_off, s, NEG_BIG)
            m_new = jnp.maximum(m_prev, jnp.max(s, axis=0, keepdims=True))
            alpha = jnp.exp2(m_prev - m_new)
            p = jnp.exp2(s - m_new).astype(BF16)
            acc = acc * alpha + _dot(vt_ref[blks[h], kv], p)
            out.append((m_new, acc))
        return tuple(out)

    def two_blocks(i, stats):
        j = 2 * i
        scores(j + 1, 1)
        stats = consume(j, 0, stats, False)
        scores(j + 2, 0)
        return consume(j + 1, 1, stats, False)

    def odd_block(stats):
        scores(n_full, 1)
        return consume(n_full - 1, 0, stats, False)

    init = tuple((jnp.full((1, tq), NEG_BIG, F32), jnp.zeros((LANES, tq), F32)) for _ in range(nh))
    scores(0, 0)
    stats = lax.fori_loop(0, n_full // 2, two_blocks, init)
    stats = lax.cond(n_full % 2 == 1, odd_block, lambda st: st, stats)
    stats = consume(n_full, n_full % 2, stats, True)
    outs = [(acc * (1.0 / acc[MLA_V:MLA_V + 1, :])).T for _, acc in stats]
    for pr in range(nh // 2):
        pair = jnp.where(lane < MLA_V, outs[2 * pr], pltpu.roll(outs[2 * pr + 1], MLA_V, axis=1))
        o_ref[:, pr * LANES:(pr + 1) * LANES] = pair.astype(o_ref.dtype)


def _attn(q3, k3, vt):
    B, S, _ = q3.shape
    tq = min(ATT_TQ, S)
    tk = min(ATT_TK, S)
    assert tk % tq == 0 and S % tk == 0
    nh = ATT_HEADS
    return pl.pallas_call(
        functools.partial(_attn_kernel, tk=tk),
        grid=(B, MLA_HEADS // nh, S // tq),
        in_specs=[
            pl.BlockSpec((None, tq, nh * LANES), lambda b, h, i: (b, i, h)),
            pl.BlockSpec((None, S, nh * LANES), lambda b, h, i: (b, 0, h)),
            pl.BlockSpec((nh * LANES, S), lambda b, h, i: (h, b)),
        ],
        out_specs=pl.BlockSpec((None, tq, nh * MLA_V), lambda b, h, i: (b, i, h)),
        out_shape=jax.ShapeDtypeStruct((B, S, MLA_W), BF16),
        scratch_shapes=[pltpu.VMEM((2, nh, tk, tq), F32)],
        compiler_params=pltpu.CompilerParams(
            dimension_semantics=("arbitrary", "arbitrary", "arbitrary"), vmem_limit_bytes=VMEM_LIMIT),
        name="attn",
    )(q3, k3, vt)


def _tail_kernel(x_ref, yg_ref, om_ref, p_ref, monw_ref, wout_ref, mlpnw_ref, wup_ref, wdown_ref,
                 wpp_ref, postnw_ref, gatenw_ref, wpg_ref, finnw_ref, o_ref):
    ymla = _rms(om_ref[...].astype(F32), monw_ref[...]).astype(BF16)
    h = x_ref[...] + _dot(yg_ref[...], wout_ref[0:GDN_V_W, :]) + _dot(ymla, wout_ref[GDN_V_W:, :])
    u = _rms(h, mlpnw_ref[...]).astype(BF16)
    ff_blk = 1024
    acc = jnp.zeros_like(h)
    for j in range(D_FF // ff_blk):
        cols = slice(j * ff_blk, (j + 1) * ff_blk)
        hid = jnp.maximum(_dot(u, wup_ref[:, cols]), 0.0)
        acc = acc + _dot((hid * hid).astype(BF16), wdown_ref[cols, :])
    h = h + acc
    e = _rms(_dot(p_ref[...].astype(BF16), wpp_ref[...]), postnw_ref[...])
    gate = _sigmoid(_dot(_rms(h, gatenw_ref[...]).astype(BF16), wpg_ref[...]))
    h = h + gate * e
    o_ref[...] = _rms(h, finnw_ref[...])


def _tail(x2, yg, om, p2, mla_out_norm_w, w_out, mlp_norm_w, w_up, w_down, w_ple_proj,
          ple_post_norm_w, ple_gate_norm_w, w_ple_gate, final_norm_w):
    T = x2.shape[0]
    tm = min(TOK_TILE, T)
    tok = lambda w: pl.BlockSpec((tm, w), lambda i: (i, 0))
    const = lambda a: pl.BlockSpec(a.shape, lambda i: (0, 0), pipeline_mode=pl.Buffered(1))
    return pl.pallas_call(
        _tail_kernel,
        grid=(T // tm,),
        in_specs=[
            tok(D_MODEL), tok(GDN_V_W), tok(MLA_W), tok(PLE_DIM),
            const(mla_out_norm_w), const(w_out), const(mlp_norm_w), const(w_up), const(w_down),
            const(w_ple_proj), const(ple_post_norm_w), const(ple_gate_norm_w), const(w_ple_gate),
            const(final_norm_w),
        ],
        out_specs=tok(D_MODEL),
        out_shape=jax.ShapeDtypeStruct((T, D_MODEL), F32),
        compiler_params=pltpu.CompilerParams(
            dimension_semantics=("arbitrary",), vmem_limit_bytes=VMEM_LIMIT),
        name="tail",
    )(x2, yg, om, p2, mla_out_norm_w, w_out, mlp_norm_w, w_up, w_down, w_ple_proj,
      ple_post_norm_w, ple_gate_norm_w, w_ple_gate, final_norm_w)


def _pack_w_in(w):
    o_b = QKV_W + GDN_V_W
    o_cq = o_b + 2 * GDN_HEADS
    n_lat = MLA_Q_RANK + MLA_KV_RANK + MLA_ROPE
    pad = jnp.zeros((D_MODEL, LAT_W - n_lat - 2 * GDN_HEADS), w.dtype)
    return jnp.concatenate([w[:, :o_b], w[:, o_cq:o_cq + n_lat], w[:, o_b:o_cq], pad], axis=1).astype(BF16)


def _gate_lane_vector(per_head):
    return jnp.zeros((1, LANES), F32).at[0, A_OFF:A_OFF + GDN_HEADS].set(per_head.astype(F32))


def _pack_mla_weights(w_q_b, w_kv_b):
    wq = w_q_b.reshape(MLA_Q_RANK, MLA_HEADS, MLA_NOPE + MLA_ROPE)
    wq = jnp.pad(wq, ((0, 0), (0, 0), (0, LANES - MLA_NOPE - MLA_ROPE)))
    wkv = w_kv_b.reshape(MLA_KV_RANK, MLA_HEADS, MLA_NOPE + MLA_V)
    wk = jnp.pad(wkv[:, :, :MLA_NOPE], ((0, 0), (0, 0), (0, LANES - MLA_NOPE)))
    wv = jnp.pad(wkv[:, :, MLA_NOPE:], ((0, 0), (0, 0), (0, LANES - MLA_V)))
    half = MLA_ROPE // 2
    zeros = lambda n: jnp.zeros((MLA_Q_RANK, MLA_HEADS, n), wq.dtype)
    wq_sw = jnp.concatenate([zeros(MLA_NOPE), wq[:, :, MLA_NOPE + half:MLA_NOPE + MLA_ROPE],
                             wq[:, :, MLA_NOPE:MLA_NOPE + half], zeros(LANES - MLA_NOPE - MLA_ROPE)], axis=2)
    return (wq.reshape(MLA_Q_RANK, MLA_HEADS * LANES).astype(BF16),
            wq_sw.reshape(MLA_Q_RANK, MLA_HEADS * LANES).astype(BF16),
            wk.reshape(MLA_KV_RANK, MLA_HEADS * LANES).astype(BF16),
            wv.reshape(MLA_KV_RANK, MLA_HEADS * LANES).T.astype(BF16))


def kernel(x, p, positions, mix_norm_w, w_in, conv_w, A_log, dt_bias, gdn_norm_w, q_norm_w, w_q_b,
           kv_norm_w, w_kv_b, mla_out_norm_w, w_out, mlp_norm_w, w_up, w_down, w_ple_proj,
           ple_post_norm_w, ple_gate_norm_w, w_ple_gate, final_norm_w):
    B, S, _ = x.shape
    T = B * S
    assert w_in.shape[0] == 1, "one layer"
    row = lambda a: a.reshape(1, -1).astype(F32)
    x2 = x.reshape(T, D_MODEL)

    qkv, zg, lat = _inproj(x2, row(mix_norm_w[0]), _pack_w_in(w_in[0]), conv_w[0].astype(F32),
                           _gate_lane_vector(A_log[0]), _gate_lane_vector(dt_bias[0]), S)

    factors = _gdn_prep(qkv.reshape(B, S, QKV_W), lat.reshape(B, S, LAT_W))
    y_gdn = _gdn_scan(*factors, zg.reshape(B, S, GDN_V_W), row(jnp.tile(gdn_norm_w[0], GDN_HEADS)))

    half = MLA_ROPE // 2
    inv_freq = (ROPE_THETA ** (-jnp.arange(0, MLA_ROPE, 2, dtype=F32) / MLA_ROPE)).reshape(half, 1)
    q_att, k_att, v_att = _mla_prep(positions.reshape(1, T), inv_freq, lat, row(q_norm_w[0]),
                                    row(kv_norm_w[0]), *_pack_mla_weights(w_q_b[0], w_kv_b[0]))
    o_mla = _attn(q_att.reshape(B, S, -1), k_att.reshape(B, S, -1), v_att)

    out = _tail(x2, y_gdn.reshape(T, GDN_V_W), o_mla.reshape(T, MLA_W), p[0].reshape(T, PLE_DIM),
                row(mla_out_norm_w[0]), w_out[0].astype(BF16), row(mlp_norm_w[0]),
                w_up[0].astype(BF16), w_down[0].astype(BF16), w_ple_proj[0].astype(BF16),
                row(ple_post_norm_w[0]), row(ple_gate_norm_w[0]), w_ple_gate[0].astype(BF16),
                row(final_norm_w))
    return out.reshape(B, S, D_MODEL)
```

```python
import functools

import jax
import jax.numpy as jnp
from jax import lax
from jax.experimental import pallas as pl
from jax.experimental.pallas import tpu as pltpu

F32 = jnp.float32
BF16 = jnp.bfloat16

D_MODEL = 1024
PLE_DIM = 256
GDN_HEADS = 8
GDN_DK = 64
GDN_DV = 64
GDN_QK_W = GDN_HEADS * GDN_DK
GDN_V_W = GDN_HEADS * GDN_DV
GDN_CONV = 4
GDN_CHUNK = 64
MLA_HEADS = 8
MLA_NOPE = 64
MLA_ROPE = 32
MLA_V = 64
MLA_W = MLA_HEADS * MLA_V
MLA_Q_RANK = 256
MLA_KV_RANK = 128
ROPE_THETA = 10000.0
D_FF = 4 * D_MODEL
EPS = 1e-6

LANES = 128
SUBLANES = 8
HEAD_PAIRS = GDN_HEADS // 2
QKV_W = 2 * GDN_QK_W + GDN_V_W
LAT_W = 512
IN_W_PAD = QKV_W + GDN_V_W + LAT_W
KPE_OFF, B_OFF, A_OFF = 0, MLA_ROPE, MLA_ROPE + GDN_HEADS
VMEM_LIMIT = 56 * 1024 * 1024

TOK_TILE = 512
CONV_COLS = 256
GDN_ROWS = 1024
INV_BASE = 8
SCAN_ROWS = 512
SCAN_BATCH = 4
ATT_TQ = 512
ATT_TK = 512
ATT_HEADS = 4
ATT_UNROLL = 4
ATT_AHEAD = 2
NEG_BIG = -1e30
LOG2E = 1.4426950408889634


def _dot(a, b):
    return jnp.dot(a, b, preferred_element_type=F32)


def _dot_nt(a, b):
    return lax.dot_general(a, b, (((1,), (1,)), ((), ())), preferred_element_type=F32)


def _dot_tn(a, b):
    return lax.dot_general(a, b, (((0,), (0,)), ((), ())), preferred_element_type=F32)


def _split2(x):
    hi = x.astype(BF16)
    lo = (x - hi.astype(F32)).astype(BF16)
    return hi, lo


def _split3(x):
    hi = x.astype(BF16)
    r1 = x - hi.astype(F32)
    mid = r1.astype(BF16)
    lo = (r1 - mid.astype(F32)).astype(BF16)
    return hi, mid, lo


def _dot_exact_rhs(a_bf16, x):
    hi, mid, lo = _split3(x)
    return _dot(a_bf16, hi) + _dot(a_bf16, mid) + _dot(a_bf16, lo)


def _dot_exact_lhs(x, b_bf16):
    hi, mid, lo = _split3(x)
    return _dot(hi, b_bf16) + _dot(mid, b_bf16) + _dot(lo, b_bf16)


def _rms(x, w):
    ms = jnp.mean(x * x, axis=-1, keepdims=True)
    return x * lax.rsqrt(ms + EPS) * w


def _sigmoid(x):
    return 1.0 / (1.0 + jnp.exp(-x))


def _silu(x):
    h = 0.5 * x
    return h + h * jnp.tanh(h)


def _softplus(x):
    return jnp.maximum(x, 0.0) + jnp.log(1.0 + jnp.exp(-jnp.abs(x)))


def _chunk_tril(n):
    r = lax.broadcasted_iota(jnp.int32, (n, n), 0)
    c = lax.broadcasted_iota(jnp.int32, (n, n), 1)
    return jnp.where(((r // GDN_CHUNK) == (c // GDN_CHUNK)) & (c <= r), 1.0, 0.0).astype(BF16)


def _inproj_kernel(x_ref, nw_ref, w_ref, cw_ref, alog_ref, dtb_ref, qkv_ref, z_ref, lat_ref, halo_ref,
                   *, tiles_per_seq):
    tm = x_ref.shape[0]
    first = (pl.program_id(0) % tiles_per_seq) == 0
    u = _rms(x_ref[...], nw_ref[...]).astype(BF16)

    for c in range(QKV_W // CONV_COLS):
        cols = slice(c * CONV_COLS, (c + 1) * CONV_COLS)
        raw = _dot(u, w_ref[:, cols])
        prev = jnp.where(first, 0.0, halo_ref[:, cols])
        halo_ref[:, cols] = raw[tm - SUBLANES:, :]
        ext = jnp.concatenate([prev, raw], axis=0)
        cw = cw_ref[:, cols]
        acc = raw * cw[GDN_CONV - 1:GDN_CONV, :]
        for kk in range(GDN_CONV - 1):
            shifted = pltpu.roll(ext, GDN_CONV - 1 - kk, axis=0)[SUBLANES:, :]
            acc = acc + shifted * cw[kk:kk + 1, :]
        qkv_ref[:, cols] = _silu(acc)

    zf = _dot(u, w_ref[:, QKV_W:QKV_W + GDN_V_W])
    z_ref[...] = _silu(zf).astype(z_ref.dtype)

    lat = _dot(u, w_ref[:, QKV_W + GDN_V_W:IN_W_PAD])
    lat_ref[:, :LAT_W - LANES] = lat[:, :LAT_W - LANES]
    l3 = lat[:, LAT_W - LANES:]
    lane = lax.broadcasted_iota(jnp.int32, (1, LANES), 1)
    is_b = (lane >= B_OFF) & (lane < B_OFF + GDN_HEADS)
    is_a = (lane >= A_OFF) & (lane < A_OFF + GDN_HEADS)
    g_raw = jnp.where(is_a, -(jnp.exp(alog_ref[...]) * _softplus(l3 + dtb_ref[...])), 0.0)
    g_cum = _dot_exact_rhs(_chunk_tril(tm), g_raw)
    lat_ref[:, LAT_W - LANES:] = jnp.where(is_b, _sigmoid(l3), jnp.where(is_a, g_cum, l3))


def _inproj(x2, mix_norm_w, w_in_p, conv_w, alog_vec, dtb_vec, seq_len):
    T = x2.shape[0]
    tm = min(TOK_TILE, seq_len)
    assert seq_len % tm == 0 and tm % GDN_CHUNK == 0
    nt = T // tm
    const = lambda a: pl.BlockSpec(a.shape, lambda i: (0, 0))
    return pl.pallas_call(
        functools.partial(_inproj_kernel, tiles_per_seq=seq_len // tm),
        grid=(nt,),
        in_specs=[
            pl.BlockSpec((tm, D_MODEL), lambda i: (i, 0)),
            const(mix_norm_w), const(w_in_p), const(conv_w), const(alog_vec), const(dtb_vec),
        ],
        out_specs=[
            pl.BlockSpec((tm, QKV_W), lambda i: (i, 0)),
            pl.BlockSpec((tm, GDN_V_W), lambda i: (i, 0)),
            pl.BlockSpec((tm, LAT_W), lambda i: (i, 0)),
        ],
        out_shape=[
            jax.ShapeDtypeStruct((T, QKV_W), F32),
            jax.ShapeDtypeStruct((T, GDN_V_W), BF16),
            jax.ShapeDtypeStruct((T, LAT_W), F32),
        ],
        scratch_shapes=[pltpu.VMEM((SUBLANES, QKV_W), F32)],
        compiler_params=pltpu.CompilerParams(
            dimension_semantics=("arbitrary",), vmem_limit_bytes=VMEM_LIMIT),
        name="inproj",
    )(x2, mix_norm_w, w_in_p, conv_w, alog_vec, dtb_vec)


def _gdn_prep_kernel(q_ref, k_ref, v_ref, lat_ref, u_ref, w_ref, qd_ref, kd_ref, a_ref, sd_ref):
    C = GDN_CHUNK
    S = q_ref.shape[0]
    R = min(GDN_ROWS, S)
    G = R // C
    hp = pl.program_id(1)

    lane = lax.broadcasted_iota(jnp.int32, (1, LANES), 1)
    head1 = lane >= C
    m0 = jnp.where(head1, 0.0, 1.0).astype(BF16)
    m1 = jnp.where(head1, 1.0, 0.0).astype(BF16)
    r128 = lax.broadcasted_iota(jnp.int32, (LANES, LANES), 0)
    c128 = lax.broadcasted_iota(jnp.int32, (LANES, LANES), 1)
    ones_bd = jnp.where((r128 // C) == (c128 // C), 1.0, 0.0).astype(BF16)
    rC = lax.broadcasted_iota(jnp.int32, (C, LANES), 0)
    cC = lax.broadcasted_iota(jnp.int32, (C, LANES), 1) % C
    eye2 = rC == cC
    tril2 = rC >= cC
    strict2 = rC > cC
    eye2f = jnp.where(eye2, 1.0, 0.0)
    ones_cc = jnp.ones((C, C), BF16)
    sel_r = lax.broadcasted_iota(jnp.int32, (LANES, 2 * LANES), 0)
    sel_c = lax.broadcasted_iota(jnp.int32, (LANES, 2 * LANES), 1)
    sel_src = jnp.where(sel_c < LANES, B_OFF, A_OFF) + 2 * hp + ((sel_c % LANES) // C)
    sel = jnp.where(sel_r == sel_src, 1.0, 0.0).astype(BF16)

    def bd16(m):
        return jnp.concatenate([m * m0, m * m1], axis=0)

    def bd_parts(m):
        hi, lo = _split2(m)
        return bd16(hi), bd16(lo)

    def dot_hi(a, b_hi, b_lo):
        a_hi, a_lo = _split2(a)
        return _dot(a_hi, b_hi) + _dot(a_hi, b_lo) + _dot(a_lo, b_hi)

    def merge_dot(a, b):
        return _dot(a.astype(BF16), bd16(b.astype(BF16)))

    diag_blk = (rC // INV_BASE) == (cC // INV_BASE)
    off_blks = []
    size = INV_BASE
    while size < C:
        off_blks.append(((rC // (2 * size)) == (cC // (2 * size))) & ((rC // size) % 2 == 1) & ((cC // size) % 2 == 0))
        size *= 2

    def tile(t, carry):
        rows = pl.ds(pl.multiple_of(t * R, R), R)
        q = q_ref[rows, :]
        k = k_ref[rows, :]
        v = v_ref[rows, :]
        q = q * lax.rsqrt(_dot((q * q).astype(BF16), ones_bd) + EPS) * (GDN_DK ** -0.5)
        k = k * lax.rsqrt(_dot((k * k).astype(BF16), ones_bd) + EPS)
        bg = _dot_exact_lhs(lat_ref[rows, :], sel)
        beta = bg[:, :LANES]
        gc = bg[:, LANES:]
        eg = jnp.exp(gc)
        kb = k * beta
        vb16 = (v * beta).astype(BF16)
        kbg16 = (kb * eg).astype(BF16)
        qd_ref[rows, :] = (q * eg).astype(BF16)
        k16 = k.astype(BF16)
        q16 = q.astype(BF16)

        ch = lambda arr, g: arr[g * C:(g + 1) * C, :]
        rng = range(G)
        kkqk = [_dot_nt(jnp.concatenate([ch(k16, g), ch(q16, g)], axis=0), bd16(ch(k16, g))) for g in rng]
        g_row = [_dot_exact_rhs(ones_cc, jnp.where(eye2, ch(gc, g), 0.0)) for g in rng]
        dm = [jnp.where(tril2, jnp.exp(jnp.where(tril2, ch(gc, g) - g_row[g], 0.0)), 0.0) for g in rng]
        for g in rng:
            a_ref[pl.ds(pl.multiple_of(t * R + g * C, C), C), :] = (kkqk[g][C:] * dm[g]).astype(BF16)
            g_last = ch(gc, g)[C - 1:C, :]
            kd_ref[pl.ds(pl.multiple_of(t * R + g * C, C), C), :] = (
                ch(k, g) * jnp.exp(g_last - ch(gc, g))).astype(BF16)
            sd_ref[pl.ds(pl.multiple_of((t * G + g) * SUBLANES, SUBLANES), SUBLANES), :] = jnp.broadcast_to(
                jnp.exp(g_last), (SUBLANES, LANES))
        low = [jnp.where(strict2, kkqk[g][:C] * dm[g] * ch(beta, g), 0.0) for g in rng]
        x = [jnp.where(diag_blk, -low[g], 0.0) for g in rng]
        p = [eye2f + x[g] for g in rng]
        y = [dot_hi(x[g], *bd_parts(x[g])) for g in rng]
        rhs = []
        for g in rng:
            p_hi, p_lo = bd_parts(p[g])
            y_hi, y_lo = bd_parts(y[g])
            rhs.append((jnp.concatenate([p_hi, y_hi], axis=1), jnp.concatenate([p_lo, y_lo], axis=1)))
        res = [dot_hi(y[g], *rhs[g]) for g in rng]
        p = [p[g] + res[g][:, :LANES] for g in rng]
        y = [res[g][:, LANES:] for g in rng]
        p = [p[g] + dot_hi(y[g], *bd_parts(p[g])) for g in rng]
        for off_blk in off_blks:
            m1 = [merge_dot(jnp.where(off_blk, low[g], 0.0), p[g]) for g in rng]
            p = [p[g] - merge_dot(p[g], m1[g]) for g in rng]
        uw = [_dot(p[g].astype(BF16), jnp.concatenate([bd16(ch(vb16, g)), bd16(ch(kbg16, g))], axis=1))
              for g in rng]
        for g in rng:
            dst = pl.ds(pl.multiple_of(t * R + g * C, C), C)
            u_ref[dst, :] = uw[g][:, :LANES]
            w_ref[dst, :] = uw[g][:, LANES:].astype(BF16)
        return carry

    lax.fori_loop(0, S // R, tile, 0)


def _gdn_prep(qkv3, lat3):
    B, S, _ = qkv3.shape
    HP = HEAD_PAIRS
    seq_blk = lambda off: pl.BlockSpec((None, S, LANES), lambda b, h, off=off: (b, 0, off + h))
    n_sd = S // GDN_CHUNK * SUBLANES
    return pl.pallas_call(
        _gdn_prep_kernel,
        grid=(B, HP),
        in_specs=[
            seq_blk(0), seq_blk(HP), seq_blk(2 * HP),
            pl.BlockSpec((None, S, LANES), lambda b, h: (b, 0, LAT_W // LANES - 1)),
        ],
        out_specs=[seq_blk(0)] * 5 + [pl.BlockSpec((None, n_sd, LANES), lambda b, h: (b, 0, h))],
        out_shape=[
            jax.ShapeDtypeStruct((B, S, GDN_V_W), F32),
            jax.ShapeDtypeStruct((B, S, GDN_V_W), BF16),
            jax.ShapeDtypeStruct((B, S, GDN_V_W), BF16),
            jax.ShapeDtypeStruct((B, S, GDN_V_W), BF16),
            jax.ShapeDtypeStruct((B, S, GDN_V_W), BF16),
            jax.ShapeDtypeStruct((B, n_sd, GDN_V_W), F32),
        ],
        compiler_params=pltpu.CompilerParams(
            dimension_semantics=("arbitrary", "arbitrary"), vmem_limit_bytes=VMEM_LIMIT),
        name="gdn_prep",
    )(qkv3, qkv3, qkv3, lat3)


def _gdn_scan_kernel(u_ref, w_ref, qd_ref, kd_ref, a_ref, sd_ref, z_ref, nw_ref, o_ref, state_ref):
    C = GDN_CHUNK
    nb, rt, _ = u_ref.shape

    @pl.when(pl.program_id(1) == 0)
    def _():
        state_ref[...] = jnp.zeros_like(state_ref)

    lane = lax.broadcasted_iota(jnp.int32, (1, LANES), 1)
    head1 = lane >= C
    m0 = jnp.where(head1, 0.0, 1.0).astype(BF16)
    m1 = jnp.where(head1, 1.0, 0.0).astype(BF16)
    r128 = lax.broadcasted_iota(jnp.int32, (LANES, LANES), 0)
    c128 = lax.broadcasted_iota(jnp.int32, (LANES, LANES), 1)
    bdmask = (r128 // C) == (c128 // C)
    ones_bd = jnp.where(bdmask, 1.0, 0.0).astype(BF16)

    chains = [(bb, hp) for bb in range(nb) for hp in range(HEAD_PAIRS)]
    blk = lambda hp: slice(hp * LANES, (hp + 1) * LANES)

    def bd16(m):
        return jnp.concatenate([m * m0, m * m1], axis=0)

    def chunk(n, carry):
        rows = pl.ds(pl.multiple_of(n * C, C), C)
        sd_rows = pl.ds(pl.multiple_of(n * SUBLANES, SUBLANES), SUBLANES)
        state = [state_ref[i] for i in range(len(chains))]
        res = [_dot(jnp.concatenate([w_ref[bb, rows, blk(hp)], qd_ref[bb, rows, blk(hp)]], axis=0),
                    state[i].astype(BF16)) for i, (bb, hp) in enumerate(chains)]
        v_new = [(u_ref[bb, rows, blk(hp)] - res[i][:C]).astype(BF16) for i, (bb, hp) in enumerate(chains)]
        upd = [_dot_tn(kd_ref[bb, rows, blk(hp)], v_new[i]) for i, (bb, hp) in enumerate(chains)]
        for i, (bb, hp) in enumerate(chains):
            sd = sd_ref[bb, sd_rows, blk(hp)][0:1, :]
            state_ref[i] = state[i] * sd + jnp.where(bdmask, upd[i], 0.0)
        o = [res[i][C:] + _dot(a_ref[bb, rows, blk(hp)], bd16(v_new[i])) for i, (bb, hp) in enumerate(chains)]
        for i, (bb, hp) in enumerate(chains):
            ms = _dot((o[i] * o[i]).astype(BF16), ones_bd) * (1.0 / GDN_DV)
            y = o[i] * lax.rsqrt(ms + EPS) * nw_ref[:, blk(hp)] * z_ref[bb, rows, blk(hp)].astype(F32)
            o_ref[bb, rows, blk(hp)] = y.astype(o_ref.dtype)
        return carry

    lax.fori_loop(0, rt // C, chunk, 0)


def _gdn_scan(u, w, qd, kd, a, sd, zg, norm_w8):
    B, S, W = u.shape
    rt = min(SCAN_ROWS, S)
    nb = min(SCAN_BATCH, B)
    assert B % nb == 0 and S % rt == 0
    n_sd = rt // GDN_CHUNK * SUBLANES
    seq = pl.BlockSpec((nb, rt, W), lambda b, t: (b, t, 0))
    return pl.pallas_call(
        _gdn_scan_kernel,
        grid=(B // nb, S // rt),
        in_specs=[seq, seq, seq, seq, seq,
                  pl.BlockSpec((nb, n_sd, W), lambda b, t: (b, t, 0)),
                  seq,
                  pl.BlockSpec((1, W), lambda b, t: (0, 0))],
        out_specs=seq,
        out_shape=jax.ShapeDtypeStruct((B, S, W), BF16),
        scratch_shapes=[pltpu.VMEM((nb * HEAD_PAIRS, LANES, LANES), F32)],
        compiler_params=pltpu.CompilerParams(
            dimension_semantics=("arbitrary", "arbitrary"), vmem_limit_bytes=VMEM_LIMIT),
        name="gdn_scan",
    )(u, w, qd, kd, a, sd, zg, norm_w8)


def _mla_prep_kernel(pos_ref, freq_ref, lat_ref, qnw_ref, kvnw_ref, wq_ref, wqs_ref, wk_ref, wv_ref,
                     q_ref, k_ref, v_ref):
    half = MLA_ROPE // 2
    lo, mid, hi = MLA_NOPE, MLA_NOPE + half, MLA_NOPE + MLA_ROPE
    scale = (MLA_NOPE + MLA_ROPE) ** -0.5 * LOG2E
    ang = freq_ref[...] * pos_ref[...].astype(F32)
    f_i = lax.broadcasted_iota(jnp.int32, (3 * half, LANES), 0) % half
    l_i = lax.broadcasted_iota(jnp.int32, (3 * half, LANES), 1)
    in_lo = (l_i >= lo) & (l_i < mid) & (l_i - lo == f_i)
    in_hi = (l_i >= mid) & (l_i < hi) & (l_i - mid == f_i)
    expand_cos = jnp.where(in_lo | in_hi, 1.0, 0.0).astype(BF16)
    expand_sin = jnp.where(in_lo, -1.0, jnp.where(in_hi, 1.0, 0.0)).astype(BF16)
    cos_tab = _dot_tn(jnp.concatenate(_split3(jnp.cos(ang)), axis=0), expand_cos)
    sin_tab = _dot_tn(jnp.concatenate(_split3(jnp.sin(ang)), axis=0), expand_sin)
    lane = lax.broadcasted_iota(jnp.int32, (1, LANES), 1)
    nope = jnp.where(lane < MLA_NOPE, 1.0, 0.0)

    lat = lat_ref[...]
    cq = _rms(lat[:, 0:MLA_Q_RANK], qnw_ref[...]).astype(BF16)
    ckv = _rms(lat[:, MLA_Q_RANK:MLA_Q_RANK + MLA_KV_RANK], kvnw_ref[...]).astype(BF16)
    q = _dot(cq, wq_ref[...])
    q_sw = _dot(cq, wqs_ref[...])
    k_nope = _dot(ckv, wk_ref[...])
    row_all = lax.broadcasted_iota(jnp.int32, (MLA_HEADS * LANES, 1), 0)
    ones_row = jnp.where(row_all % LANES == MLA_V, 1.0, 0.0)
    v_ref[...] = (_dot_nt(wv_ref[...], ckv) + ones_row).astype(v_ref.dtype)

    kpe = pltpu.roll(lat[:, LAT_W - LANES:], MLA_NOPE - KPE_OFF, axis=1)
    kpe_sw = jnp.where(lane < mid, pltpu.roll(kpe, LANES - half, axis=1), pltpu.roll(kpe, half, axis=1))
    k_rope = kpe * cos_tab + kpe_sw * sin_tab
    c_q = (cos_tab + nope) * scale
    s_q = sin_tab * scale
    for h in range(MLA_HEADS):
        blk = slice(h * LANES, (h + 1) * LANES)
        q_ref[:, blk] = (q[:, blk] * c_q + q_sw[:, blk] * s_q).astype(q_ref.dtype)
        k_ref[:, blk] = (k_nope[:, blk] + k_rope).astype(k_ref.dtype)


def _mla_prep(pos_row, inv_freq, lat, q_norm_w, kv_norm_w, wq_p, wqs_p, wk_p, wv_p):
    T = lat.shape[0]
    tm = min(TOK_TILE, T)
    full = lambda a: pl.BlockSpec(a.shape, lambda i: (0,) * a.ndim)
    return pl.pallas_call(
        _mla_prep_kernel,
        grid=(T // tm,),
        in_specs=[
            pl.BlockSpec((1, tm), lambda i: (0, i)),
            full(inv_freq),
            pl.BlockSpec((tm, LAT_W), lambda i: (i, 0)),
            full(q_norm_w), full(kv_norm_w), full(wq_p), full(wqs_p), full(wk_p), full(wv_p),
        ],
        out_specs=[
            pl.BlockSpec((tm, MLA_HEADS * LANES), lambda i: (i, 0)),
            pl.BlockSpec((tm, MLA_HEADS * LANES), lambda i: (i, 0)),
            pl.BlockSpec((MLA_HEADS * LANES, tm), lambda i: (0, i)),
        ],
        out_shape=[
            jax.ShapeDtypeStruct((T, MLA_HEADS * LANES), BF16),
            jax.ShapeDtypeStruct((T, MLA_HEADS * LANES), BF16),
            jax.ShapeDtypeStruct((MLA_HEADS * LANES, T), BF16),
        ],
        compiler_params=pltpu.CompilerParams(
            dimension_semantics=("arbitrary",), vmem_limit_bytes=VMEM_LIMIT),
        name="mla_prep",
    )(pos_row, inv_freq, lat, q_norm_w, kv_norm_w, wq_p, wqs_p, wk_p, wv_p)


def _attn_kernel(q_ref, k_ref, vt_ref, o_ref, s_ref, m_ref, acc_ref, *, tk):
    tq = q_ref.shape[0]
    qi = pl.program_id(2)
    n_full = (qi * tq) // tk
    diag_off = qi * tq - n_full * tk
    lane = lax.broadcasted_iota(jnp.int32, (1, LANES), 1)
    q = q_ref[...]
    key_i = lax.broadcasted_iota(jnp.int32, (tk, tq), 0)
    qry_i = lax.broadcasted_iota(jnp.int32, (tk, tq), 1)
    nh = q_ref.shape[1] // LANES
    blks = [slice(h * LANES, (h + 1) * LANES) for h in range(nh)]

    def scores(j, h):
        kv = pl.ds(pl.multiple_of(j * tk, tk), tk)
        s_ref[h] = _dot_nt(k_ref[kv, blks[h]], q[:, blks[h]])

    def consume(j, h, masked):
        kv = pl.ds(pl.multiple_of(j * tk, tk), tk)
        m_prev = m_ref[h][0:1, :]
        s = s_ref[h]
        if masked:
            s = jnp.where(key_i <= qry_i + diag_off, s, NEG_BIG)
        m_new = jnp.maximum(m_prev, jnp.max(s, axis=0, keepdims=True))
        alpha = jnp.exp2(m_prev - m_new)
        p = jnp.exp2(s - m_new).astype(BF16)
        m_ref[h] = jnp.broadcast_to(m_new, (SUBLANES, tq))
        acc_ref[h] = acc_ref[h] * alpha + _dot(vt_ref[blks[h], kv], p)

    def consume_and_prefetch(j, h, masked):
        consume(j, h, masked)
        nxt = h + ATT_AHEAD
        if nxt < nh:
            scores(j, nxt)
        elif not masked:
            scores(j + 1, nxt - nh)

    def blocks(j0, count):
        for j in range(count):
            for h in range(nh):
                consume_and_prefetch(j0 + j, h, False)

    def unrolled(i, carry):
        blocks(ATT_UNROLL * i, ATT_UNROLL)
        return carry

    m_ref[...] = jnp.full(m_ref.shape, NEG_BIG, F32)
    acc_ref[...] = jnp.zeros(acc_ref.shape, F32)
    for h in range(ATT_AHEAD):
        scores(0, h)
    lax.fori_loop(0, n_full // ATT_UNROLL, unrolled, 0)
    done = n_full - n_full % ATT_UNROLL
    size = ATT_UNROLL // 2
    while size >= 1:
        take = (n_full % (2 * size)) >= size

        @pl.when(take)
        def _(done=done, size=size):
            blocks(done, size)

        done = done + jnp.where(take, size, 0)
        size //= 2
    for h in range(nh):
        consume_and_prefetch(n_full, h, True)
    outs = []
    for h in range(nh):
        acc = acc_ref[h]
        outs.append((acc * (1.0 / acc[MLA_V:MLA_V + 1, :])).T)
    for pr in range(nh // 2):
        pair = jnp.where(lane < MLA_V, outs[2 * pr], pltpu.roll(outs[2 * pr + 1], MLA_V, axis=1))
        o_ref[:, pr * LANES:(pr + 1) * LANES] = pair.astype(o_ref.dtype)


def _attn(q3, k3, vt):
    B, S, _ = q3.shape
    tq = min(ATT_TQ, S)
    tk = min(ATT_TK, S)
    assert tk % tq == 0 and S % tk == 0
    nh = ATT_HEADS
    return pl.pallas_call(
        functools.partial(_attn_kernel, tk=tk),
        grid=(B, MLA_HEADS // nh, S // tq),
        in_specs=[
            pl.BlockSpec((None, tq, nh * LANES), lambda b, h, i: (b, i, h)),
            pl.BlockSpec((None, S, nh * LANES), lambda b, h, i: (b, 0, h)),
            pl.BlockSpec((nh * LANES, S), lambda b, h, i: (h, b)),
        ],
        out_specs=pl.BlockSpec((None, tq, nh * MLA_V), lambda b, h, i: (b, i, h)),
        out_shape=jax.ShapeDtypeStruct((B, S, MLA_W), BF16),
        scratch_shapes=[pltpu.VMEM((nh, tk, tq), F32),
                        pltpu.VMEM((nh, SUBLANES, tq), F32),
                        pltpu.VMEM((nh, LANES, tq), F32)],
        compiler_params=pltpu.CompilerParams(
            dimension_semantics=("arbitrary", "arbitrary", "arbitrary"), vmem_limit_bytes=VMEM_LIMIT),
        name="attn",
    )(q3, k3, vt)


def _tail_kernel(x_ref, yg_ref, om_ref, p_ref, monw_ref, wout_ref, mlpnw_ref, wup_ref, wdown_ref,
                 wpp_ref, postnw_ref, gatenw_ref, wpg_ref, finnw_ref, o_ref):
    ymla = _rms(om_ref[...].astype(F32), monw_ref[...]).astype(BF16)
    h = x_ref[...] + _dot(yg_ref[...], wout_ref[0:GDN_V_W, :]) + _dot(ymla, wout_ref[GDN_V_W:, :])
    u = _rms(h, mlpnw_ref[...]).astype(BF16)
    ff_blk = 1024
    acc = jnp.zeros_like(h)
    for j in range(D_FF // ff_blk):
        cols = slice(j * ff_blk, (j + 1) * ff_blk)
        hid = jnp.maximum(_dot(u, wup_ref[:, cols]), 0.0)
        acc = acc + _dot((hid * hid).astype(BF16), wdown_ref[cols, :])
    h = h + acc
    e = _rms(_dot(p_ref[...].astype(BF16), wpp_ref[...]), postnw_ref[...])
    gate = _sigmoid(_dot(_rms(h, gatenw_ref[...]).astype(BF16), wpg_ref[...]))
    h = h + gate * e
    o_ref[...] = _rms(h, finnw_ref[...])


def _tail(x2, yg, om, p2, mla_out_norm_w, w_out, mlp_norm_w, w_up, w_down, w_ple_proj,
          ple_post_norm_w, ple_gate_norm_w, w_ple_gate, final_norm_w):
    T = x2.shape[0]
    tm = min(TOK_TILE, T)
    tok = lambda w: pl.BlockSpec((tm, w), lambda i: (i, 0))
    const = lambda a: pl.BlockSpec(a.shape, lambda i: (0, 0), pipeline_mode=pl.Buffered(1))
    return pl.pallas_call(
        _tail_kernel,
        grid=(T // tm,),
        in_specs=[
            tok(D_MODEL), tok(GDN_V_W), tok(MLA_W), tok(PLE_DIM),
            const(mla_out_norm_w), const(w_out), const(mlp_norm_w), const(w_up), const(w_down),
            const(w_ple_proj), const(ple_post_norm_w), const(ple_gate_norm_w), const(w_ple_gate),
            const(final_norm_w),
        ],
        out_specs=tok(D_MODEL),
        out_shape=jax.ShapeDtypeStruct((T, D_MODEL), F32),
        compiler_params=pltpu.CompilerParams(
            dimension_semantics=("arbitrary",), vmem_limit_bytes=VMEM_LIMIT),
        name="tail",
    )(x2, yg, om, p2, mla_out_norm_w, w_out, mlp_norm_w, w_up, w_down, w_ple_proj,
      ple_post_norm_w, ple_gate_norm_w, w_ple_gate, final_norm_w)


def _pack_w_in(w):
    o_b = QKV_W + GDN_V_W
    o_cq = o_b + 2 * GDN_HEADS
    n_lat = MLA_Q_RANK + MLA_KV_RANK + MLA_ROPE
    pad = jnp.zeros((D_MODEL, LAT_W - n_lat - 2 * GDN_HEADS), w.dtype)
    return jnp.concatenate([w[:, :o_b], w[:, o_cq:o_cq + n_lat], w[:, o_b:o_cq], pad], axis=1).astype(BF16)


def _gate_lane_vector(per_head):
    return jnp.zeros((1, LANES), F32).at[0, A_OFF:A_OFF + GDN_HEADS].set(per_head.astype(F32))


def _pack_mla_weights(w_q_b, w_kv_b):
    wq = w_q_b.reshape(MLA_Q_RANK, MLA_HEADS, MLA_NOPE + MLA_ROPE)
    wq = jnp.pad(wq, ((0, 0), (0, 0), (0, LANES - MLA_NOPE - MLA_ROPE)))
    wkv = w_kv_b.reshape(MLA_KV_RANK, MLA_HEADS, MLA_NOPE + MLA_V)
    wk = jnp.pad(wkv[:, :, :MLA_NOPE], ((0, 0), (0, 0), (0, LANES - MLA_NOPE)))
    wv = jnp.pad(wkv[:, :, MLA_NOPE:], ((0, 0), (0, 0), (0, LANES - MLA_V)))
    half = MLA_ROPE // 2
    zeros = lambda n: jnp.zeros((MLA_Q_RANK, MLA_HEADS, n), wq.dtype)
    wq_sw = jnp.concatenate([zeros(MLA_NOPE), wq[:, :, MLA_NOPE + half:MLA_NOPE + MLA_ROPE],
                             wq[:, :, MLA_NOPE:MLA_NOPE + half], zeros(LANES - MLA_NOPE - MLA_ROPE)], axis=2)
    return (wq.reshape(MLA_Q_RANK, MLA_HEADS * LANES).astype(BF16),
            wq_sw.reshape(MLA_Q_RANK, MLA_HEADS * LANES).astype(BF16),
            wk.reshape(MLA_KV_RANK, MLA_HEADS * LANES).astype(BF16),
            wv.reshape(MLA_KV_RANK, MLA_HEADS * LANES).T.astype(BF16))


def kernel(x, p, positions, mix_norm_w, w_in, conv_w, A_log, dt_bias, gdn_norm_w, q_norm_w, w_q_b,
           kv_norm_w, w_kv_b, mla_out_norm_w, w_out, mlp_norm_w, w_up, w_down, w_ple_proj,
           ple_post_norm_w, ple_gate_norm_w, w_ple_gate, final_norm_w):
    B, S, _ = x.shape
    T = B * S
    assert w_in.shape[0] == 1, "one layer"
    row = lambda a: a.reshape(1, -1).astype(F32)
    x2 = x.reshape(T, D_MODEL)

    qkv, zg, lat = _inproj(x2, row(mix_norm_w[0]), _pack_w_in(w_in[0]), conv_w[0].astype(F32),
                           _gate_lane_vector(A_log[0]), _gate_lane_vector(dt_bias[0]), S)

    factors = _gdn_prep(qkv.reshape(B, S, QKV_W), lat.reshape(B, S, LAT_W))
    y_gdn = _gdn_scan(*factors, zg.reshape(B, S, GDN_V_W), row(jnp.tile(gdn_norm_w[0], GDN_HEADS)))

    half = MLA_ROPE // 2
    inv_freq = (ROPE_THETA ** (-jnp.arange(0, MLA_ROPE, 2, dtype=F32) / MLA_ROPE)).reshape(half, 1)
    q_att, k_att, v_att = _mla_prep(positions.reshape(1, T), inv_freq, lat, row(q_norm_w[0]),
                                    row(kv_norm_w[0]), *_pack_mla_weights(w_q_b[0], w_kv_b[0]))
    o_mla = _attn(q_att.reshape(B, S, -1), k_att.reshape(B, S, -1), v_att)

    out = _tail(x2, y_gdn.reshape(T, GDN_V_W), o_mla.reshape(T, MLA_W), p[0].reshape(T, PLE_DIM),
                row(mla_out_norm_w[0]), w_out[0].astype(BF16), row(mlp_norm_w[0]),
                w_up[0].astype(BF16), w_down[0].astype(BF16), w_ple_proj[0].astype(BF16),
                row(ple_post_norm_w[0]), row(ple_gate_norm_w[0]), w_ple_gate[0].astype(BF16),
                row(final_norm_w))
    return out.reshape(B, S, D_MODEL)
```

```python
import functools

import jax
import jax.numpy as jnp
from jax import lax
from jax.experimental import pallas as pl
from jax.experimental.pallas import tpu as pltpu

F32 = jnp.float32
BF16 = jnp.bfloat16

D_MODEL = 1024
PLE_DIM = 256
GDN_HEADS = 8
GDN_DK = 64
GDN_DV = 64
GDN_QK_W = GDN_HEADS * GDN_DK
GDN_V_W = GDN_HEADS * GDN_DV
GDN_CONV = 4
GDN_CHUNK = 64
MLA_HEADS = 8
MLA_NOPE = 64
MLA_ROPE = 32
MLA_V = 64
MLA_W = MLA_HEADS * MLA_V
MLA_Q_RANK = 256
MLA_KV_RANK = 128
ROPE_THETA = 10000.0
D_FF = 4 * D_MODEL
EPS = 1e-6

LANES = 128
SUBLANES = 8
HEAD_PAIRS = GDN_HEADS // 2
QKV_W = 2 * GDN_QK_W + GDN_V_W
LAT_W = 512
IN_W_PAD = QKV_W + GDN_V_W + LAT_W
KPE_OFF, B_OFF, A_OFF = 0, MLA_ROPE, MLA_ROPE + GDN_HEADS
VMEM_LIMIT = 56 * 1024 * 1024

TOK_TILE = 512
CUM_ROWS = 256
CONV_COLS = 512
GDN_ROWS = 1024
INV_BASE = 8
SCAN_ROWS = 512
SCAN_BATCH = 4
ATT_TQ = 512
ATT_TK = 512
ATT_HEADS = 4
ATT_UNROLL = 4
ATT_AHEAD = 2
NEG_BIG = -1e30
LOG2E = 1.4426950408889634


def _dot(a, b):
    return jnp.dot(a, b, preferred_element_type=F32)


def _dot_nt(a, b):
    return lax.dot_general(a, b, (((1,), (1,)), ((), ())), preferred_element_type=F32)


def _dot_tn(a, b):
    return lax.dot_general(a, b, (((0,), (0,)), ((), ())), preferred_element_type=F32)


def _split2(x):
    hi = x.astype(BF16)
    lo = (x - hi.astype(F32)).astype(BF16)
    return hi, lo


def _split3(x):
    hi = x.astype(BF16)
    r1 = x - hi.astype(F32)
    mid = r1.astype(BF16)
    lo = (r1 - mid.astype(F32)).astype(BF16)
    return hi, mid, lo


def _dot_exact_rhs(a_bf16, x):
    return _dot(jnp.concatenate([a_bf16] * 3, axis=1), jnp.concatenate(_split3(x), axis=0))


def _dot_exact_lhs(x, b_bf16):
    return _dot(jnp.concatenate(_split3(x), axis=1), jnp.concatenate([b_bf16] * 3, axis=0))


def _rms(x, w):
    ms = jnp.mean(x * x, axis=-1, keepdims=True)
    return x * lax.rsqrt(ms + EPS) * w


def _sigmoid(x):
    return 1.0 / (1.0 + jnp.exp(-x))


def _silu(x):
    h = 0.5 * x
    return h + h * jnp.tanh(h)


def _softplus(x):
    return jnp.maximum(x, 0.0) + jnp.log(1.0 + jnp.exp(-jnp.abs(x)))


def _chunk_tril(n):
    r = lax.broadcasted_iota(jnp.int32, (n, n), 0)
    c = lax.broadcasted_iota(jnp.int32, (n, n), 1)
    return jnp.where(((r // GDN_CHUNK) == (c // GDN_CHUNK)) & (c <= r), 1.0, 0.0).astype(BF16)


def _inproj_kernel(x_ref, nw_ref, w_ref, cw_ref, alog_ref, dtb_ref, qkv_ref, z_ref, lat_ref, halo_ref,
                   *, tiles_per_seq):
    tm = x_ref.shape[0]
    first = (pl.program_id(0) % tiles_per_seq) == 0
    u = _rms(x_ref[...], nw_ref[...]).astype(BF16)

    def conv_chunk(c):
        cols = slice(c * CONV_COLS, (c + 1) * CONV_COLS)
        raw = _dot(u, w_ref[:, cols])
        prev = jnp.where(first, 0.0, halo_ref[:, cols])
        halo_ref[:, cols] = raw[tm - SUBLANES:, :]
        ext = jnp.concatenate([prev, raw], axis=0)
        cw = cw_ref[:, cols]
        acc = raw * cw[GDN_CONV - 1:GDN_CONV, :]
        for kk in range(GDN_CONV - 1):
            shifted = pltpu.roll(ext, GDN_CONV - 1 - kk, axis=0)[SUBLANES:, :]
            acc = acc + shifted * cw[kk:kk + 1, :]
        qkv_ref[:, cols] = _silu(acc)

    def gate_z():
        zf = _dot(u, w_ref[:, QKV_W:QKV_W + GDN_V_W])
        z_ref[...] = _silu(zf).astype(z_ref.dtype)

    def latent_head():
        lat_ref[:, :LAT_W - LANES] = _dot(u, w_ref[:, QKV_W + GDN_V_W:IN_W_PAD - LANES])

    def latent_gates():
        l3 = _dot(u, w_ref[:, IN_W_PAD - LANES:IN_W_PAD])
        lane = lax.broadcasted_iota(jnp.int32, (1, LANES), 1)
        is_b = (lane >= B_OFF) & (lane < B_OFF + GDN_HEADS)
        is_a = (lane >= A_OFF) & (lane < A_OFF + GDN_HEADS)
        g_raw = jnp.where(is_a, -(jnp.exp(alog_ref[...]) * _softplus(l3 + dtb_ref[...])), 0.0)
        hi, mid, lo = (t.astype(F32) for t in _split3(g_raw))
        packed = (hi + pltpu.roll(mid, GDN_HEADS, axis=1) + pltpu.roll(lo, 2 * GDN_HEADS, axis=1)).astype(BF16)
        tril = _chunk_tril(CUM_ROWS)
        cum = jnp.concatenate([_dot(tril, packed[r:r + CUM_ROWS, :]) for r in range(0, tm, CUM_ROWS)], axis=0)
        g_cum = cum + pltpu.roll(cum, LANES - GDN_HEADS, axis=1) + pltpu.roll(cum, LANES - 2 * GDN_HEADS, axis=1)
        lat_ref[:, LAT_W - LANES:] = jnp.where(is_b, _sigmoid(l3), jnp.where(is_a, g_cum, l3))

    fillers = [latent_gates, gate_z, latent_head]
    for c in range(QKV_W // CONV_COLS):
        if fillers:
            fillers.pop(0)()
        conv_chunk(c)
    for f in fillers:
        f()


def _inproj(x2, mix_norm_w, w_in_p, conv_w, alog_vec, dtb_vec, seq_len):
    T = x2.shape[0]
    tm = min(TOK_TILE, seq_len)
    assert seq_len % tm == 0 and tm % GDN_CHUNK == 0
    nt = T // tm
    const = lambda a: pl.BlockSpec(a.shape, lambda i: (0, 0))
    return pl.pallas_call(
        functools.partial(_inproj_kernel, tiles_per_seq=seq_len // tm),
        grid=(nt,),
        in_specs=[
            pl.BlockSpec((tm, D_MODEL), lambda i: (i, 0)),
            const(mix_norm_w), const(w_in_p), const(conv_w), const(alog_vec), const(dtb_vec),
        ],
        out_specs=[
            pl.BlockSpec((tm, QKV_W), lambda i: (i, 0)),
            pl.BlockSpec((tm, GDN_V_W), lambda i: (i, 0)),
            pl.BlockSpec((tm, LAT_W), lambda i: (i, 0)),
        ],
        out_shape=[
            jax.ShapeDtypeStruct((T, QKV_W), F32),
            jax.ShapeDtypeStruct((T, GDN_V_W), BF16),
            jax.ShapeDtypeStruct((T, LAT_W), F32),
        ],
        scratch_shapes=[pltpu.VMEM((SUBLANES, QKV_W), F32)],
        compiler_params=pltpu.CompilerParams(
            dimension_semantics=("arbitrary",), vmem_limit_bytes=VMEM_LIMIT),
        name="inproj",
    )(x2, mix_norm_w, w_in_p, conv_w, alog_vec, dtb_vec)


def _gdn_prep_kernel(q_ref, k_ref, v_ref, lat_ref, u_ref, w_ref, qd_ref, kd_ref, a_ref, sd_ref):
    C = GDN_CHUNK
    S = q_ref.shape[0]
    R = min(GDN_ROWS, S)
    G = R // C
    hp = pl.program_id(1)

    lane = lax.broadcasted_iota(jnp.int32, (1, LANES), 1)
    head1 = lane >= C
    m0 = jnp.where(head1, 0.0, 1.0).astype(BF16)
    m1 = jnp.where(head1, 1.0, 0.0).astype(BF16)
    r128 = lax.broadcasted_iota(jnp.int32, (LANES, LANES), 0)
    c128 = lax.broadcasted_iota(jnp.int32, (LANES, LANES), 1)
    ones_bd = jnp.where((r128 // C) == (c128 // C), 1.0, 0.0).astype(BF16)
    rC = lax.broadcasted_iota(jnp.int32, (C, LANES), 0)
    cC = lax.broadcasted_iota(jnp.int32, (C, LANES), 1) % C
    eye2 = rC == cC
    tril2 = rC >= cC
    strict2 = rC > cC
    eye2f = jnp.where(eye2, 1.0, 0.0)
    ones_cc = jnp.ones((C, C), BF16)
    sel_r = lax.broadcasted_iota(jnp.int32, (LANES, 2 * LANES), 0)
    sel_c = lax.broadcasted_iota(jnp.int32, (LANES, 2 * LANES), 1)
    sel_src = jnp.where(sel_c < LANES, B_OFF, A_OFF) + 2 * hp + ((sel_c % LANES) // C)
    sel = jnp.where(sel_r == sel_src, 1.0, 0.0).astype(BF16)

    def bd16(m):
        return jnp.concatenate([m * m0, m * m1], axis=0)

    def bd_parts(m):
        hi, lo = _split2(m)
        return bd16(hi), bd16(lo)

    def dot_hi(a, b_hi, b_lo):
        a_hi, a_lo = _split2(a)
        return _dot(jnp.concatenate([a_hi, a_hi, a_lo], axis=1), jnp.concatenate([b_hi, b_lo, b_hi], axis=0))

    def merge_dot(a, b):
        return _dot(a.astype(BF16), bd16(b.astype(BF16)))

    diag_blk = (rC // INV_BASE) == (cC // INV_BASE)
    off_blks = []
    size = INV_BASE
    while size < C:
        off_blks.append(((rC // (2 * size)) == (cC // (2 * size))) & ((rC // size) % 2 == 1) & ((cC // size) % 2 == 0))
        size *= 2

    def tile(t, carry):
        rows = pl.ds(pl.multiple_of(t * R, R), R)
        q = q_ref[rows, :]
        k = k_ref[rows, :]
        v = v_ref[rows, :]
        q = q * lax.rsqrt(_dot((q * q).astype(BF16), ones_bd) + EPS) * (GDN_DK ** -0.5)
        k = k * lax.rsqrt(_dot((k * k).astype(BF16), ones_bd) + EPS)
        bg = _dot_exact_lhs(lat_ref[rows, :], sel)
        beta = bg[:, :LANES]
        gc = bg[:, LANES:]
        eg = jnp.exp(gc)
        kb = k * beta
        vb16 = (v * beta).astype(BF16)
        kbg16 = (kb * eg).astype(BF16)
        qd_ref[rows, :] = (q * eg).astype(BF16)
        k16 = k.astype(BF16)
        q16 = q.astype(BF16)

        ch = lambda arr, g: arr[g * C:(g + 1) * C, :]
        rng = range(G)
        kkqk = [_dot_nt(jnp.concatenate([ch(k16, g), ch(q16, g)], axis=0), bd16(ch(k16, g))) for g in rng]
        g_row = [_dot_exact_rhs(ones_cc, jnp.where(eye2, ch(gc, g), 0.0)) for g in rng]
        dm = [jnp.where(tril2, jnp.exp(jnp.where(tril2, ch(gc, g) - g_row[g], 0.0)), 0.0) for g in rng]
        for g in rng:
            a_ref[pl.ds(pl.multiple_of(t * R + g * C, C), C), :] = (kkqk[g][C:] * dm[g]).astype(BF16)
            g_last = ch(gc, g)[C - 1:C, :]
            kd_ref[pl.ds(pl.multiple_of(t * R + g * C, C), C), :] = (
                ch(k, g) * jnp.exp(g_last - ch(gc, g))).astype(BF16)
            sd_ref[pl.ds(pl.multiple_of((t * G + g) * SUBLANES, SUBLANES), SUBLANES), :] = jnp.broadcast_to(
                jnp.exp(g_last), (SUBLANES, LANES))
        low = [jnp.where(strict2, kkqk[g][:C] * dm[g] * ch(beta, g), 0.0) for g in rng]
        x = [jnp.where(diag_blk, -low[g], 0.0) for g in rng]
        p = [eye2f + x[g] for g in rng]
        y = [dot_hi(x[g], *bd_parts(x[g])) for g in rng]
        rhs = []
        for g in rng:
            p_hi, p_lo = bd_parts(p[g])
            y_hi, y_lo = bd_parts(y[g])
            rhs.append((jnp.concatenate([p_hi, y_hi], axis=1), jnp.concatenate([p_lo, y_lo], axis=1)))
        res = [dot_hi(y[g], *rhs[g]) for g in rng]
        p = [p[g] + res[g][:, :LANES] for g in rng]
        y = [res[g][:, LANES:] for g in rng]
        p = [p[g] + dot_hi(y[g], *bd_parts(p[g])) for g in rng]
        for off_blk in off_blks:
            m1 = [merge_dot(jnp.where(off_blk, low[g], 0.0), p[g]) for g in rng]
            p = [p[g] - merge_dot(p[g], m1[g]) for g in rng]
        uw = [_dot(p[g].astype(BF16), jnp.concatenate([bd16(ch(vb16, g)), bd16(ch(kbg16, g))], axis=1))
              for g in rng]
        for g in rng:
            dst = pl.ds(pl.multiple_of(t * R + g * C, C), C)
            u_ref[dst, :] = uw[g][:, :LANES]
            w_ref[dst, :] = uw[g][:, LANES:].astype(BF16)
        return carry

    lax.fori_loop(0, S // R, tile, 0)


def _gdn_prep(qkv3, lat3):
    B, S, _ = qkv3.shape
    HP = HEAD_PAIRS
    seq_blk = lambda off: pl.BlockSpec((None, S, LANES), lambda b, h, off=off: (b, 0, off + h))
    n_sd = S // GDN_CHUNK * SUBLANES
    return pl.pallas_call(
        _gdn_prep_kernel,
        grid=(B, HP),
        in_specs=[
            seq_blk(0), seq_blk(HP), seq_blk(2 * HP),
            pl.BlockSpec((None, S, LANES), lambda b, h: (b, 0, LAT_W // LANES - 1)),
        ],
        out_specs=[seq_blk(0)] * 5 + [pl.BlockSpec((None, n_sd, LANES), lambda b, h: (b, 0, h))],
        out_shape=[
            jax.ShapeDtypeStruct((B, S, GDN_V_W), F32),
            jax.ShapeDtypeStruct((B, S, GDN_V_W), BF16),
            jax.ShapeDtypeStruct((B, S, GDN_V_W), BF16),
            jax.ShapeDtypeStruct((B, S, GDN_V_W), BF16),
            jax.ShapeDtypeStruct((B, S, GDN_V_W), BF16),
            jax.ShapeDtypeStruct((B, n_sd, GDN_V_W), F32),
        ],
        compiler_params=pltpu.CompilerParams(
            dimension_semantics=("arbitrary", "arbitrary"), vmem_limit_bytes=VMEM_LIMIT),
        name="gdn_prep",
    )(qkv3, qkv3, qkv3, lat3)


def _gdn_scan_kernel(u_ref, w_ref, qd_ref, kd_ref, a_ref, sd_ref, z_ref, nw_ref, o_ref, state_ref):
    C = GDN_CHUNK
    nb, rt, _ = u_ref.shape

    @pl.when(pl.program_id(1) == 0)
    def _():
        state_ref[...] = jnp.zeros_like(state_ref)

    lane = lax.broadcasted_iota(jnp.int32, (1, LANES), 1)
    head1 = lane >= C
    m0 = jnp.where(head1, 0.0, 1.0).astype(BF16)
    m1 = jnp.where(head1, 1.0, 0.0).astype(BF16)
    r128 = lax.broadcasted_iota(jnp.int32, (LANES, LANES), 0)
    c128 = lax.broadcasted_iota(jnp.int32, (LANES, LANES), 1)
    bdmask = (r128 // C) == (c128 // C)
    ones_bd = jnp.where(bdmask, 1.0, 0.0).astype(BF16)

    chains = [(bb, hp) for bb in range(nb) for hp in range(HEAD_PAIRS)]
    blk = lambda hp: slice(hp * LANES, (hp + 1) * LANES)

    def bd16(m):
        return jnp.concatenate([m * m0, m * m1], axis=0)

    def chunk(n, carry):
        rows = pl.ds(pl.multiple_of(n * C, C), C)
        sd_rows = pl.ds(pl.multiple_of(n * SUBLANES, SUBLANES), SUBLANES)
        state = [state_ref[i] for i in range(len(chains))]
        res = [_dot(jnp.concatenate([w_ref[bb, rows, blk(hp)], qd_ref[bb, rows, blk(hp)]], axis=0),
                    state[i].astype(BF16)) for i, (bb, hp) in enumerate(chains)]
        v_new = [(u_ref[bb, rows, blk(hp)] - res[i][:C]).astype(BF16) for i, (bb, hp) in enumerate(chains)]
        upd = [_dot_tn(kd_ref[bb, rows, blk(hp)], v_new[i]) for i, (bb, hp) in enumerate(chains)]
        for i, (bb, hp) in enumerate(chains):
            sd = sd_ref[bb, sd_rows, blk(hp)][0:1, :]
            state_ref[i] = state[i] * sd + jnp.where(bdmask, upd[i], 0.0)
        o = [res[i][C:] + _dot(a_ref[bb, rows, blk(hp)], bd16(v_new[i])) for i, (bb, hp) in enumerate(chains)]
        for i, (bb, hp) in enumerate(chains):
            ms = _dot((o[i] * o[i]).astype(BF16), ones_bd) * (1.0 / GDN_DV)
            y = o[i] * lax.rsqrt(ms + EPS) * nw_ref[:, blk(hp)] * z_ref[bb, rows, blk(hp)].astype(F32)
            o_ref[bb, rows, blk(hp)] = y.astype(o_ref.dtype)
        return carry

    lax.fori_loop(0, rt // C, chunk, 0)


def _gdn_scan(u, w, qd, kd, a, sd, zg, norm_w8):
    B, S, W = u.shape
    rt = min(SCAN_ROWS, S)
    nb = min(SCAN_BATCH, B)
    assert B % nb == 0 and S % rt == 0
    n_sd = rt // GDN_CHUNK * SUBLANES
    seq = pl.BlockSpec((nb, rt, W), lambda b, t: (b, t, 0))
    return pl.pallas_call(
        _gdn_scan_kernel,
        grid=(B // nb, S // rt),
        in_specs=[seq, seq, seq, seq, seq,
                  pl.BlockSpec((nb, n_sd, W), lambda b, t: (b, t, 0)),
                  seq,
                  pl.BlockSpec((1, W), lambda b, t: (0, 0))],
        out_specs=seq,
        out_shape=jax.ShapeDtypeStruct((B, S, W), BF16),
        scratch_shapes=[pltpu.VMEM((nb * HEAD_PAIRS, LANES, LANES), F32)],
        compiler_params=pltpu.CompilerParams(
            dimension_semantics=("arbitrary", "arbitrary"), vmem_limit_bytes=VMEM_LIMIT),
        name="gdn_scan",
    )(u, w, qd, kd, a, sd, zg, norm_w8)


def _mla_prep_kernel(pos_ref, freq_ref, lat_ref, qnw_ref, kvnw_ref, wq_ref, wqs_ref, wk_ref, wv_ref,
                     q_ref, k_ref, v_ref):
    half = MLA_ROPE // 2
    lo, mid, hi = MLA_NOPE, MLA_NOPE + half, MLA_NOPE + MLA_ROPE
    scale = (MLA_NOPE + MLA_ROPE) ** -0.5 * LOG2E
    ang = freq_ref[...] * pos_ref[...].astype(F32)
    f_i = lax.broadcasted_iota(jnp.int32, (3 * half, LANES), 0) % half
    l_i = lax.broadcasted_iota(jnp.int32, (3 * half, LANES), 1)
    in_lo = (l_i >= lo) & (l_i < mid) & (l_i - lo == f_i)
    in_hi = (l_i >= mid) & (l_i < hi) & (l_i - mid == f_i)
    expand_cos = jnp.where(in_lo | in_hi, 1.0, 0.0).astype(BF16)
    expand_sin = jnp.where(in_lo, -1.0, jnp.where(in_hi, 1.0, 0.0)).astype(BF16)
    cos_tab = _dot_tn(jnp.concatenate(_split3(jnp.cos(ang)), axis=0), expand_cos)
    sin_tab = _dot_tn(jnp.concatenate(_split3(jnp.sin(ang)), axis=0), expand_sin)
    lane = lax.broadcasted_iota(jnp.int32, (1, LANES), 1)
    nope = jnp.where(lane < MLA_NOPE, 1.0, 0.0)

    lat = lat_ref[...]
    cq = _rms(lat[:, 0:MLA_Q_RANK], qnw_ref[...]).astype(BF16)
    ckv = _rms(lat[:, MLA_Q_RANK:MLA_Q_RANK + MLA_KV_RANK], kvnw_ref[...]).astype(BF16)
    q = _dot(cq, wq_ref[...])
    q_sw = _dot(cq, wqs_ref[...])
    k_nope = _dot(ckv, wk_ref[...])
    row_all = lax.broadcasted_iota(jnp.int32, (MLA_HEADS * LANES, 1), 0)
    ones_row = jnp.where(row_all % LANES == MLA_V, 1.0, 0.0)
    v_ref[...] = (_dot_nt(wv_ref[...], ckv) + ones_row).astype(v_ref.dtype)

    kpe = pltpu.roll(lat[:, LAT_W - LANES:], MLA_NOPE - KPE_OFF, axis=1)
    kpe_sw = jnp.where(lane < mid, pltpu.roll(kpe, LANES - half, axis=1), pltpu.roll(kpe, half, axis=1))
    k_rope = kpe * cos_tab + kpe_sw * sin_tab
    c_q = (cos_tab + nope) * scale
    s_q = sin_tab * scale
    for h in range(MLA_HEADS):
        blk = slice(h * LANES, (h + 1) * LANES)
        q_ref[:, blk] = (q[:, blk] * c_q + q_sw[:, blk] * s_q).astype(q_ref.dtype)
        k_ref[:, blk] = (k_nope[:, blk] + k_rope).astype(k_ref.dtype)


def _mla_prep(pos_row, inv_freq, lat, q_norm_w, kv_norm_w, wq_p, wqs_p, wk_p, wv_p):
    T = lat.shape[0]
    tm = min(TOK_TILE, T)
    full = lambda a: pl.BlockSpec(a.shape, lambda i: (0,) * a.ndim)
    return pl.pallas_call(
        _mla_prep_kernel,
        grid=(T // tm,),
        in_specs=[
            pl.BlockSpec((1, tm), lambda i: (0, i)),
            full(inv_freq),
            pl.BlockSpec((tm, LAT_W), lambda i: (i, 0)),
            full(q_norm_w), full(kv_norm_w), full(wq_p), full(wqs_p), full(wk_p), full(wv_p),
        ],
        out_specs=[
            pl.BlockSpec((tm, MLA_HEADS * LANES), lambda i: (i, 0)),
            pl.BlockSpec((tm, MLA_HEADS * LANES), lambda i: (i, 0)),
            pl.BlockSpec((MLA_HEADS * LANES, tm), lambda i: (0, i)),
        ],
        out_shape=[
            jax.ShapeDtypeStruct((T, MLA_HEADS * LANES), BF16),
            jax.ShapeDtypeStruct((T, MLA_HEADS * LANES), BF16),
            jax.ShapeDtypeStruct((MLA_HEADS * LANES, T), BF16),
        ],
        compiler_params=pltpu.CompilerParams(
            dimension_semantics=("arbitrary",), vmem_limit_bytes=VMEM_LIMIT),
        name="mla_prep",
    )(pos_row, inv_freq, lat, q_norm_w, kv_norm_w, wq_p, wqs_p, wk_p, wv_p)


def _attn_kernel(q_ref, k_ref, vt_ref, o_ref, s_ref, m_ref, acc_ref, *, tk):
    tq = q_ref.shape[0]
    qi = pl.program_id(2)
    n_full = (qi * tq) // tk
    diag_off = qi * tq - n_full * tk
    lane = lax.broadcasted_iota(jnp.int32, (1, LANES), 1)
    q = q_ref[...]
    key_i = lax.broadcasted_iota(jnp.int32, (tk, tq), 0)
    qry_i = lax.broadcasted_iota(jnp.int32, (tk, tq), 1)
    nh = q_ref.shape[1] // LANES
    blks = [slice(h * LANES, (h + 1) * LANES) for h in range(nh)]

    def scores(j, h):
        kv = pl.ds(pl.multiple_of(j * tk, tk), tk)
        s_ref[h] = _dot_nt(k_ref[kv, blks[h]], q[:, blks[h]])

    def consume(j, h, masked):
        kv = pl.ds(pl.multiple_of(j * tk, tk), tk)
        m_prev = m_ref[h][0:1, :]
        s = s_ref[h]
        if masked:
            s = jnp.where(key_i <= qry_i + diag_off, s, NEG_BIG)
        m_new = jnp.maximum(m_prev, jnp.max(s, axis=0, keepdims=True))
        alpha = jnp.exp2(m_prev - m_new)
        p = jnp.exp2(s - m_new).astype(BF16)
        m_ref[h] = jnp.broadcast_to(m_new, (SUBLANES, tq))
        acc_ref[h] = acc_ref[h] * alpha + _dot(vt_ref[blks[h], kv], p)

    def consume_and_prefetch(j, h, masked):
        consume(j, h, masked)
        nxt = h + ATT_AHEAD
        if nxt < nh:
            scores(j, nxt)
        elif not masked:
            scores(j + 1, nxt - nh)

    def blocks(j0, count):
        for j in range(count):
            for h in range(nh):
                consume_and_prefetch(j0 + j, h, False)

    def unrolled(i, carry):
        blocks(ATT_UNROLL * i, ATT_UNROLL)
        return carry

    m_ref[...] = jnp.full(m_ref.shape, NEG_BIG, F32)
    acc_ref[...] = jnp.zeros(acc_ref.shape, F32)
    for h in range(ATT_AHEAD):
        scores(0, h)
    lax.fori_loop(0, n_full // ATT_UNROLL, unrolled, 0)
    done = n_full - n_full % ATT_UNROLL
    size = ATT_UNROLL // 2
    while size >= 1:
        take = (n_full % (2 * size)) >= size

        @pl.when(take)
        def _(done=done, size=size):
            blocks(done, size)

        done = done + jnp.where(take, size, 0)
        size //= 2
    for h in range(nh):
        consume_and_prefetch(n_full, h, True)
    outs = []
    for h in range(nh):
        acc = acc_ref[h]
        outs.append((acc * (1.0 / acc[MLA_V:MLA_V + 1, :])).T)
    for pr in range(nh // 2):
        pair = jnp.where(lane < MLA_V, outs[2 * pr], pltpu.roll(outs[2 * pr + 1], MLA_V, axis=1))
        o_ref[:, pr * LANES:(pr + 1) * LANES] = pair.astype(o_ref.dtype)


def _attn(q3, k3, vt):
    B, S, _ = q3.shape
    tq = min(ATT_TQ, S)
    tk = min(ATT_TK, S)
    assert tk % tq == 0 and S % tk == 0
    nh = ATT_HEADS
    return pl.pallas_call(
        functools.partial(_attn_kernel, tk=tk),
        grid=(B, MLA_HEADS // nh, S // tq),
        in_specs=[
            pl.BlockSpec((None, tq, nh * LANES), lambda b, h, i: (b, i, h)),
            pl.BlockSpec((None, S, nh * LANES), lambda b, h, i: (b, 0, h)),
            pl.BlockSpec((nh * LANES, S), lambda b, h, i: (h, b)),
        ],
        out_specs=pl.BlockSpec((None, tq, nh * MLA_V), lambda b, h, i: (b, i, h)),
        out_shape=jax.ShapeDtypeStruct((B, S, MLA_W), BF16),
        scratch_shapes=[pltpu.VMEM((nh, tk, tq), F32),
                        pltpu.VMEM((nh, SUBLANES, tq), F32),
                        pltpu.VMEM((nh, LANES, tq), F32)],
        compiler_params=pltpu.CompilerParams(
            dimension_semantics=("arbitrary", "arbitrary", "arbitrary"), vmem_limit_bytes=VMEM_LIMIT),
        name="attn",
    )(q3, k3, vt)


def _tail_kernel(x_ref, yg_ref, om_ref, p_ref, monw_ref, wout_ref, mlpnw_ref, wup_ref, wdown_ref,
                 wpp_ref, postnw_ref, gatenw_ref, wpg_ref, finnw_ref, o_ref):
    ymla = _rms(om_ref[...].astype(F32), monw_ref[...]).astype(BF16)
    h = x_ref[...] + _dot(yg_ref[...], wout_ref[0:GDN_V_W, :]) + _dot(ymla, wout_ref[GDN_V_W:, :])
    u = _rms(h, mlpnw_ref[...]).astype(BF16)
    ff_blk = 1024
    acc = jnp.zeros_like(h)
    for j in range(D_FF // ff_blk):
        cols = slice(j * ff_blk, (j + 1) * ff_blk)
        hid = jnp.maximum(_dot(u, wup_ref[:, cols]), 0.0)
        acc = acc + _dot((hid * hid).astype(BF16), wdown_ref[cols, :])
    h = h + acc
    e = _rms(_dot(p_ref[...].astype(BF16), wpp_ref[...]), postnw_ref[...])
    gate = _sigmoid(_dot(_rms(h, gatenw_ref[...]).astype(BF16), wpg_ref[...]))
    h = h + gate * e
    o_ref[...] = _rms(h, finnw_ref[...])


def _tail(x2, yg, om, p2, mla_out_norm_w, w_out, mlp_norm_w, w_up, w_down, w_ple_proj,
          ple_post_norm_w, ple_gate_norm_w, w_ple_gate, final_norm_w):
    T = x2.shape[0]
    tm = min(TOK_TILE, T)
    tok = lambda w: pl.BlockSpec((tm, w), lambda i: (i, 0))
    const = lambda a: pl.BlockSpec(a.shape, lambda i: (0, 0), pipeline_mode=pl.Buffered(1))
    return pl.pallas_call(
        _tail_kernel,
        grid=(T // tm,),
        in_specs=[
            tok(D_MODEL), tok(GDN_V_W), tok(MLA_W), tok(PLE_DIM),
            const(mla_out_norm_w), const(w_out), const(mlp_norm_w), const(w_up), const(w_down),
            const(w_ple_proj), const(ple_post_norm_w), const(ple_gate_norm_w), const(w_ple_gate),
            const(final_norm_w),
        ],
        out_specs=tok(D_MODEL),
        out_shape=jax.ShapeDtypeStruct((T, D_MODEL), F32),
        compiler_params=pltpu.CompilerParams(
            dimension_semantics=("arbitrary",), vmem_limit_bytes=VMEM_LIMIT),
        name="tail",
    )(x2, yg, om, p2, mla_out_norm_w, w_out, mlp_norm_w, w_up, w_down, w_ple_proj,
      ple_post_norm_w, ple_gate_norm_w, w_ple_gate, final_norm_w)


def _pack_w_in(w):
    o_b = QKV_W + GDN_V_W
    o_cq = o_b + 2 * GDN_HEADS
    n_lat = MLA_Q_RANK + MLA_KV_RANK + MLA_ROPE
    pad = jnp.zeros((D_MODEL, LAT_W - n_lat - 2 * GDN_HEADS), w.dtype)
    return jnp.concatenate([w[:, :o_b], w[:, o_cq:o_cq + n_lat], w[:, o_b:o_cq], pad], axis=1).astype(BF16)


def _gate_lane_vector(per_head):
    return jnp.zeros((1, LANES), F32).at[0, A_OFF:A_OFF + GDN_HEADS].set(per_head.astype(F32))


def _pack_mla_weights(w_q_b, w_kv_b):
    wq = w_q_b.reshape(MLA_Q_RANK, MLA_HEADS, MLA_NOPE + MLA_ROPE)
    wq = jnp.pad(wq, ((0, 0), (0, 0), (0, LANES - MLA_NOPE - MLA_ROPE)))
    wkv = w_kv_b.reshape(MLA_KV_RANK, MLA_HEADS, MLA_NOPE + MLA_V)
    wk = jnp.pad(wkv[:, :, :MLA_NOPE], ((0, 0), (0, 0), (0, LANES - MLA_NOPE)))
    wv = jnp.pad(wkv[:, :, MLA_NOPE:], ((0, 0), (0, 0), (0, LANES - MLA_V)))
    half = MLA_ROPE // 2
    zeros = lambda n: jnp.zeros((MLA_Q_RANK, MLA_HEADS, n), wq.dtype)
    wq_sw = jnp.concatenate([zeros(MLA_NOPE), wq[:, :, MLA_NOPE + half:MLA_NOPE + MLA_ROPE],
                             wq[:, :, MLA_NOPE:MLA_NOPE + half], zeros(LANES - MLA_NOPE - MLA_ROPE)], axis=2)
    return (wq.reshape(MLA_Q_RANK, MLA_HEADS * LANES).astype(BF16),
            wq_sw.reshape(MLA_Q_RANK, MLA_HEADS * LANES).astype(BF16),
            wk.reshape(MLA_KV_RANK, MLA_HEADS * LANES).astype(BF16),
            wv.reshape(MLA_KV_RANK, MLA_HEADS * LANES).T.astype(BF16))


def kernel(x, p, positions, mix_norm_w, w_in, conv_w, A_log, dt_bias, gdn_norm_w, q_norm_w, w_q_b,
           kv_norm_w, w_kv_b, mla_out_norm_w, w_out, mlp_norm_w, w_up, w_down, w_ple_proj,
           ple_post_norm_w, ple_gate_norm_w, w_ple_gate, final_norm_w):
    B, S, _ = x.shape
    T = B * S
    assert w_in.shape[0] == 1, "one layer"
    row = lambda a: a.reshape(1, -1).astype(F32)
    x2 = x.reshape(T, D_MODEL)

    qkv, zg, lat = _inproj(x2, row(mix_norm_w[0]), _pack_w_in(w_in[0]), conv_w[0].astype(F32),
                           _gate_lane_vector(A_log[0]), _gate_lane_vector(dt_bias[0]), S)

    factors = _gdn_prep(qkv.reshape(B, S, QKV_W), lat.reshape(B, S, LAT_W))
    y_gdn = _gdn_scan(*factors, zg.reshape(B, S, GDN_V_W), row(jnp.tile(gdn_norm_w[0], GDN_HEADS)))

    half = MLA_ROPE // 2
    inv_freq = (ROPE_THETA ** (-jnp.arange(0, MLA_ROPE, 2, dtype=F32) / MLA_ROPE)).reshape(half, 1)
    q_att, k_att, v_att = _mla_prep(positions.reshape(1, T), inv_freq, lat, row(q_norm_w[0]),
                                    row(kv_norm_w[0]), *_pack_mla_weights(w_q_b[0], w_kv_b[0]))
    o_mla = _attn(q_att.reshape(B, S, -1), k_att.reshape(B, S, -1), v_att)

    out = _tail(x2, y_gdn.reshape(T, GDN_V_W), o_mla.reshape(T, MLA_W), p[0].reshape(T, PLE_DIM),
                row(mla_out_norm_w[0]), w_out[0].astype(BF16), row(mlp_norm_w[0]),
                w_up[0].astype(BF16), w_down[0].astype(BF16), w_ple_proj[0].astype(BF16),
                row(ple_post_norm_w[0]), row(ple_gate_norm_w[0]), w_ple_gate[0].astype(BF16),
                row(final_norm_w))
    return out.reshape(B, S, D_MODEL)
```

```python
import functools

import jax
import jax.numpy as jnp
from jax import lax
from jax.experimental import pallas as pl
from jax.experimental.pallas import tpu as pltpu

F32 = jnp.float32
BF16 = jnp.bfloat16

D_MODEL = 1024
PLE_DIM = 256
GDN_HEADS = 8
GDN_DK = 64
GDN_DV = 64
GDN_QK_W = GDN_HEADS * GDN_DK
GDN_V_W = GDN_HEADS * GDN_DV
GDN_CONV = 4
GDN_CHUNK = 64
MLA_HEADS = 8
MLA_NOPE = 64
MLA_ROPE = 32
MLA_V = 64
MLA_W = MLA_HEADS * MLA_V
MLA_Q_RANK = 256
MLA_KV_RANK = 128
ROPE_THETA = 10000.0
D_FF = 4 * D_MODEL
EPS = 1e-6

LANES = 128
SUBLANES = 8
HEAD_PAIRS = GDN_HEADS // 2
QKV_W = 2 * GDN_QK_W + GDN_V_W
LAT_W = 512
IN_W_PAD = QKV_W + GDN_V_W + LAT_W
KPE_OFF, B_OFF, A_OFF = 0, MLA_ROPE, MLA_ROPE + GDN_HEADS
VMEM_LIMIT = 56 * 1024 * 1024

TOK_TILE = 512
CUM_ROWS = 256
CONV_COLS = 512
GDN_ROWS = 1024
INV_BASE = 8
SCAN_ROWS = 512
SCAN_BATCH = 4
ATT_BLOCK = 512
ATT_HEADS = 4
ATT_UNROLL = 4
ATT_AHEAD = 2
NEG_BIG = -1e30
LOG2E = 1.4426950408889634


def _dot(a, b):
    return jnp.dot(a, b, preferred_element_type=F32)


def _dot_nt(a, b):
    return lax.dot_general(a, b, (((1,), (1,)), ((), ())), preferred_element_type=F32)


def _dot_tn(a, b):
    return lax.dot_general(a, b, (((0,), (0,)), ((), ())), preferred_element_type=F32)


def _split2(x):
    hi = x.astype(BF16)
    lo = (x - hi.astype(F32)).astype(BF16)
    return hi, lo


def _split3(x):
    hi = x.astype(BF16)
    r1 = x - hi.astype(F32)
    mid = r1.astype(BF16)
    lo = (r1 - mid.astype(F32)).astype(BF16)
    return hi, mid, lo


def _dot_exact_rhs(a_bf16, x):
    return _dot(jnp.concatenate([a_bf16] * 3, axis=1), jnp.concatenate(_split3(x), axis=0))


def _dot_exact_lhs(x, b_bf16):
    return _dot(jnp.concatenate(_split3(x), axis=1), jnp.concatenate([b_bf16] * 3, axis=0))


def _rms(x, w):
    ms = jnp.mean(x * x, axis=-1, keepdims=True)
    return x * lax.rsqrt(ms + EPS) * w


def _sigmoid(x):
    return 1.0 / (1.0 + jnp.exp(-x))


def _silu(x):
    h = 0.5 * x
    return h + h * jnp.tanh(h)


def _softplus(x):
    return jnp.maximum(x, 0.0) + jnp.log(1.0 + jnp.exp(-jnp.abs(x)))


def _chunk_tril(n):
    r = lax.broadcasted_iota(jnp.int32, (n, n), 0)
    c = lax.broadcasted_iota(jnp.int32, (n, n), 1)
    return jnp.where(((r // GDN_CHUNK) == (c // GDN_CHUNK)) & (c <= r), 1.0, 0.0).astype(BF16)


def _inproj_kernel(x_ref, nw_ref, w_ref, cw_ref, alog_ref, dtb_ref, qkv_ref, z_ref, lat_ref, halo_ref,
                   *, tiles_per_seq):
    tm = x_ref.shape[0]
    first = (pl.program_id(0) % tiles_per_seq) == 0
    u = _rms(x_ref[...], nw_ref[...]).astype(BF16)

    def conv_chunk(c):
        cols = slice(c * CONV_COLS, (c + 1) * CONV_COLS)
        raw = _dot(u, w_ref[:, cols])
        prev = jnp.where(first, 0.0, halo_ref[:, cols])
        halo_ref[:, cols] = raw[tm - SUBLANES:, :]
        ext = jnp.concatenate([prev, raw], axis=0)
        cw = cw_ref[:, cols]
        acc = raw * cw[GDN_CONV - 1:GDN_CONV, :]
        for kk in range(GDN_CONV - 1):
            shifted = pltpu.roll(ext, GDN_CONV - 1 - kk, axis=0)[SUBLANES:, :]
            acc = acc + shifted * cw[kk:kk + 1, :]
        qkv_ref[:, cols] = _silu(acc)

    def gate_z():
        zf = _dot(u, w_ref[:, QKV_W:QKV_W + GDN_V_W])
        z_ref[...] = _silu(zf).astype(z_ref.dtype)

    def latent_head():
        lat_ref[:, :LAT_W - LANES] = _dot(u, w_ref[:, QKV_W + GDN_V_W:IN_W_PAD - LANES])

    def latent_gates():
        l3 = _dot(u, w_ref[:, IN_W_PAD - LANES:IN_W_PAD])
        lane = lax.broadcasted_iota(jnp.int32, (1, LANES), 1)
        is_b = (lane >= B_OFF) & (lane < B_OFF + GDN_HEADS)
        is_a = (lane >= A_OFF) & (lane < A_OFF + GDN_HEADS)
        g_raw = jnp.where(is_a, -(jnp.exp(alog_ref[...]) * _softplus(l3 + dtb_ref[...])), 0.0)
        hi, mid, lo = (t.astype(F32) for t in _split3(g_raw))
        packed = (hi + pltpu.roll(mid, GDN_HEADS, axis=1) + pltpu.roll(lo, 2 * GDN_HEADS, axis=1)).astype(BF16)
        tril = _chunk_tril(CUM_ROWS)
        cum = jnp.concatenate([_dot(tril, packed[r:r + CUM_ROWS, :]) for r in range(0, tm, CUM_ROWS)], axis=0)
        g_cum = cum + pltpu.roll(cum, LANES - GDN_HEADS, axis=1) + pltpu.roll(cum, LANES - 2 * GDN_HEADS, axis=1)
        lat_ref[:, LAT_W - LANES:] = jnp.where(is_b, _sigmoid(l3), jnp.where(is_a, g_cum, l3))

    fillers = [latent_gates, gate_z, latent_head]
    for c in range(QKV_W // CONV_COLS):
        if fillers:
            fillers.pop(0)()
        conv_chunk(c)
    for f in fillers:
        f()


def _inproj(x2, mix_norm_w, w_in_p, conv_w, alog_vec, dtb_vec, seq_len):
    T = x2.shape[0]
    tm = min(TOK_TILE, seq_len)
    assert seq_len % tm == 0 and tm % GDN_CHUNK == 0
    nt = T // tm
    const = lambda a: pl.BlockSpec(a.shape, lambda i: (0, 0))
    return pl.pallas_call(
        functools.partial(_inproj_kernel, tiles_per_seq=seq_len // tm),
        grid=(nt,),
        in_specs=[
            pl.BlockSpec((tm, D_MODEL), lambda i: (i, 0)),
            const(mix_norm_w), const(w_in_p), const(conv_w), const(alog_vec), const(dtb_vec),
        ],
        out_specs=[
            pl.BlockSpec((tm, QKV_W), lambda i: (i, 0)),
            pl.BlockSpec((tm, GDN_V_W), lambda i: (i, 0)),
            pl.BlockSpec((tm, LAT_W), lambda i: (i, 0)),
        ],
        out_shape=[
            jax.ShapeDtypeStruct((T, QKV_W), F32),
            jax.ShapeDtypeStruct((T, GDN_V_W), BF16),
            jax.ShapeDtypeStruct((T, LAT_W), F32),
        ],
        scratch_shapes=[pltpu.VMEM((SUBLANES, QKV_W), F32)],
        compiler_params=pltpu.CompilerParams(
            dimension_semantics=("arbitrary",), vmem_limit_bytes=VMEM_LIMIT),
        name="inproj",
    )(x2, mix_norm_w, w_in_p, conv_w, alog_vec, dtb_vec)


def _gdn_prep_kernel(q_ref, k_ref, v_ref, lat_ref, u_ref, w_ref, qd_ref, kd_ref, a_ref, sd_ref):
    C = GDN_CHUNK
    S = q_ref.shape[0]
    R = min(GDN_ROWS, S)
    G = R // C
    hp = pl.program_id(1)

    lane = lax.broadcasted_iota(jnp.int32, (1, LANES), 1)
    head1 = lane >= C
    m0 = jnp.where(head1, 0.0, 1.0).astype(BF16)
    m1 = jnp.where(head1, 1.0, 0.0).astype(BF16)
    r128 = lax.broadcasted_iota(jnp.int32, (LANES, LANES), 0)
    c128 = lax.broadcasted_iota(jnp.int32, (LANES, LANES), 1)
    ones_bd = jnp.where((r128 // C) == (c128 // C), 1.0, 0.0).astype(BF16)
    rC = lax.broadcasted_iota(jnp.int32, (C, LANES), 0)
    cC = lax.broadcasted_iota(jnp.int32, (C, LANES), 1) % C
    eye2 = rC == cC
    tril2 = rC >= cC
    strict2 = rC > cC
    eye2f = jnp.where(eye2, 1.0, 0.0)
    ones_cc = jnp.ones((C, C), BF16)
    sel_r = lax.broadcasted_iota(jnp.int32, (LANES, 2 * LANES), 0)
    sel_c = lax.broadcasted_iota(jnp.int32, (LANES, 2 * LANES), 1)
    sel_src = jnp.where(sel_c < LANES, B_OFF, A_OFF) + 2 * hp + ((sel_c % LANES) // C)
    sel = jnp.where(sel_r == sel_src, 1.0, 0.0).astype(BF16)

    def bd16(m):
        return jnp.concatenate([m * m0, m * m1], axis=0)

    def bd_parts(m):
        hi, lo = _split2(m)
        return bd16(hi), bd16(lo)

    def dot_hi(a, b_hi, b_lo):
        a_hi, a_lo = _split2(a)
        return _dot(jnp.concatenate([a_hi, a_hi, a_lo], axis=1), jnp.concatenate([b_hi, b_lo, b_hi], axis=0))

    def merge_dot(a, b):
        return _dot(a.astype(BF16), bd16(b.astype(BF16)))

    diag_blk = (rC // INV_BASE) == (cC // INV_BASE)
    off_blks = []
    size = INV_BASE
    while size < C:
        off_blks.append(((rC // (2 * size)) == (cC // (2 * size))) & ((rC // size) % 2 == 1) & ((cC // size) % 2 == 0))
        size *= 2

    def tile(t, carry):
        rows = pl.ds(pl.multiple_of(t * R, R), R)
        q = q_ref[rows, :]
        k = k_ref[rows, :]
        v = v_ref[rows, :]
        q = q * lax.rsqrt(_dot((q * q).astype(BF16), ones_bd) + EPS) * (GDN_DK ** -0.5)
        k = k * lax.rsqrt(_dot((k * k).astype(BF16), ones_bd) + EPS)
        bg = _dot_exact_lhs(lat_ref[rows, :], sel)
        beta = bg[:, :LANES]
        gc = bg[:, LANES:]
        eg = jnp.exp(gc)
        kb = k * beta
        vb16 = (v * beta).astype(BF16)
        kbg16 = (kb * eg).astype(BF16)
        qd_ref[rows, :] = (q * eg).astype(BF16)
        k16 = k.astype(BF16)
        q16 = q.astype(BF16)

        ch = lambda arr, g: arr[g * C:(g + 1) * C, :]
        rng = range(G)
        kkqk = [_dot_nt(jnp.concatenate([ch(k16, g), ch(q16, g)], axis=0), bd16(ch(k16, g))) for g in rng]
        g_row = [_dot_exact_rhs(ones_cc, jnp.where(eye2, ch(gc, g), 0.0)) for g in rng]
        dm = [jnp.where(tril2, jnp.exp(jnp.where(tril2, ch(gc, g) - g_row[g], 0.0)), 0.0) for g in rng]
        for g in rng:
            a_ref[pl.ds(pl.multiple_of(t * R + g * C, C), C), :] = (kkqk[g][C:] * dm[g]).astype(BF16)
            g_last = ch(gc, g)[C - 1:C, :]
            kd_ref[pl.ds(pl.multiple_of(t * R + g * C, C), C), :] = (
                ch(k, g) * jnp.exp(g_last - ch(gc, g))).astype(BF16)
            sd_ref[pl.ds(pl.multiple_of((t * G + g) * SUBLANES, SUBLANES), SUBLANES), :] = jnp.broadcast_to(
                jnp.exp(g_last), (SUBLANES, LANES))
        low = [jnp.where(strict2, kkqk[g][:C] * dm[g] * ch(beta, g), 0.0) for g in rng]
        x = [jnp.where(diag_blk, -low[g], 0.0) for g in rng]
        p = [eye2f + x[g] for g in rng]
        y = [dot_hi(x[g], *bd_parts(x[g])) for g in rng]
        rhs = []
        for g in rng:
            p_hi, p_lo = bd_parts(p[g])
            y_hi, y_lo = bd_parts(y[g])
            rhs.append((jnp.concatenate([p_hi, y_hi], axis=1), jnp.concatenate([p_lo, y_lo], axis=1)))
        res = [dot_hi(y[g], *rhs[g]) for g in rng]
        p = [p[g] + res[g][:, :LANES] for g in rng]
        y = [res[g][:, LANES:] for g in rng]
        p = [p[g] + dot_hi(y[g], *bd_parts(p[g])) for g in rng]
        for off_blk in off_blks:
            m1 = [merge_dot(jnp.where(off_blk, low[g], 0.0), p[g]) for g in rng]
            p = [p[g] - merge_dot(p[g], m1[g]) for g in rng]
        uw = [_dot(p[g].astype(BF16), jnp.concatenate([bd16(ch(vb16, g)), bd16(ch(kbg16, g))], axis=1))
              for g in rng]
        for g in rng:
            dst = pl.ds(pl.multiple_of(t * R + g * C, C), C)
            u_ref[dst, :] = uw[g][:, :LANES]
            w_ref[dst, :] = uw[g][:, LANES:].astype(BF16)
        return carry

    lax.fori_loop(0, S // R, tile, 0)


def _gdn_prep(qkv3, lat3):
    B, S, _ = qkv3.shape
    HP = HEAD_PAIRS
    seq_blk = lambda off: pl.BlockSpec((None, S, LANES), lambda b, h, off=off: (b, 0, off + h))
    n_sd = S // GDN_CHUNK * SUBLANES
    return pl.pallas_call(
        _gdn_prep_kernel,
        grid=(B, HP),
        in_specs=[
            seq_blk(0), seq_blk(HP), seq_blk(2 * HP),
            pl.BlockSpec((None, S, LANES), lambda b, h: (b, 0, LAT_W // LANES - 1)),
        ],
        out_specs=[seq_blk(0)] * 5 + [pl.BlockSpec((None, n_sd, LANES), lambda b, h: (b, 0, h))],
        out_shape=[
            jax.ShapeDtypeStruct((B, S, GDN_V_W), F32),
            jax.ShapeDtypeStruct((B, S, GDN_V_W), BF16),
            jax.ShapeDtypeStruct((B, S, GDN_V_W), BF16),
            jax.ShapeDtypeStruct((B, S, GDN_V_W), BF16),
            jax.ShapeDtypeStruct((B, S, GDN_V_W), BF16),
            jax.ShapeDtypeStruct((B, n_sd, GDN_V_W), F32),
        ],
        compiler_params=pltpu.CompilerParams(
            dimension_semantics=("arbitrary", "arbitrary"), vmem_limit_bytes=VMEM_LIMIT),
        name="gdn_prep",
    )(qkv3, qkv3, qkv3, lat3)


def _gdn_scan_kernel(u_ref, w_ref, qd_ref, kd_ref, a_ref, sd_ref, z_ref, nw_ref, o_ref, state_ref):
    C = GDN_CHUNK
    nb, rt, _ = u_ref.shape

    @pl.when(pl.program_id(1) == 0)
    def _():
        state_ref[...] = jnp.zeros_like(state_ref)

    lane = lax.broadcasted_iota(jnp.int32, (1, LANES), 1)
    head1 = lane >= C
    m0 = jnp.where(head1, 0.0, 1.0).astype(BF16)
    m1 = jnp.where(head1, 1.0, 0.0).astype(BF16)
    r128 = lax.broadcasted_iota(jnp.int32, (LANES, LANES), 0)
    c128 = lax.broadcasted_iota(jnp.int32, (LANES, LANES), 1)
    bdmask = (r128 // C) == (c128 // C)
    ones_bd = jnp.where(bdmask, 1.0, 0.0).astype(BF16)

    chains = [(bb, hp) for bb in range(nb) for hp in range(HEAD_PAIRS)]
    blk = lambda hp: slice(hp * LANES, (hp + 1) * LANES)

    def bd16(m):
        return jnp.concatenate([m * m0, m * m1], axis=0)

    def chunk(n, carry):
        rows = pl.ds(pl.multiple_of(n * C, C), C)
        sd_rows = pl.ds(pl.multiple_of(n * SUBLANES, SUBLANES), SUBLANES)
        state = [state_ref[i] for i in range(len(chains))]
        res = [_dot(jnp.concatenate([w_ref[bb, rows, blk(hp)], qd_ref[bb, rows, blk(hp)]], axis=0),
                    state[i].astype(BF16)) for i, (bb, hp) in enumerate(chains)]
        v_new = [(u_ref[bb, rows, blk(hp)] - res[i][:C]).astype(BF16) for i, (bb, hp) in enumerate(chains)]
        upd = [_dot_tn(kd_ref[bb, rows, blk(hp)], v_new[i]) for i, (bb, hp) in enumerate(chains)]
        for i, (bb, hp) in enumerate(chains):
            sd = sd_ref[bb, sd_rows, blk(hp)][0:1, :]
            state_ref[i] = state[i] * sd + jnp.where(bdmask, upd[i], 0.0)
        o = [res[i][C:] + _dot(a_ref[bb, rows, blk(hp)], bd16(v_new[i])) for i, (bb, hp) in enumerate(chains)]
        for i, (bb, hp) in enumerate(chains):
            ms = _dot((o[i] * o[i]).astype(BF16), ones_bd) * (1.0 / GDN_DV)
            y = o[i] * lax.rsqrt(ms + EPS) * nw_ref[:, blk(hp)] * z_ref[bb, rows, blk(hp)].astype(F32)
            o_ref[bb, rows, blk(hp)] = y.astype(o_ref.dtype)
        return carry

    lax.fori_loop(0, rt // C, chunk, 0)


def _gdn_scan(u, w, qd, kd, a, sd, zg, norm_w8):
    B, S, W = u.shape
    rt = min(SCAN_ROWS, S)
    nb = min(SCAN_BATCH, B)
    assert B % nb == 0 and S % rt == 0
    n_sd = rt // GDN_CHUNK * SUBLANES
    seq = pl.BlockSpec((nb, rt, W), lambda b, t: (b, t, 0))
    return pl.pallas_call(
        _gdn_scan_kernel,
        grid=(B // nb, S // rt),
        in_specs=[seq, seq, seq, seq, seq,
                  pl.BlockSpec((nb, n_sd, W), lambda b, t: (b, t, 0)),
                  seq,
                  pl.BlockSpec((1, W), lambda b, t: (0, 0))],
        out_specs=seq,
        out_shape=jax.ShapeDtypeStruct((B, S, W), BF16),
        scratch_shapes=[pltpu.VMEM((nb * HEAD_PAIRS, LANES, LANES), F32)],
        compiler_params=pltpu.CompilerParams(
            dimension_semantics=("arbitrary", "arbitrary"), vmem_limit_bytes=VMEM_LIMIT),
        name="gdn_scan",
    )(u, w, qd, kd, a, sd, zg, norm_w8)


def _mla_prep_kernel(pos_ref, freq_ref, lat_ref, qnw_ref, kvnw_ref, wq_ref, wqs_ref, wk_ref, wv_ref,
                     q_ref, k_ref, v_ref):
    half = MLA_ROPE // 2
    lo, mid, hi = MLA_NOPE, MLA_NOPE + half, MLA_NOPE + MLA_ROPE
    scale = (MLA_NOPE + MLA_ROPE) ** -0.5 * LOG2E
    ang = freq_ref[...] * pos_ref[...].astype(F32)
    f_i = lax.broadcasted_iota(jnp.int32, (3 * half, LANES), 0) % half
    l_i = lax.broadcasted_iota(jnp.int32, (3 * half, LANES), 1)
    in_lo = (l_i >= lo) & (l_i < mid) & (l_i - lo == f_i)
    in_hi = (l_i >= mid) & (l_i < hi) & (l_i - mid == f_i)
    expand_cos = jnp.where(in_lo | in_hi, 1.0, 0.0).astype(BF16)
    expand_sin = jnp.where(in_lo, -1.0, jnp.where(in_hi, 1.0, 0.0)).astype(BF16)
    cos_tab = _dot_tn(jnp.concatenate(_split3(jnp.cos(ang)), axis=0), expand_cos)
    sin_tab = _dot_tn(jnp.concatenate(_split3(jnp.sin(ang)), axis=0), expand_sin)
    lane = lax.broadcasted_iota(jnp.int32, (1, LANES), 1)
    nope = jnp.where(lane < MLA_NOPE, 1.0, 0.0)

    lat = lat_ref[...]
    cq = _rms(lat[:, 0:MLA_Q_RANK], qnw_ref[...]).astype(BF16)
    ckv = _rms(lat[:, MLA_Q_RANK:MLA_Q_RANK + MLA_KV_RANK], kvnw_ref[...]).astype(BF16)
    q = _dot(cq, wq_ref[...])
    q_sw = _dot(cq, wqs_ref[...])
    k_nope = _dot(ckv, wk_ref[...])
    row_all = lax.broadcasted_iota(jnp.int32, (MLA_HEADS * LANES, 1), 0)
    ones_row = jnp.where(row_all % LANES == MLA_V, 1.0, 0.0)
    v_ref[...] = (_dot_nt(wv_ref[...], ckv) + ones_row).astype(v_ref.dtype)

    kpe = pltpu.roll(lat[:, LAT_W - LANES:], MLA_NOPE - KPE_OFF, axis=1)
    kpe_sw = jnp.where(lane < mid, pltpu.roll(kpe, LANES - half, axis=1), pltpu.roll(kpe, half, axis=1))
    k_rope = kpe * cos_tab + kpe_sw * sin_tab
    c_q = (cos_tab + nope) * scale
    s_q = sin_tab * scale
    for h in range(MLA_HEADS):
        blk = slice(h * LANES, (h + 1) * LANES)
        q_ref[:, blk] = (q[:, blk] * c_q + q_sw[:, blk] * s_q).astype(q_ref.dtype)
        k_ref[:, blk] = (k_nope[:, blk] + k_rope).astype(k_ref.dtype)


def _mla_prep(pos_row, inv_freq, lat, q_norm_w, kv_norm_w, wq_p, wqs_p, wk_p, wv_p):
    T = lat.shape[0]
    tm = min(TOK_TILE, T)
    full = lambda a: pl.BlockSpec(a.shape, lambda i: (0,) * a.ndim)
    return pl.pallas_call(
        _mla_prep_kernel,
        grid=(T // tm,),
        in_specs=[
            pl.BlockSpec((1, tm), lambda i: (0, i)),
            full(inv_freq),
            pl.BlockSpec((tm, LAT_W), lambda i: (i, 0)),
            full(q_norm_w), full(kv_norm_w), full(wq_p), full(wqs_p), full(wk_p), full(wv_p),
        ],
        out_specs=[
            pl.BlockSpec((tm, MLA_HEADS * LANES), lambda i: (i, 0)),
            pl.BlockSpec((tm, MLA_HEADS * LANES), lambda i: (i, 0)),
            pl.BlockSpec((MLA_HEADS * LANES, tm), lambda i: (0, i)),
        ],
        out_shape=[
            jax.ShapeDtypeStruct((T, MLA_HEADS * LANES), BF16),
            jax.ShapeDtypeStruct((T, MLA_HEADS * LANES), BF16),
            jax.ShapeDtypeStruct((MLA_HEADS * LANES, T), BF16),
        ],
        compiler_params=pltpu.CompilerParams(
            dimension_semantics=("arbitrary",), vmem_limit_bytes=VMEM_LIMIT),
        name="mla_prep",
    )(pos_row, inv_freq, lat, q_norm_w, kv_norm_w, wq_p, wqs_p, wk_p, wv_p)


def _attn_kernel(q_ref, k_ref, vt_ref, o_ref, s_ref, m_ref, acc_ref, *, tq):
    seq = q_ref.shape[0]
    nq = seq // tq
    nh = q_ref.shape[1] // LANES
    lane = lax.broadcasted_iota(jnp.int32, (1, LANES), 1)
    causal = lax.broadcasted_iota(jnp.int32, (tq, tq), 0) <= lax.broadcasted_iota(jnp.int32, (tq, tq), 1)
    blks = [slice(h * LANES, (h + 1) * LANES) for h in range(nh)]
    rows = lambda i: pl.ds(pl.multiple_of(i * tq, tq), tq)

    def scores(qi, j, h):
        s_ref[h] = _dot_nt(k_ref[rows(j), blks[h]], q_ref[rows(qi), blks[h]])

    def consume(j, h, masked):
        m_prev = m_ref[h][0:1, :]
        s = s_ref[h]
        if masked:
            s = jnp.where(causal, s, NEG_BIG)
        m_new = jnp.maximum(m_prev, jnp.max(s, axis=0, keepdims=True))
        alpha = jnp.exp2(m_prev - m_new)
        p = jnp.exp2(s - m_new).astype(BF16)
        m_ref[h] = jnp.broadcast_to(m_new, (SUBLANES, tq))
        acc_ref[h] = acc_ref[h] * alpha + _dot(vt_ref[blks[h], rows(j)], p)

    def consume_and_prefetch(qi, j, h, masked):
        consume(j, h, masked)
        nxt = h + ATT_AHEAD
        if nxt < nh:
            scores(qi, j, nxt)
        elif not masked:
            scores(qi, j + 1, nxt - nh)
        else:
            scores(jnp.minimum(qi + 1, nq - 1), 0, nxt - nh)

    def q_block(qi, carry):
        def blocks(j0, count):
            for j in range(count):
                for h in range(nh):
                    consume_and_prefetch(qi, j0 + j, h, False)

        def unrolled(i, c):
            blocks(ATT_UNROLL * i, ATT_UNROLL)
            return c

        m_ref[...] = jnp.full(m_ref.shape, NEG_BIG, F32)
        acc_ref[...] = jnp.zeros(acc_ref.shape, F32)
        n_full = qi
        lax.fori_loop(0, n_full // ATT_UNROLL, unrolled, 0)
        done = n_full - n_full % ATT_UNROLL
        size = ATT_UNROLL // 2
        while size >= 1:
            take = (n_full % (2 * size)) >= size

            @pl.when(take)
            def _(done=done, size=size):
                blocks(done, size)

            done = done + jnp.where(take, size, 0)
            size //= 2
        for h in range(nh):
            consume_and_prefetch(qi, n_full, h, True)
        outs = []
        for h in range(nh):
            acc = acc_ref[h]
            outs.append((acc * (1.0 / acc[MLA_V:MLA_V + 1, :])).T)
        for pr in range(nh // 2):
            pair = jnp.where(lane < MLA_V, outs[2 * pr], pltpu.roll(outs[2 * pr + 1], MLA_V, axis=1))
            o_ref[rows(qi), pr * LANES:(pr + 1) * LANES] = pair.astype(o_ref.dtype)
        return carry

    for h in range(ATT_AHEAD):
        scores(0, 0, h)
    lax.fori_loop(0, nq, q_block, 0)


def _attn(q3, k3, vt):
    B, S, _ = q3.shape
    tq = min(ATT_BLOCK, S)
    assert S % tq == 0
    nh = ATT_HEADS
    return pl.pallas_call(
        functools.partial(_attn_kernel, tq=tq),
        grid=(B, MLA_HEADS // nh),
        in_specs=[
            pl.BlockSpec((None, S, nh * LANES), lambda b, h: (b, 0, h)),
            pl.BlockSpec((None, S, nh * LANES), lambda b, h: (b, 0, h)),
            pl.BlockSpec((nh * LANES, S), lambda b, h: (h, b)),
        ],
        out_specs=pl.BlockSpec((None, S, nh * MLA_V), lambda b, h: (b, 0, h)),
        out_shape=jax.ShapeDtypeStruct((B, S, MLA_W), BF16),
        scratch_shapes=[pltpu.VMEM((nh, tq, tq), F32),
                        pltpu.VMEM((nh, SUBLANES, tq), F32),
                        pltpu.VMEM((nh, LANES, tq), F32)],
        compiler_params=pltpu.CompilerParams(
            dimension_semantics=("arbitrary", "arbitrary"), vmem_limit_bytes=VMEM_LIMIT),
        name="attn",
    )(q3, k3, vt)


def _tail_kernel(x_ref, yg_ref, om_ref, p_ref, monw_ref, wout_ref, mlpnw_ref, wup_ref, wdown_ref,
                 wpp_ref, postnw_ref, gatenw_ref, wpg_ref, finnw_ref, o_ref):
    ymla = _rms(om_ref[...].astype(F32), monw_ref[...]).astype(BF16)
    h = x_ref[...] + _dot(yg_ref[...], wout_ref[0:GDN_V_W, :]) + _dot(ymla, wout_ref[GDN_V_W:, :])
    u = _rms(h, mlpnw_ref[...]).astype(BF16)
    ff_blk = 1024
    acc = jnp.zeros_like(h)
    for j in range(D_FF // ff_blk):
        cols = slice(j * ff_blk, (j + 1) * ff_blk)
        hid = jnp.maximum(_dot(u, wup_ref[:, cols]), 0.0)
        acc = acc + _dot((hid * hid).astype(BF16), wdown_ref[cols, :])
    h = h + acc
    e = _rms(_dot(p_ref[...].astype(BF16), wpp_ref[...]), postnw_ref[...])
    gate = _sigmoid(_dot(_rms(h, gatenw_ref[...]).astype(BF16), wpg_ref[...]))
    h = h + gate * e
    o_ref[...] = _rms(h, finnw_ref[...])


def _tail(x2, yg, om, p2, mla_out_norm_w, w_out, mlp_norm_w, w_up, w_down, w_ple_proj,
          ple_post_norm_w, ple_gate_norm_w, w_ple_gate, final_norm_w):
    T = x2.shape[0]
    tm = min(TOK_TILE, T)
    tok = lambda w: pl.BlockSpec((tm, w), lambda i: (i, 0))
    const = lambda a: pl.BlockSpec(a.shape, lambda i: (0, 0), pipeline_mode=pl.Buffered(1))
    return pl.pallas_call(
        _tail_kernel,
        grid=(T // tm,),
        in_specs=[
            tok(D_MODEL), tok(GDN_V_W), tok(MLA_W), tok(PLE_DIM),
            const(mla_out_norm_w), const(w_out), const(mlp_norm_w), const(w_up), const(w_down),
            const(w_ple_proj), const(ple_post_norm_w), const(ple_gate_norm_w), const(w_ple_gate),
            const(final_norm_w),
        ],
        out_specs=tok(D_MODEL),
        out_shape=jax.ShapeDtypeStruct((T, D_MODEL), F32),
        compiler_params=pltpu.CompilerParams(
            dimension_semantics=("arbitrary",), vmem_limit_bytes=VMEM_LIMIT),
        name="tail",
    )(x2, yg, om, p2, mla_out_norm_w, w_out, mlp_norm_w, w_up, w_down, w_ple_proj,
      ple_post_norm_w, ple_gate_norm_w, w_ple_gate, final_norm_w)


def _pack_w_in(w):
    o_b = QKV_W + GDN_V_W
    o_cq = o_b + 2 * GDN_HEADS
    n_lat = MLA_Q_RANK + MLA_KV_RANK + MLA_ROPE
    pad = jnp.zeros((D_MODEL, LAT_W - n_lat - 2 * GDN_HEADS), w.dtype)
    return jnp.concatenate([w[:, :o_b], w[:, o_cq:o_cq + n_lat], w[:, o_b:o_cq], pad], axis=1).astype(BF16)


def _gate_lane_vector(per_head):
    return jnp.zeros((1, LANES), F32).at[0, A_OFF:A_OFF + GDN_HEADS].set(per_head.astype(F32))


def _pack_mla_weights(w_q_b, w_kv_b):
    wq = w_q_b.reshape(MLA_Q_RANK, MLA_HEADS, MLA_NOPE + MLA_ROPE)
    wq = jnp.pad(wq, ((0, 0), (0, 0), (0, LANES - MLA_NOPE - MLA_ROPE)))
    wkv = w_kv_b.reshape(MLA_KV_RANK, MLA_HEADS, MLA_NOPE + MLA_V)
    wk = jnp.pad(wkv[:, :, :MLA_NOPE], ((0, 0), (0, 0), (0, LANES - MLA_NOPE)))
    wv = jnp.pad(wkv[:, :, MLA_NOPE:], ((0, 0), (0, 0), (0, LANES - MLA_V)))
    half = MLA_ROPE // 2
    zeros = lambda n: jnp.zeros((MLA_Q_RANK, MLA_HEADS, n), wq.dtype)
    wq_sw = jnp.concatenate([zeros(MLA_NOPE), wq[:, :, MLA_NOPE + half:MLA_NOPE + MLA_ROPE],
                             wq[:, :, MLA_NOPE:MLA_NOPE + half], zeros(LANES - MLA_NOPE - MLA_ROPE)], axis=2)
    return (wq.reshape(MLA_Q_RANK, MLA_HEADS * LANES).astype(BF16),
            wq_sw.reshape(MLA_Q_RANK, MLA_HEADS * LANES).astype(BF16),
            wk.reshape(MLA_KV_RANK, MLA_HEADS * LANES).astype(BF16),
            wv.reshape(MLA_KV_RANK, MLA_HEADS * LANES).T.astype(BF16))


def kernel(x, p, positions, mix_norm_w, w_in, conv_w, A_log, dt_bias, gdn_norm_w, q_norm_w, w_q_b,
           kv_norm_w, w_kv_b, mla_out_norm_w, w_out, mlp_norm_w, w_up, w_down, w_ple_proj,
           ple_post_norm_w, ple_gate_norm_w, w_ple_gate, final_norm_w):
    B, S, _ = x.shape
    T = B * S
    assert w_in.shape[0] == 1, "one layer"
    row = lambda a: a.reshape(1, -1).astype(F32)
    x2 = x.reshape(T, D_MODEL)

    qkv, zg, lat = _inproj(x2, row(mix_norm_w[0]), _pack_w_in(w_in[0]), conv_w[0].astype(F32),
                           _gate_lane_vector(A_log[0]), _gate_lane_vector(dt_bias[0]), S)

    factors = _gdn_prep(qkv.reshape(B, S, QKV_W), lat.reshape(B, S, LAT_W))
    y_gdn = _gdn_scan(*factors, zg.reshape(B, S, GDN_V_W), row(jnp.tile(gdn_norm_w[0], GDN_HEADS)))

    half = MLA_ROPE // 2
    inv_freq = (ROPE_THETA ** (-jnp.arange(0, MLA_ROPE, 2, dtype=F32) / MLA_ROPE)).reshape(half, 1)
    q_att, k_att, v_att = _mla_prep(positions.reshape(1, T), inv_freq, lat, row(q_norm_w[0]),
                                    row(kv_norm_w[0]), *_pack_mla_weights(w_q_b[0], w_kv_b[0]))
    o_mla = _attn(q_att.reshape(B, S, -1), k_att.reshape(B, S, -1), v_att)

    out = _tail(x2, y_gdn.reshape(T, GDN_V_W), o_mla.reshape(T, MLA_W), p[0].reshape(T, PLE_DIM),
                row(mla_out_norm_w[0]), w_out[0].astype(BF16), row(mlp_norm_w[0]),
                w_up[0].astype(BF16), w_down[0].astype(BF16), w_ple_proj[0].astype(BF16),
                row(ple_post_norm_w[0]), row(ple_gate_norm_w[0]), w_ple_gate[0].astype(BF16),
                row(final_norm_w))
    return out.reshape(B, S, D_MODEL)
```

```python
import functools

import jax
import jax.numpy as jnp
from jax import lax
from jax.experimental import pallas as pl
from jax.experimental.pallas import tpu as pltpu

F32 = jnp.float32
BF16 = jnp.bfloat16

D_MODEL = 1024
PLE_DIM = 256
GDN_HEADS = 8
GDN_DK = 64
GDN_DV = 64
GDN_QK_W = GDN_HEADS * GDN_DK
GDN_V_W = GDN_HEADS * GDN_DV
GDN_CONV = 4
GDN_CHUNK = 64
MLA_HEADS = 8
MLA_NOPE = 64
MLA_ROPE = 32
MLA_V = 64
MLA_W = MLA_HEADS * MLA_V
MLA_Q_RANK = 256
MLA_KV_RANK = 128
ROPE_THETA = 10000.0
D_FF = 4 * D_MODEL
EPS = 1e-6

LANES = 128
SUBLANES = 8
HEAD_PAIRS = GDN_HEADS // 2
QKV_W = 2 * GDN_QK_W + GDN_V_W
LAT_W = 512
IN_W_PAD = QKV_W + GDN_V_W + LAT_W
KPE_OFF, B_OFF, A_OFF = 0, MLA_ROPE, MLA_ROPE + GDN_HEADS
VMEM_LIMIT = 56 * 1024 * 1024

TOK_TILE = 512
CUM_ROWS = 256
CONV_COLS = 512
GDN_ROWS = 1024
INV_BASE = 8
SCAN_ROWS = 512
SCAN_BATCH = 4
ATT_BLOCK = 512
ATT_HEADS = 4
ATT_UNROLL = 4
ATT_AHEAD = 3
NEG_BIG = -1e30
LOG2E = 1.4426950408889634


def _dot(a, b):
    return jnp.dot(a, b, preferred_element_type=F32)


def _dot_nt(a, b):
    return lax.dot_general(a, b, (((1,), (1,)), ((), ())), preferred_element_type=F32)


def _dot_tn(a, b):
    return lax.dot_general(a, b, (((0,), (0,)), ((), ())), preferred_element_type=F32)


def _split2(x):
    hi = x.astype(BF16)
    lo = (x - hi.astype(F32)).astype(BF16)
    return hi, lo


def _split3(x):
    hi = x.astype(BF16)
    r1 = x - hi.astype(F32)
    mid = r1.astype(BF16)
    lo = (r1 - mid.astype(F32)).astype(BF16)
    return hi, mid, lo


def _dot_exact_rhs(a_bf16, x):
    return _dot(jnp.concatenate([a_bf16] * 3, axis=1), jnp.concatenate(_split3(x), axis=0))


def _dot_exact_lhs(x, b_bf16):
    return _dot(jnp.concatenate(_split3(x), axis=1), jnp.concatenate([b_bf16] * 3, axis=0))


def _rms(x, w):
    ms = jnp.mean(x * x, axis=-1, keepdims=True)
    return x * lax.rsqrt(ms + EPS) * w


def _sigmoid(x):
    return 1.0 / (1.0 + jnp.exp(-x))


def _silu(x):
    h = 0.5 * x
    return h + h * jnp.tanh(h)


def _softplus(x):
    return jnp.maximum(x, 0.0) + jnp.log(1.0 + jnp.exp(-jnp.abs(x)))


def _chunk_tril(n):
    r = lax.broadcasted_iota(jnp.int32, (n, n), 0)
    c = lax.broadcasted_iota(jnp.int32, (n, n), 1)
    return jnp.where(((r // GDN_CHUNK) == (c // GDN_CHUNK)) & (c <= r), 1.0, 0.0).astype(BF16)


def _inproj_kernel(x_ref, nw_ref, w_ref, cw_ref, alog_ref, dtb_ref, qkv_ref, z_ref, lat_ref, halo_ref,
                   *, tiles_per_seq):
    tm = x_ref.shape[0]
    first = (pl.program_id(0) % tiles_per_seq) == 0
    u = _rms(x_ref[...], nw_ref[...]).astype(BF16)

    def conv_chunk(c):
        cols = slice(c * CONV_COLS, (c + 1) * CONV_COLS)
        raw = _dot(u, w_ref[:, cols])
        prev = jnp.where(first, 0.0, halo_ref[:, cols])
        halo_ref[:, cols] = raw[tm - SUBLANES:, :]
        ext = jnp.concatenate([prev, raw], axis=0)
        cw = cw_ref[:, cols]
        acc = raw * cw[GDN_CONV - 1:GDN_CONV, :]
        for kk in range(GDN_CONV - 1):
            shifted = pltpu.roll(ext, GDN_CONV - 1 - kk, axis=0)[SUBLANES:, :]
            acc = acc + shifted * cw[kk:kk + 1, :]
        qkv_ref[:, cols] = _silu(acc)

    def gate_z():
        zf = _dot(u, w_ref[:, QKV_W:QKV_W + GDN_V_W])
        z_ref[...] = _silu(zf).astype(z_ref.dtype)

    def latent_head():
        lat_ref[:, :LAT_W - LANES] = _dot(u, w_ref[:, QKV_W + GDN_V_W:IN_W_PAD - LANES])

    def latent_gates():
        l3 = _dot(u, w_ref[:, IN_W_PAD - LANES:IN_W_PAD])
        lane = lax.broadcasted_iota(jnp.int32, (1, LANES), 1)
        is_b = (lane >= B_OFF) & (lane < B_OFF + GDN_HEADS)
        is_a = (lane >= A_OFF) & (lane < A_OFF + GDN_HEADS)
        g_raw = jnp.where(is_a, -(jnp.exp(alog_ref[...]) * _softplus(l3 + dtb_ref[...])), 0.0)
        hi, mid, lo = (t.astype(F32) for t in _split3(g_raw))
        packed = (hi + pltpu.roll(mid, GDN_HEADS, axis=1) + pltpu.roll(lo, 2 * GDN_HEADS, axis=1)).astype(BF16)
        tril = _chunk_tril(CUM_ROWS)
        cum = jnp.concatenate([_dot(tril, packed[r:r + CUM_ROWS, :]) for r in range(0, tm, CUM_ROWS)], axis=0)
        g_cum = cum + pltpu.roll(cum, LANES - GDN_HEADS, axis=1) + pltpu.roll(cum, LANES - 2 * GDN_HEADS, axis=1)
        lat_ref[:, LAT_W - LANES:] = jnp.where(is_b, _sigmoid(l3), jnp.where(is_a, g_cum, l3))

    fillers = [latent_gates, gate_z, latent_head]
    for c in range(QKV_W // CONV_COLS):
        if fillers:
            fillers.pop(0)()
        conv_chunk(c)
    for f in fillers:
        f()


def _inproj(x2, mix_norm_w, w_in_p, conv_w, alog_vec, dtb_vec, seq_len):
    T = x2.shape[0]
    tm = min(TOK_TILE, seq_len)
    assert seq_len % tm == 0 and tm % GDN_CHUNK == 0
    nt = T // tm
    const = lambda a: pl.BlockSpec(a.shape, lambda i: (0, 0))
    return pl.pallas_call(
        functools.partial(_inproj_kernel, tiles_per_seq=seq_len // tm),
        grid=(nt,),
        in_specs=[
            pl.BlockSpec((tm, D_MODEL), lambda i: (i, 0)),
            const(mix_norm_w), const(w_in_p), const(conv_w), const(alog_vec), const(dtb_vec),
        ],
        out_specs=[
            pl.BlockSpec((tm, QKV_W), lambda i: (i, 0)),
            pl.BlockSpec((tm, GDN_V_W), lambda i: (i, 0)),
            pl.BlockSpec((tm, LAT_W), lambda i: (i, 0)),
        ],
        out_shape=[
            jax.ShapeDtypeStruct((T, QKV_W), F32),
            jax.ShapeDtypeStruct((T, GDN_V_W), BF16),
            jax.ShapeDtypeStruct((T, LAT_W), F32),
        ],
        scratch_shapes=[pltpu.VMEM((SUBLANES, QKV_W), F32)],
        compiler_params=pltpu.CompilerParams(
            dimension_semantics=("arbitrary",), vmem_limit_bytes=VMEM_LIMIT),
        name="inproj",
    )(x2, mix_norm_w, w_in_p, conv_w, alog_vec, dtb_vec)


def _gdn_prep_kernel(q_ref, k_ref, v_ref, lat_ref, u_ref, w_ref, qd_ref, kd_ref, a_ref, sd_ref):
    C = GDN_CHUNK
    S = q_ref.shape[0]
    R = min(GDN_ROWS, S)
    G = R // C
    hp = pl.program_id(1)

    lane = lax.broadcasted_iota(jnp.int32, (1, LANES), 1)
    head1 = lane >= C
    m0 = jnp.where(head1, 0.0, 1.0).astype(BF16)
    m1 = jnp.where(head1, 1.0, 0.0).astype(BF16)
    r128 = lax.broadcasted_iota(jnp.int32, (LANES, LANES), 0)
    c128 = lax.broadcasted_iota(jnp.int32, (LANES, LANES), 1)
    ones_bd = jnp.where((r128 // C) == (c128 // C), 1.0, 0.0).astype(BF16)
    rC = lax.broadcasted_iota(jnp.int32, (C, LANES), 0)
    cC = lax.broadcasted_iota(jnp.int32, (C, LANES), 1) % C
    eye2 = rC == cC
    tril2 = rC >= cC
    strict2 = rC > cC
    eye2f = jnp.where(eye2, 1.0, 0.0)
    ones_cc = jnp.ones((C, C), BF16)
    sel_r = lax.broadcasted_iota(jnp.int32, (LANES, 2 * LANES), 0)
    sel_c = lax.broadcasted_iota(jnp.int32, (LANES, 2 * LANES), 1)
    sel_src = jnp.where(sel_c < LANES, B_OFF, A_OFF) + 2 * hp + ((sel_c % LANES) // C)
    sel = jnp.where(sel_r == sel_src, 1.0, 0.0).astype(BF16)

    def bd16(m):
        return jnp.concatenate([m * m0, m * m1], axis=0)

    def bd_parts(m):
        hi, lo = _split2(m)
        return bd16(hi), bd16(lo)

    def dot_hi(a, b_hi, b_lo):
        a_hi, a_lo = _split2(a)
        return _dot(jnp.concatenate([a_hi, a_hi, a_lo], axis=1), jnp.concatenate([b_hi, b_lo, b_hi], axis=0))

    def merge_dot(a, b):
        return _dot(a.astype(BF16), bd16(b.astype(BF16)))

    diag_blk = (rC // INV_BASE) == (cC // INV_BASE)
    off_blks = []
    size = INV_BASE
    while size < C:
        off_blks.append(((rC // (2 * size)) == (cC // (2 * size))) & ((rC // size) % 2 == 1) & ((cC // size) % 2 == 0))
        size *= 2

    def tile(t, carry):
        rows = pl.ds(pl.multiple_of(t * R, R), R)
        q = q_ref[rows, :]
        k = k_ref[rows, :]
        v = v_ref[rows, :]
        q = q * lax.rsqrt(_dot((q * q).astype(BF16), ones_bd) + EPS) * (GDN_DK ** -0.5)
        k = k * lax.rsqrt(_dot((k * k).astype(BF16), ones_bd) + EPS)
        bg = _dot_exact_lhs(lat_ref[rows, :], sel)
        beta = bg[:, :LANES]
        gc = bg[:, LANES:]
        eg = jnp.exp(gc)
        kb = k * beta
        vb16 = (v * beta).astype(BF16)
        kbg16 = (kb * eg).astype(BF16)
        qd_ref[rows, :] = (q * eg).astype(BF16)
        k16 = k.astype(BF16)
        q16 = q.astype(BF16)

        ch = lambda arr, g: arr[g * C:(g + 1) * C, :]
        rng = range(G)
        kkqk = [_dot_nt(jnp.concatenate([ch(k16, g), ch(q16, g)], axis=0), bd16(ch(k16, g))) for g in rng]
        g_row = [_dot_exact_rhs(ones_cc, jnp.where(eye2, ch(gc, g), 0.0)) for g in rng]
        dm = [jnp.where(tril2, jnp.exp(jnp.where(tril2, ch(gc, g) - g_row[g], 0.0)), 0.0) for g in rng]
        for g in rng:
            a_ref[pl.ds(pl.multiple_of(t * R + g * C, C), C), :] = (kkqk[g][C:] * dm[g]).astype(BF16)
            g_last = ch(gc, g)[C - 1:C, :]
            kd_ref[pl.ds(pl.multiple_of(t * R + g * C, C), C), :] = (
                ch(k, g) * jnp.exp(g_last - ch(gc, g))).astype(BF16)
            sd_ref[pl.ds(pl.multiple_of((t * G + g) * SUBLANES, SUBLANES), SUBLANES), :] = jnp.broadcast_to(
                jnp.exp(g_last), (SUBLANES, LANES))
        low = [jnp.where(strict2, kkqk[g][:C] * dm[g] * ch(beta, g), 0.0) for g in rng]
        x = [jnp.where(diag_blk, -low[g], 0.0) for g in rng]
        p = [eye2f + x[g] for g in rng]
        y = [dot_hi(x[g], *bd_parts(x[g])) for g in rng]
        rhs = []
        for g in rng:
            p_hi, p_lo = bd_parts(p[g])
            y_hi, y_lo = bd_parts(y[g])
            rhs.append((jnp.concatenate([p_hi, y_hi], axis=1), jnp.concatenate([p_lo, y_lo], axis=1)))
        res = [dot_hi(y[g], *rhs[g]) for g in rng]
        p = [p[g] + res[g][:, :LANES] for g in rng]
        y = [res[g][:, LANES:] for g in rng]
        p = [p[g] + dot_hi(y[g], *bd_parts(p[g])) for g in rng]
        for off_blk in off_blks:
            m1 = [merge_dot(jnp.where(off_blk, low[g], 0.0), p[g]) for g in rng]
            p = [p[g] - merge_dot(p[g], m1[g]) for g in rng]
        uw = [_dot(p[g].astype(BF16), jnp.concatenate([bd16(ch(vb16, g)), bd16(ch(kbg16, g))], axis=1))
              for g in rng]
        for g in rng:
            dst = pl.ds(pl.multiple_of(t * R + g * C, C), C)
            u_ref[dst, :] = uw[g][:, :LANES]
            w_ref[dst, :] = uw[g][:, LANES:].astype(BF16)
        return carry

    lax.fori_loop(0, S // R, tile, 0)


def _gdn_prep(qkv3, lat3):
    B, S, _ = qkv3.shape
    HP = HEAD_PAIRS
    seq_blk = lambda off: pl.BlockSpec((None, S, LANES), lambda b, h, off=off: (b, 0, off + h))
    n_sd = S // GDN_CHUNK * SUBLANES
    return pl.pallas_call(
        _gdn_prep_kernel,
        grid=(B, HP),
        in_specs=[
            seq_blk(0), seq_blk(HP), seq_blk(2 * HP),
            pl.BlockSpec((None, S, LANES), lambda b, h: (b, 0, LAT_W // LANES - 1)),
        ],
        out_specs=[seq_blk(0)] * 5 + [pl.BlockSpec((None, n_sd, LANES), lambda b, h: (b, 0, h))],
        out_shape=[
            jax.ShapeDtypeStruct((B, S, GDN_V_W), F32),
            jax.ShapeDtypeStruct((B, S, GDN_V_W), BF16),
            jax.ShapeDtypeStruct((B, S, GDN_V_W), BF16),
            jax.ShapeDtypeStruct((B, S, GDN_V_W), BF16),
            jax.ShapeDtypeStruct((B, S, GDN_V_W), BF16),
            jax.ShapeDtypeStruct((B, n_sd, GDN_V_W), F32),
        ],
        compiler_params=pltpu.CompilerParams(
            dimension_semantics=("arbitrary", "arbitrary"), vmem_limit_bytes=VMEM_LIMIT),
        name="gdn_prep",
    )(qkv3, qkv3, qkv3, lat3)


def _gdn_scan_kernel(u_ref, w_ref, qd_ref, kd_ref, a_ref, sd_ref, z_ref, nw_ref, o_ref, state_ref):
    C = GDN_CHUNK
    nb, rt, _ = u_ref.shape

    @pl.when(pl.program_id(1) == 0)
    def _():
        state_ref[...] = jnp.zeros_like(state_ref)

    lane = lax.broadcasted_iota(jnp.int32, (1, LANES), 1)
    head1 = lane >= C
    m0 = jnp.where(head1, 0.0, 1.0).astype(BF16)
    m1 = jnp.where(head1, 1.0, 0.0).astype(BF16)
    r128 = lax.broadcasted_iota(jnp.int32, (LANES, LANES), 0)
    c128 = lax.broadcasted_iota(jnp.int32, (LANES, LANES), 1)
    bdmask = (r128 // C) == (c128 // C)
    ones_bd = jnp.where(bdmask, 1.0, 0.0).astype(BF16)

    chains = [(bb, hp) for bb in range(nb) for hp in range(HEAD_PAIRS)]
    blk = lambda hp: slice(hp * LANES, (hp + 1) * LANES)

    def bd16(m):
        return jnp.concatenate([m * m0, m * m1], axis=0)

    def chunk(n, carry):
        rows = pl.ds(pl.multiple_of(n * C, C), C)
        sd_rows = pl.ds(pl.multiple_of(n * SUBLANES, SUBLANES), SUBLANES)
        state = [state_ref[i] for i in range(len(chains))]
        res = [_dot(jnp.concatenate([w_ref[bb, rows, blk(hp)], qd_ref[bb, rows, blk(hp)]], axis=0),
                    state[i].astype(BF16)) for i, (bb, hp) in enumerate(chains)]
        v_new = [(u_ref[bb, rows, blk(hp)] - res[i][:C]).astype(BF16) for i, (bb, hp) in enumerate(chains)]
        upd = [_dot_tn(kd_ref[bb, rows, blk(hp)], v_new[i]) for i, (bb, hp) in enumerate(chains)]
        for i, (bb, hp) in enumerate(chains):
            sd = sd_ref[bb, sd_rows, blk(hp)][0:1, :]
            state_ref[i] = state[i] * sd + jnp.where(bdmask, upd[i], 0.0)
        o = [res[i][C:] + _dot(a_ref[bb, rows, blk(hp)], bd16(v_new[i])) for i, (bb, hp) in enumerate(chains)]
        for i, (bb, hp) in enumerate(chains):
            ms = _dot((o[i] * o[i]).astype(BF16), ones_bd) * (1.0 / GDN_DV)
            y = o[i] * lax.rsqrt(ms + EPS) * nw_ref[:, blk(hp)] * z_ref[bb, rows, blk(hp)].astype(F32)
            o_ref[bb, rows, blk(hp)] = y.astype(o_ref.dtype)
        return carry

    lax.fori_loop(0, rt // C, chunk, 0)


def _gdn_scan(u, w, qd, kd, a, sd, zg, norm_w8):
    B, S, W = u.shape
    rt = min(SCAN_ROWS, S)
    nb = min(SCAN_BATCH, B)
    assert B % nb == 0 and S % rt == 0
    n_sd = rt // GDN_CHUNK * SUBLANES
    seq = pl.BlockSpec((nb, rt, W), lambda b, t: (b, t, 0))
    return pl.pallas_call(
        _gdn_scan_kernel,
        grid=(B // nb, S // rt),
        in_specs=[seq, seq, seq, seq, seq,
                  pl.BlockSpec((nb, n_sd, W), lambda b, t: (b, t, 0)),
                  seq,
                  pl.BlockSpec((1, W), lambda b, t: (0, 0))],
        out_specs=seq,
        out_shape=jax.ShapeDtypeStruct((B, S, W), BF16),
        scratch_shapes=[pltpu.VMEM((nb * HEAD_PAIRS, LANES, LANES), F32)],
        compiler_params=pltpu.CompilerParams(
            dimension_semantics=("arbitrary", "arbitrary"), vmem_limit_bytes=VMEM_LIMIT),
        name="gdn_scan",
    )(u, w, qd, kd, a, sd, zg, norm_w8)


def _mla_prep_kernel(pos_ref, freq_ref, lat_ref, qnw_ref, kvnw_ref, wq_ref, wqs_ref, wk_ref, wv_ref,
                     q_ref, k_ref, v_ref):
    half = MLA_ROPE // 2
    lo, mid, hi = MLA_NOPE, MLA_NOPE + half, MLA_NOPE + MLA_ROPE
    scale = (MLA_NOPE + MLA_ROPE) ** -0.5 * LOG2E
    ang = freq_ref[...] * pos_ref[...].astype(F32)
    f_i = lax.broadcasted_iota(jnp.int32, (3 * half, LANES), 0) % half
    l_i = lax.broadcasted_iota(jnp.int32, (3 * half, LANES), 1)
    in_lo = (l_i >= lo) & (l_i < mid) & (l_i - lo == f_i)
    in_hi = (l_i >= mid) & (l_i < hi) & (l_i - mid == f_i)
    expand_cos = jnp.where(in_lo | in_hi, 1.0, 0.0).astype(BF16)
    expand_sin = jnp.where(in_lo, -1.0, jnp.where(in_hi, 1.0, 0.0)).astype(BF16)
    cos_tab = _dot_tn(jnp.concatenate(_split3(jnp.cos(ang)), axis=0), expand_cos)
    sin_tab = _dot_tn(jnp.concatenate(_split3(jnp.sin(ang)), axis=0), expand_sin)
    lane = lax.broadcasted_iota(jnp.int32, (1, LANES), 1)
    nope = jnp.where(lane < MLA_NOPE, 1.0, 0.0)

    lat = lat_ref[...]
    cq = _rms(lat[:, 0:MLA_Q_RANK], qnw_ref[...]).astype(BF16)
    ckv = _rms(lat[:, MLA_Q_RANK:MLA_Q_RANK + MLA_KV_RANK], kvnw_ref[...]).astype(BF16)
    q = _dot(cq, wq_ref[...])
    q_sw = _dot(cq, wqs_ref[...])
    k_nope = _dot(ckv, wk_ref[...])
    row_all = lax.broadcasted_iota(jnp.int32, (MLA_HEADS * LANES, 1), 0)
    ones_row = jnp.where(row_all % LANES == MLA_V, 1.0, 0.0)
    v_ref[...] = (_dot_nt(wv_ref[...], ckv) + ones_row).astype(v_ref.dtype)

    kpe = pltpu.roll(lat[:, LAT_W - LANES:], MLA_NOPE - KPE_OFF, axis=1)
    kpe_sw = jnp.where(lane < mid, pltpu.roll(kpe, LANES - half, axis=1), pltpu.roll(kpe, half, axis=1))
    k_rope = kpe * cos_tab + kpe_sw * sin_tab
    c_q = (cos_tab + nope) * scale
    s_q = sin_tab * scale
    for h in range(MLA_HEADS):
        blk = slice(h * LANES, (h + 1) * LANES)
        q_ref[:, blk] = (q[:, blk] * c_q + q_sw[:, blk] * s_q).astype(q_ref.dtype)
        k_ref[:, blk] = (k_nope[:, blk] + k_rope).astype(k_ref.dtype)


def _mla_prep(pos_row, inv_freq, lat, q_norm_w, kv_norm_w, wq_p, wqs_p, wk_p, wv_p):
    T = lat.shape[0]
    tm = min(TOK_TILE, T)
    full = lambda a: pl.BlockSpec(a.shape, lambda i: (0,) * a.ndim)
    return pl.pallas_call(
        _mla_prep_kernel,
        grid=(T // tm,),
        in_specs=[
            pl.BlockSpec((1, tm), lambda i: (0, i)),
            full(inv_freq),
            pl.BlockSpec((tm, LAT_W), lambda i: (i, 0)),
            full(q_norm_w), full(kv_norm_w), full(wq_p), full(wqs_p), full(wk_p), full(wv_p),
        ],
        out_specs=[
            pl.BlockSpec((tm, MLA_HEADS * LANES), lambda i: (i, 0)),
            pl.BlockSpec((tm, MLA_HEADS * LANES), lambda i: (i, 0)),
            pl.BlockSpec((MLA_HEADS * LANES, tm), lambda i: (0, i)),
        ],
        out_shape=[
            jax.ShapeDtypeStruct((T, MLA_HEADS * LANES), BF16),
            jax.ShapeDtypeStruct((T, MLA_HEADS * LANES), BF16),
            jax.ShapeDtypeStruct((MLA_HEADS * LANES, T), BF16),
        ],
        compiler_params=pltpu.CompilerParams(
            dimension_semantics=("arbitrary",), vmem_limit_bytes=VMEM_LIMIT),
        name="mla_prep",
    )(pos_row, inv_freq, lat, q_norm_w, kv_norm_w, wq_p, wqs_p, wk_p, wv_p)


def _attn_kernel(q_ref, k_ref, vt_ref, o_ref, s_ref, m_ref, acc_ref, *, tq):
    seq = q_ref.shape[0]
    nq = seq // tq
    nh = q_ref.shape[1] // LANES
    lane = lax.broadcasted_iota(jnp.int32, (1, LANES), 1)
    causal = lax.broadcasted_iota(jnp.int32, (tq, tq), 0) <= lax.broadcasted_iota(jnp.int32, (tq, tq), 1)
    blks = [slice(h * LANES, (h + 1) * LANES) for h in range(nh)]
    rows = lambda i: pl.ds(pl.multiple_of(i * tq, tq), tq)

    def scores(qi, j, h):
        s_ref[h] = _dot_nt(k_ref[rows(j), blks[h]], q_ref[rows(qi), blks[h]])

    def consume(j, h, masked):
        m_prev = m_ref[h][0:1, :]
        s = s_ref[h]
        if masked:
            s = jnp.where(causal, s, NEG_BIG)
        m_new = jnp.maximum(m_prev, jnp.max(s, axis=0, keepdims=True))
        alpha = jnp.exp2(m_prev - m_new)
        p = jnp.exp2(s - m_new).astype(BF16)
        m_ref[h] = jnp.broadcast_to(m_new, (SUBLANES, tq))
        acc_ref[h] = acc_ref[h] * alpha + _dot(vt_ref[blks[h], rows(j)], p)

    def consume_and_prefetch(qi, j, h, masked):
        consume(j, h, masked)
        nxt = h + ATT_AHEAD
        if nxt < nh:
            scores(qi, j, nxt)
        elif not masked:
            scores(qi, j + 1, nxt - nh)
        else:
            scores(jnp.minimum(qi + 1, nq - 1), 0, nxt - nh)

    def q_block(qi, carry):
        def blocks(j0, count):
            for j in range(count):
                for h in range(nh):
                    consume_and_prefetch(qi, j0 + j, h, False)

        def unrolled(i, c):
            blocks(ATT_UNROLL * i, ATT_UNROLL)
            return c

        m_ref[...] = jnp.full(m_ref.shape, NEG_BIG, F32)
        acc_ref[...] = jnp.zeros(acc_ref.shape, F32)
        n_full = qi
        lax.fori_loop(0, n_full // ATT_UNROLL, unrolled, 0)
        done = n_full - n_full % ATT_UNROLL
        size = ATT_UNROLL // 2
        while size >= 1:
            take = (n_full % (2 * size)) >= size

            @pl.when(take)
            def _(done=done, size=size):
                blocks(done, size)

            done = done + jnp.where(take, size, 0)
            size //= 2
        for h in range(nh):
            consume_and_prefetch(qi, n_full, h, True)
        outs = []
        for h in range(nh):
            acc = acc_ref[h]
            outs.append((acc * (1.0 / acc[MLA_V:MLA_V + 1, :])).T)
        for pr in range(nh // 2):
            pair = jnp.where(lane < MLA_V, outs[2 * pr], pltpu.roll(outs[2 * pr + 1], MLA_V, axis=1))
            o_ref[rows(qi), pr * LANES:(pr + 1) * LANES] = pair.astype(o_ref.dtype)
        return carry

    for h in range(ATT_AHEAD):
        scores(0, 0, h)
    lax.fori_loop(0, nq, q_block, 0)


def _attn(q3, k3, vt):
    B, S, _ = q3.shape
    tq = min(ATT_BLOCK, S)
    assert S % tq == 0
    nh = ATT_HEADS
    return pl.pallas_call(
        functools.partial(_attn_kernel, tq=tq),
        grid=(B, MLA_HEADS // nh),
        in_specs=[
            pl.BlockSpec((None, S, nh * LANES), lambda b, h: (b, 0, h)),
            pl.BlockSpec((None, S, nh * LANES), lambda b, h: (b, 0, h)),
            pl.BlockSpec((nh * LANES, S), lambda b, h: (h, b)),
        ],
        out_specs=pl.BlockSpec((None, S, nh * MLA_V), lambda b, h: (b, 0, h)),
        out_shape=jax.ShapeDtypeStruct((B, S, MLA_W), BF16),
        scratch_shapes=[pltpu.VMEM((nh, tq, tq), F32),
                        pltpu.VMEM((nh, SUBLANES, tq), F32),
                        pltpu.VMEM((nh, LANES, tq), F32)],
        compiler_params=pltpu.CompilerParams(
            dimension_semantics=("arbitrary", "arbitrary"), vmem_limit_bytes=VMEM_LIMIT),
        name="attn",
    )(q3, k3, vt)


def _tail_kernel(x_ref, yg_ref, om_ref, p_ref, monw_ref, wout_ref, mlpnw_ref, wup_ref, wdown_ref,
                 wpp_ref, postnw_ref, gatenw_ref, wpg_ref, finnw_ref, o_ref):
    ymla = _rms(om_ref[...].astype(F32), monw_ref[...]).astype(BF16)
    h = x_ref[...] + _dot(yg_ref[...], wout_ref[0:GDN_V_W, :]) + _dot(ymla, wout_ref[GDN_V_W:, :])
    u = _rms(h, mlpnw_ref[...]).astype(BF16)
    ff_blk = 1024
    acc = jnp.zeros_like(h)
    for j in range(D_FF // ff_blk):
        cols = slice(j * ff_blk, (j + 1) * ff_blk)
        hid = jnp.maximum(_dot(u, wup_ref[:, cols]), 0.0)
        acc = acc + _dot((hid * hid).astype(BF16), wdown_ref[cols, :])
    h = h + acc
    e = _rms(_dot(p_ref[...].astype(BF16), wpp_ref[...]), postnw_ref[...])
    gate = _sigmoid(_dot(_rms(h, gatenw_ref[...]).astype(BF16), wpg_ref[...]))
    h = h + gate * e
    o_ref[...] = _rms(h, finnw_ref[...])


def _tail(x2, yg, om, p2, mla_out_norm_w, w_out, mlp_norm_w, w_up, w_down, w_ple_proj,
          ple_post_norm_w, ple_gate_norm_w, w_ple_gate, final_norm_w):
    T = x2.shape[0]
    tm = min(TOK_TILE, T)
    tok = lambda w: pl.BlockSpec((tm, w), lambda i: (i, 0))
    const = lambda a: pl.BlockSpec(a.shape, lambda i: (0, 0), pipeline_mode=pl.Buffered(1))
    return pl.pallas_call(
        _tail_kernel,
        grid=(T // tm,),
        in_specs=[
            tok(D_MODEL), tok(GDN_V_W), tok(MLA_W), tok(PLE_DIM),
            const(mla_out_norm_w), const(w_out), const(mlp_norm_w), const(w_up), const(w_down),
            const(w_ple_proj), const(ple_post_norm_w), const(ple_gate_norm_w), const(w_ple_gate),
            const(final_norm_w),
        ],
        out_specs=tok(D_MODEL),
        out_shape=jax.ShapeDtypeStruct((T, D_MODEL), F32),
        compiler_params=pltpu.CompilerParams(
            dimension_semantics=("arbitrary",), vmem_limit_bytes=VMEM_LIMIT),
        name="tail",
    )(x2, yg, om, p2, mla_out_norm_w, w_out, mlp_norm_w, w_up, w_down, w_ple_proj,
      ple_post_norm_w, ple_gate_norm_w, w_ple_gate, final_norm_w)


def _pack_w_in(w):
    o_b = QKV_W + GDN_V_W
    o_cq = o_b + 2 * GDN_HEADS
    n_lat = MLA_Q_RANK + MLA_KV_RANK + MLA_ROPE
    w = w.astype(BF16)
    pad = jnp.zeros((D_MODEL, LAT_W - n_lat - 2 * GDN_HEADS), BF16)
    return jnp.concatenate([w[:, :o_b], w[:, o_cq:o_cq + n_lat], w[:, o_b:o_cq], pad], axis=1)


def _gate_lane_vector(per_head):
    return jnp.zeros((1, LANES), F32).at[0, A_OFF:A_OFF + GDN_HEADS].set(per_head.astype(F32))


def _pack_mla_weights(w_q_b, w_kv_b):
    wq = w_q_b.reshape(MLA_Q_RANK, MLA_HEADS, MLA_NOPE + MLA_ROPE)
    wq = jnp.pad(wq, ((0, 0), (0, 0), (0, LANES - MLA_NOPE - MLA_ROPE)))
    wkv = w_kv_b.reshape(MLA_KV_RANK, MLA_HEADS, MLA_NOPE + MLA_V)
    wk = jnp.pad(wkv[:, :, :MLA_NOPE], ((0, 0), (0, 0), (0, LANES - MLA_NOPE)))
    wv = jnp.pad(wkv[:, :, MLA_NOPE:], ((0, 0), (0, 0), (0, LANES - MLA_V)))
    half = MLA_ROPE // 2
    zeros = lambda n: jnp.zeros((MLA_Q_RANK, MLA_HEADS, n), wq.dtype)
    wq_sw = jnp.concatenate([zeros(MLA_NOPE), wq[:, :, MLA_NOPE + half:MLA_NOPE + MLA_ROPE],
                             wq[:, :, MLA_NOPE:MLA_NOPE + half], zeros(LANES - MLA_NOPE - MLA_ROPE)], axis=2)
    return (wq.reshape(MLA_Q_RANK, MLA_HEADS * LANES).astype(BF16),
            wq_sw.reshape(MLA_Q_RANK, MLA_HEADS * LANES).astype(BF16),
            wk.reshape(MLA_KV_RANK, MLA_HEADS * LANES).astype(BF16),
            wv.reshape(MLA_KV_RANK, MLA_HEADS * LANES).T.astype(BF16))


def kernel(x, p, positions, mix_norm_w, w_in, conv_w, A_log, dt_bias, gdn_norm_w, q_norm_w, w_q_b,
           kv_norm_w, w_kv_b, mla_out_norm_w, w_out, mlp_norm_w, w_up, w_down, w_ple_proj,
           ple_post_norm_w, ple_gate_norm_w, w_ple_gate, final_norm_w):
    B, S, _ = x.shape
    T = B * S
    assert w_in.shape[0] == 1, "one layer"
    row = lambda a: a.reshape(1, -1).astype(F32)
    x2 = x.reshape(T, D_MODEL)

    qkv, zg, lat = _inproj(x2, row(mix_norm_w[0]), _pack_w_in(w_in[0]), conv_w[0].astype(F32),
                           _gate_lane_vector(A_log[0]), _gate_lane_vector(dt_bias[0]), S)

    factors = _gdn_prep(qkv.reshape(B, S, QKV_W), lat.reshape(B, S, LAT_W))
    y_gdn = _gdn_scan(*factors, zg.reshape(B, S, GDN_V_W), row(jnp.tile(gdn_norm_w[0], GDN_HEADS)))

    half = MLA_ROPE // 2
    inv_freq = (ROPE_THETA ** (-jnp.arange(0, MLA_ROPE, 2, dtype=F32) / MLA_ROPE)).reshape(half, 1)
    q_att, k_att, v_att = _mla_prep(positions.reshape(1, T), inv_freq, lat, row(q_norm_w[0]),
                                    row(kv_norm_w[0]), *_pack_mla_weights(w_q_b[0], w_kv_b[0]))
    o_mla = _attn(q_att.reshape(B, S, -1), k_att.reshape(B, S, -1), v_att)

    out = _tail(x2, y_gdn.reshape(T, GDN_V_W), o_mla.reshape(T, MLA_W), p[0].reshape(T, PLE_DIM),
                row(mla_out_norm_w[0]), w_out[0].astype(BF16), row(mlp_norm_w[0]),
                w_up[0].astype(BF16), w_down[0].astype(BF16), w_ple_proj[0].astype(BF16),
                row(ple_post_norm_w[0]), row(ple_gate_norm_w[0]), w_ple_gate[0].astype(BF16),
                row(final_norm_w))
    return out.reshape(B, S, D_MODEL)
```

```python
import functools

import jax
import jax.numpy as jnp
from jax import lax
from jax.experimental import pallas as pl
from jax.experimental.pallas import tpu as pltpu

F32 = jnp.float32
BF16 = jnp.bfloat16

D_MODEL = 1024
PLE_DIM = 256
GDN_HEADS = 8
GDN_DK = 64
GDN_DV = 64
GDN_QK_W = GDN_HEADS * GDN_DK
GDN_V_W = GDN_HEADS * GDN_DV
GDN_CONV = 4
GDN_CHUNK = 64
MLA_HEADS = 8
MLA_NOPE = 64
MLA_ROPE = 32
MLA_V = 64
MLA_W = MLA_HEADS * MLA_V
MLA_Q_RANK = 256
MLA_KV_RANK = 128
ROPE_THETA = 10000.0
D_FF = 4 * D_MODEL
EPS = 1e-6

LANES = 128
SUBLANES = 8
HEAD_PAIRS = GDN_HEADS // 2
QKV_W = 2 * GDN_QK_W + GDN_V_W
LAT_W = 512
IN_W_PAD = QKV_W + GDN_V_W + LAT_W
KPE_OFF, B_OFF, A_OFF = 0, MLA_ROPE, MLA_ROPE + GDN_HEADS
VMEM_LIMIT = 56 * 1024 * 1024

TOK_TILE = 512
CUM_ROWS = 256
CONV_COLS = 512
GDN_ROWS = 1024
INV_BASE = 8
SCAN_ROWS = 512
SCAN_BATCH = 4
ATT_BLOCK = 512
ATT_HEADS = 4
ATT_UNROLL = 4
ATT_AHEAD = 4
NEG_BIG = -1e30
LOG2E = 1.4426950408889634


def _dot(a, b):
    return jnp.dot(a, b, preferred_element_type=F32)


def _dot_nt(a, b):
    return lax.dot_general(a, b, (((1,), (1,)), ((), ())), preferred_element_type=F32)


def _dot_tn(a, b):
    return lax.dot_general(a, b, (((0,), (0,)), ((), ())), preferred_element_type=F32)


def _split2(x):
    hi = x.astype(BF16)
    lo = (x - hi.astype(F32)).astype(BF16)
    return hi, lo


def _split3(x):
    hi = x.astype(BF16)
    r1 = x - hi.astype(F32)
    mid = r1.astype(BF16)
    lo = (r1 - mid.astype(F32)).astype(BF16)
    return hi, mid, lo


def _dot_exact_rhs(a_bf16, x):
    return _dot(jnp.concatenate([a_bf16] * 3, axis=1), jnp.concatenate(_split3(x), axis=0))


def _dot_exact_lhs(x, b_bf16):
    return _dot(jnp.concatenate(_split3(x), axis=1), jnp.concatenate([b_bf16] * 3, axis=0))


def _rms(x, w):
    ms = jnp.mean(x * x, axis=-1, keepdims=True)
    return x * lax.rsqrt(ms + EPS) * w


def _sigmoid(x):
    return 1.0 / (1.0 + jnp.exp(-x))


def _silu(x):
    h = 0.5 * x
    return h + h * jnp.tanh(h)


def _softplus(x):
    return jnp.maximum(x, 0.0) + jnp.log(1.0 + jnp.exp(-jnp.abs(x)))


def _chunk_tril(n):
    r = lax.broadcasted_iota(jnp.int32, (n, n), 0)
    c = lax.broadcasted_iota(jnp.int32, (n, n), 1)
    return jnp.where(((r // GDN_CHUNK) == (c // GDN_CHUNK)) & (c <= r), 1.0, 0.0).astype(BF16)


def _inproj_kernel(x_ref, nw_ref, w_ref, cw_ref, alog_ref, dtb_ref, qkv_ref, z_ref, lat_ref, halo_ref,
                   *, tiles_per_seq):
    tm = x_ref.shape[0]
    first = (pl.program_id(0) % tiles_per_seq) == 0
    u = _rms(x_ref[...], nw_ref[...]).astype(BF16)

    def conv_chunk(c):
        cols = slice(c * CONV_COLS, (c + 1) * CONV_COLS)
        raw = _dot(u, w_ref[:, cols])
        prev = jnp.where(first, 0.0, halo_ref[:, cols])
        halo_ref[:, cols] = raw[tm - SUBLANES:, :]
        ext = jnp.concatenate([prev, raw], axis=0)
        cw = cw_ref[:, cols]
        acc = raw * cw[GDN_CONV - 1:GDN_CONV, :]
        for kk in range(GDN_CONV - 1):
            shifted = pltpu.roll(ext, GDN_CONV - 1 - kk, axis=0)[SUBLANES:, :]
            acc = acc + shifted * cw[kk:kk + 1, :]
        qkv_ref[:, cols] = _silu(acc)

    def gate_z():
        zf = _dot(u, w_ref[:, QKV_W:QKV_W + GDN_V_W])
        z_ref[...] = _silu(zf).astype(z_ref.dtype)

    def latent_head():
        lat_ref[:, :LAT_W - LANES] = _dot(u, w_ref[:, QKV_W + GDN_V_W:IN_W_PAD - LANES])

    def latent_gates():
        l3 = _dot(u, w_ref[:, IN_W_PAD - LANES:IN_W_PAD])
        lane = lax.broadcasted_iota(jnp.int32, (1, LANES), 1)
        is_b = (lane >= B_OFF) & (lane < B_OFF + GDN_HEADS)
        is_a = (lane >= A_OFF) & (lane < A_OFF + GDN_HEADS)
        g_raw = jnp.where(is_a, -(jnp.exp(alog_ref[...]) * _softplus(l3 + dtb_ref[...])), 0.0)
        hi, mid, lo = (t.astype(F32) for t in _split3(g_raw))
        packed = (hi + pltpu.roll(mid, GDN_HEADS, axis=1) + pltpu.roll(lo, 2 * GDN_HEADS, axis=1)).astype(BF16)
        tril = _chunk_tril(CUM_ROWS)
        cum = jnp.concatenate([_dot(tril, packed[r:r + CUM_ROWS, :]) for r in range(0, tm, CUM_ROWS)], axis=0)
        g_cum = cum + pltpu.roll(cum, LANES - GDN_HEADS, axis=1) + pltpu.roll(cum, LANES - 2 * GDN_HEADS, axis=1)
        lat_ref[:, LAT_W - LANES:] = jnp.where(is_b, _sigmoid(l3), jnp.where(is_a, g_cum, l3))

    fillers = [latent_gates, gate_z, latent_head]
    for c in range(QKV_W // CONV_COLS):
        if fillers:
            fillers.pop(0)()
        conv_chunk(c)
    for f in fillers:
        f()


def _inproj(x2, mix_norm_w, w_in_p, conv_w, alog_vec, dtb_vec, seq_len):
    T = x2.shape[0]
    tm = min(TOK_TILE, seq_len)
    assert seq_len % tm == 0 and tm % GDN_CHUNK == 0
    nt = T // tm
    const = lambda a: pl.BlockSpec(a.shape, lambda i: (0, 0))
    return pl.pallas_call(
        functools.partial(_inproj_kernel, tiles_per_seq=seq_len // tm),
        grid=(nt,),
        in_specs=[
            pl.BlockSpec((tm, D_MODEL), lambda i: (i, 0)),
            const(mix_norm_w), const(w_in_p), const(conv_w), const(alog_vec), const(dtb_vec),
        ],
        out_specs=[
            pl.BlockSpec((tm, QKV_W), lambda i: (i, 0)),
            pl.BlockSpec((tm, GDN_V_W), lambda i: (i, 0)),
            pl.BlockSpec((tm, LAT_W), lambda i: (i, 0)),
        ],
        out_shape=[
            jax.ShapeDtypeStruct((T, QKV_W), F32),
            jax.ShapeDtypeStruct((T, GDN_V_W), BF16),
            jax.ShapeDtypeStruct((T, LAT_W), F32),
        ],
        scratch_shapes=[pltpu.VMEM((SUBLANES, QKV_W), F32)],
        compiler_params=pltpu.CompilerParams(
            dimension_semantics=("arbitrary",), vmem_limit_bytes=VMEM_LIMIT),
        name="inproj",
    )(x2, mix_norm_w, w_in_p, conv_w, alog_vec, dtb_vec)


def _gdn_prep_kernel(q_ref, k_ref, v_ref, lat_ref, u_ref, w_ref, qd_ref, kd_ref, a_ref, sd_ref):
    C = GDN_CHUNK
    S = q_ref.shape[0]
    R = min(GDN_ROWS, S)
    G = R // C
    hp = pl.program_id(1)

    lane = lax.broadcasted_iota(jnp.int32, (1, LANES), 1)
    head1 = lane >= C
    m0 = jnp.where(head1, 0.0, 1.0).astype(BF16)
    m1 = jnp.where(head1, 1.0, 0.0).astype(BF16)
    r128 = lax.broadcasted_iota(jnp.int32, (LANES, LANES), 0)
    c128 = lax.broadcasted_iota(jnp.int32, (LANES, LANES), 1)
    ones_bd = jnp.where((r128 // C) == (c128 // C), 1.0, 0.0).astype(BF16)
    rC = lax.broadcasted_iota(jnp.int32, (C, LANES), 0)
    cC = lax.broadcasted_iota(jnp.int32, (C, LANES), 1) % C
    eye2 = rC == cC
    tril2 = rC >= cC
    strict2 = rC > cC
    eye2f = jnp.where(eye2, 1.0, 0.0)
    ones_cc = jnp.ones((C, C), BF16)
    sel_r = lax.broadcasted_iota(jnp.int32, (LANES, 2 * LANES), 0)
    sel_c = lax.broadcasted_iota(jnp.int32, (LANES, 2 * LANES), 1)
    sel_src = jnp.where(sel_c < LANES, B_OFF, A_OFF) + 2 * hp + ((sel_c % LANES) // C)
    sel = jnp.where(sel_r == sel_src, 1.0, 0.0).astype(BF16)

    def bd16(m):
        return jnp.concatenate([m * m0, m * m1], axis=0)

    def bd_parts(m):
        hi, lo = _split2(m)
        return bd16(hi), bd16(lo)

    def dot_hi(a, b_hi, b_lo):
        a_hi, a_lo = _split2(a)
        return _dot(jnp.concatenate([a_hi, a_hi, a_lo], axis=1), jnp.concatenate([b_hi, b_lo, b_hi], axis=0))

    def merge_dot(a, b):
        return _dot(a.astype(BF16), bd16(b.astype(BF16)))

    diag_blk = (rC // INV_BASE) == (cC // INV_BASE)
    off_blks = []
    size = INV_BASE
    while size < C:
        off_blks.append(((rC // (2 * size)) == (cC // (2 * size))) & ((rC // size) % 2 == 1) & ((cC // size) % 2 == 0))
        size *= 2

    def tile(t, carry):
        rows = pl.ds(pl.multiple_of(t * R, R), R)
        q = q_ref[rows, :]
        k = k_ref[rows, :]
        v = v_ref[rows, :]
        q = q * lax.rsqrt(_dot((q * q).astype(BF16), ones_bd) + EPS) * (GDN_DK ** -0.5)
        k = k * lax.rsqrt(_dot((k * k).astype(BF16), ones_bd) + EPS)
        bg = _dot_exact_lhs(lat_ref[rows, :], sel)
        beta = bg[:, :LANES]
        gc = bg[:, LANES:]
        eg = jnp.exp(gc)
        kb = k * beta
        vb16 = (v * beta).astype(BF16)
        kbg16 = (kb * eg).astype(BF16)
        qd_ref[rows, :] = (q * eg).astype(BF16)
        k16 = k.astype(BF16)
        q16 = q.astype(BF16)

        ch = lambda arr, g: arr[g * C:(g + 1) * C, :]
        rng = range(G)
        kkqk = [_dot_nt(jnp.concatenate([ch(k16, g), ch(q16, g)], axis=0), bd16(ch(k16, g))) for g in rng]
        g_row = [_dot_exact_rhs(ones_cc, jnp.where(eye2, ch(gc, g), 0.0)) for g in rng]
        dm = [jnp.where(tril2, jnp.exp(jnp.where(tril2, ch(gc, g) - g_row[g], 0.0)), 0.0) for g in rng]
        for g in rng:
            a_ref[pl.ds(pl.multiple_of(t * R + g * C, C), C), :] = (kkqk[g][C:] * dm[g]).astype(BF16)
            g_last = ch(gc, g)[C - 1:C, :]
            kd_ref[pl.ds(pl.multiple_of(t * R + g * C, C), C), :] = (
                ch(k, g) * jnp.exp(g_last - ch(gc, g))).astype(BF16)
            sd_ref[pl.ds(pl.multiple_of((t * G + g) * SUBLANES, SUBLANES), SUBLANES), :] = jnp.broadcast_to(
                jnp.exp(g_last), (SUBLANES, LANES))
        low = [jnp.where(strict2, kkqk[g][:C] * dm[g] * ch(beta, g), 0.0) for g in rng]
        x = [jnp.where(diag_blk, -low[g], 0.0) for g in rng]
        p = [eye2f + x[g] for g in rng]
        y = [dot_hi(x[g], *bd_parts(x[g])) for g in rng]
        rhs = []
        for g in rng:
            p_hi, p_lo = bd_parts(p[g])
            y_hi, y_lo = bd_parts(y[g])
            rhs.append((jnp.concatenate([p_hi, y_hi], axis=1), jnp.concatenate([p_lo, y_lo], axis=1)))
        res = [dot_hi(y[g], *rhs[g]) for g in rng]
        p = [p[g] + res[g][:, :LANES] for g in rng]
        y = [res[g][:, LANES:] for g in rng]
        p = [p[g] + dot_hi(y[g], *bd_parts(p[g])) for g in rng]
        for off_blk in off_blks:
            m1 = [merge_dot(jnp.where(off_blk, low[g], 0.0), p[g]) for g in rng]
            p = [p[g] - merge_dot(p[g], m1[g]) for g in rng]
        uw = [_dot(p[g].astype(BF16), jnp.concatenate([bd16(ch(vb16, g)), bd16(ch(kbg16, g))], axis=1))
              for g in rng]
        for g in rng:
            dst = pl.ds(pl.multiple_of(t * R + g * C, C), C)
            u_ref[dst, :] = uw[g][:, :LANES]
            w_ref[dst, :] = uw[g][:, LANES:].astype(BF16)
        return carry

    lax.fori_loop(0, S // R, tile, 0)


def _gdn_prep(qkv3, lat3):
    B, S, _ = qkv3.shape
    HP = HEAD_PAIRS
    seq_blk = lambda off: pl.BlockSpec((None, S, LANES), lambda b, h, off=off: (b, 0, off + h))
    n_sd = S // GDN_CHUNK * SUBLANES
    return pl.pallas_call(
        _gdn_prep_kernel,
        grid=(B, HP),
        in_specs=[
            seq_blk(0), seq_blk(HP), seq_blk(2 * HP),
            pl.BlockSpec((None, S, LANES), lambda b, h: (b, 0, LAT_W // LANES - 1)),
        ],
        out_specs=[seq_blk(0)] * 5 + [pl.BlockSpec((None, n_sd, LANES), lambda b, h: (b, 0, h))],
        out_shape=[
            jax.ShapeDtypeStruct((B, S, GDN_V_W), F32),
            jax.ShapeDtypeStruct((B, S, GDN_V_W), BF16),
            jax.ShapeDtypeStruct((B, S, GDN_V_W), BF16),
            jax.ShapeDtypeStruct((B, S, GDN_V_W), BF16),
            jax.ShapeDtypeStruct((B, S, GDN_V_W), BF16),
            jax.ShapeDtypeStruct((B, n_sd, GDN_V_W), F32),
        ],
        compiler_params=pltpu.CompilerParams(
            dimension_semantics=("arbitrary", "arbitrary"), vmem_limit_bytes=VMEM_LIMIT),
        name="gdn_prep",
    )(qkv3, qkv3, qkv3, lat3)


def _gdn_scan_kernel(u_ref, w_ref, qd_ref, kd_ref, a_ref, sd_ref, z_ref, nw_ref, o_ref, state_ref):
    C = GDN_CHUNK
    nb, rt, _ = u_ref.shape

    @pl.when(pl.program_id(1) == 0)
    def _():
        state_ref[...] = jnp.zeros_like(state_ref)

    lane = lax.broadcasted_iota(jnp.int32, (1, LANES), 1)
    head1 = lane >= C
    m0 = jnp.where(head1, 0.0, 1.0).astype(BF16)
    m1 = jnp.where(head1, 1.0, 0.0).astype(BF16)
    r128 = lax.broadcasted_iota(jnp.int32, (LANES, LANES), 0)
    c128 = lax.broadcasted_iota(jnp.int32, (LANES, LANES), 1)
    bdmask = (r128 // C) == (c128 // C)
    ones_bd = jnp.where(bdmask, 1.0, 0.0).astype(BF16)

    chains = [(bb, hp) for bb in range(nb) for hp in range(HEAD_PAIRS)]
    blk = lambda hp: slice(hp * LANES, (hp + 1) * LANES)

    def bd16(m):
        return jnp.concatenate([m * m0, m * m1], axis=0)

    def chunk(n, carry):
        rows = pl.ds(pl.multiple_of(n * C, C), C)
        sd_rows = pl.ds(pl.multiple_of(n * SUBLANES, SUBLANES), SUBLANES)
        state = [state_ref[i] for i in range(len(chains))]
        res = [_dot(jnp.concatenate([w_ref[bb, rows, blk(hp)], qd_ref[bb, rows, blk(hp)]], axis=0),
                    state[i].astype(BF16)) for i, (bb, hp) in enumerate(chains)]
        v_new = [(u_ref[bb, rows, blk(hp)] - res[i][:C]).astype(BF16) for i, (bb, hp) in enumerate(chains)]
        upd = [_dot_tn(kd_ref[bb, rows, blk(hp)], v_new[i]) for i, (bb, hp) in enumerate(chains)]
        for i, (bb, hp) in enumerate(chains):
            sd = sd_ref[bb, sd_rows, blk(hp)][0:1, :]
            state_ref[i] = state[i] * sd + jnp.where(bdmask, upd[i], 0.0)
        o = [res[i][C:] + _dot(a_ref[bb, rows, blk(hp)], bd16(v_new[i])) for i, (bb, hp) in enumerate(chains)]
        for i, (bb, hp) in enumerate(chains):
            ms = _dot((o[i] * o[i]).astype(BF16), ones_bd) * (1.0 / GDN_DV)
            y = o[i] * lax.rsqrt(ms + EPS) * nw_ref[:, blk(hp)] * z_ref[bb, rows, blk(hp)].astype(F32)
            o_ref[bb, rows, blk(hp)] = y.astype(o_ref.dtype)
        return carry

    lax.fori_loop(0, rt // C, chunk, 0)


def _gdn_scan(u, w, qd, kd, a, sd, zg, norm_w8):
    B, S, W = u.shape
    rt = min(SCAN_ROWS, S)
    nb = min(SCAN_BATCH, B)
    assert B % nb == 0 and S % rt == 0
    n_sd = rt // GDN_CHUNK * SUBLANES
    seq = pl.BlockSpec((nb, rt, W), lambda b, t: (b, t, 0))
    return pl.pallas_call(
        _gdn_scan_kernel,
        grid=(B // nb, S // rt),
        in_specs=[seq, seq, seq, seq, seq,
                  pl.BlockSpec((nb, n_sd, W), lambda b, t: (b, t, 0)),
                  seq,
                  pl.BlockSpec((1, W), lambda b, t: (0, 0))],
        out_specs=seq,
        out_shape=jax.ShapeDtypeStruct((B, S, W), BF16),
        scratch_shapes=[pltpu.VMEM((nb * HEAD_PAIRS, LANES, LANES), F32)],
        compiler_params=pltpu.CompilerParams(
            dimension_semantics=("arbitrary", "arbitrary"), vmem_limit_bytes=VMEM_LIMIT),
        name="gdn_scan",
    )(u, w, qd, kd, a, sd, zg, norm_w8)


def _mla_prep_kernel(pos_ref, freq_ref, lat_ref, qnw_ref, kvnw_ref, wq_ref, wqs_ref, wk_ref, wv_ref,
                     q_ref, k_ref, v_ref):
    half = MLA_ROPE // 2
    lo, mid, hi = MLA_NOPE, MLA_NOPE + half, MLA_NOPE + MLA_ROPE
    scale = (MLA_NOPE + MLA_ROPE) ** -0.5 * LOG2E
    ang = freq_ref[...] * pos_ref[...].astype(F32)
    f_i = lax.broadcasted_iota(jnp.int32, (3 * half, LANES), 0) % half
    l_i = lax.broadcasted_iota(jnp.int32, (3 * half, LANES), 1)
    in_lo = (l_i >= lo) & (l_i < mid) & (l_i - lo == f_i)
    in_hi = (l_i >= mid) & (l_i < hi) & (l_i - mid == f_i)
    expand_cos = jnp.where(in_lo | in_hi, 1.0, 0.0).astype(BF16)
    expand_sin = jnp.where(in_lo, -1.0, jnp.where(in_hi, 1.0, 0.0)).astype(BF16)
    cos_tab = _dot_tn(jnp.concatenate(_split3(jnp.cos(ang)), axis=0), expand_cos)
    sin_tab = _dot_tn(jnp.concatenate(_split3(jnp.sin(ang)), axis=0), expand_sin)
    lane = lax.broadcasted_iota(jnp.int32, (1, LANES), 1)
    nope = jnp.where(lane < MLA_NOPE, 1.0, 0.0)

    lat = lat_ref[...]
    cq = _rms(lat[:, 0:MLA_Q_RANK], qnw_ref[...]).astype(BF16)
    ckv = _rms(lat[:, MLA_Q_RANK:MLA_Q_RANK + MLA_KV_RANK], kvnw_ref[...]).astype(BF16)
    q = _dot(cq, wq_ref[...])
    q_sw = _dot(cq, wqs_ref[...])
    k_nope = _dot(ckv, wk_ref[...])
    row_all = lax.broadcasted_iota(jnp.int32, (MLA_HEADS * LANES, 1), 0)
    ones_row = jnp.where(row_all % LANES == MLA_V, 1.0, 0.0)
    v_ref[...] = (_dot_nt(wv_ref[...], ckv) + ones_row).astype(v_ref.dtype)

    kpe = pltpu.roll(lat[:, LAT_W - LANES:], MLA_NOPE - KPE_OFF, axis=1)
    kpe_sw = jnp.where(lane < mid, pltpu.roll(kpe, LANES - half, axis=1), pltpu.roll(kpe, half, axis=1))
    k_rope = kpe * cos_tab + kpe_sw * sin_tab
    c_q = (cos_tab + nope) * scale
    s_q = sin_tab * scale
    for h in range(MLA_HEADS):
        blk = slice(h * LANES, (h + 1) * LANES)
        q_ref[:, blk] = (q[:, blk] * c_q + q_sw[:, blk] * s_q).astype(q_ref.dtype)
        k_ref[:, blk] = (k_nope[:, blk] + k_rope).astype(k_ref.dtype)


def _mla_prep(pos_row, inv_freq, lat, q_norm_w, kv_norm_w, wq_p, wqs_p, wk_p, wv_p):
    T = lat.shape[0]
    tm = min(TOK_TILE, T)
    full = lambda a: pl.BlockSpec(a.shape, lambda i: (0,) * a.ndim)
    return pl.pallas_call(
        _mla_prep_kernel,
        grid=(T // tm,),
        in_specs=[
            pl.BlockSpec((1, tm), lambda i: (0, i)),
            full(inv_freq),
            pl.BlockSpec((tm, LAT_W), lambda i: (i, 0)),
            full(q_norm_w), full(kv_norm_w), full(wq_p), full(wqs_p), full(wk_p), full(wv_p),
        ],
        out_specs=[
            pl.BlockSpec((tm, MLA_HEADS * LANES), lambda i: (i, 0)),
            pl.BlockSpec((tm, MLA_HEADS * LANES), lambda i: (i, 0)),
            pl.BlockSpec((MLA_HEADS * LANES, tm), lambda i: (0, i)),
        ],
        out_shape=[
            jax.ShapeDtypeStruct((T, MLA_HEADS * LANES), BF16),
            jax.ShapeDtypeStruct((T, MLA_HEADS * LANES), BF16),
            jax.ShapeDtypeStruct((MLA_HEADS * LANES, T), BF16),
        ],
        compiler_params=pltpu.CompilerParams(
            dimension_semantics=("arbitrary",), vmem_limit_bytes=VMEM_LIMIT),
        name="mla_prep",
    )(pos_row, inv_freq, lat, q_norm_w, kv_norm_w, wq_p, wqs_p, wk_p, wv_p)


def _attn_kernel(q_ref, k_ref, vt_ref, o_ref, s_ref, m_ref, acc_ref, *, tq):
    seq = q_ref.shape[0]
    nq = seq // tq
    nh = q_ref.shape[1] // LANES
    lane = lax.broadcasted_iota(jnp.int32, (1, LANES), 1)
    causal = lax.broadcasted_iota(jnp.int32, (tq, tq), 0) <= lax.broadcasted_iota(jnp.int32, (tq, tq), 1)
    blks = [slice(h * LANES, (h + 1) * LANES) for h in range(nh)]
    rows = lambda i: pl.ds(pl.multiple_of(i * tq, tq), tq)

    def scores(qi, j, h):
        s_ref[h] = _dot_nt(k_ref[rows(j), blks[h]], q_ref[rows(qi), blks[h]])

    def consume(j, h, masked):
        m_prev = m_ref[h][0:1, :]
        s = s_ref[h]
        if masked:
            s = jnp.where(causal, s, NEG_BIG)
        m_new = jnp.maximum(m_prev, jnp.max(s, axis=0, keepdims=True))
        alpha = jnp.exp2(m_prev - m_new)
        p = jnp.exp2(s - m_new).astype(BF16)
        m_ref[h] = jnp.broadcast_to(m_new, (SUBLANES, tq))
        acc_ref[h] = acc_ref[h] * alpha + _dot(vt_ref[blks[h], rows(j)], p)

    def consume_and_prefetch(qi, j, h, masked):
        consume(j, h, masked)
        nxt = h + ATT_AHEAD
        if nxt < nh:
            scores(qi, j, nxt)
        elif not masked:
            scores(qi, j + 1, nxt - nh)
        else:
            scores(jnp.minimum(qi + 1, nq - 1), 0, nxt - nh)

    def q_block(qi, carry):
        def blocks(j0, count):
            for j in range(count):
                for h in range(nh):
                    consume_and_prefetch(qi, j0 + j, h, False)

        def unrolled(i, c):
            blocks(ATT_UNROLL * i, ATT_UNROLL)
            return c

        m_ref[...] = jnp.full(m_ref.shape, NEG_BIG, F32)
        acc_ref[...] = jnp.zeros(acc_ref.shape, F32)
        n_full = qi
        lax.fori_loop(0, n_full // ATT_UNROLL, unrolled, 0)
        done = n_full - n_full % ATT_UNROLL
        size = ATT_UNROLL // 2
        while size >= 1:
            take = (n_full % (2 * size)) >= size

            @pl.when(take)
            def _(done=done, size=size):
                blocks(done, size)

            done = done + jnp.where(take, size, 0)
            size //= 2
        for h in range(nh):
            consume_and_prefetch(qi, n_full, h, True)
        outs = []
        for h in range(nh):
            acc = acc_ref[h]
            outs.append((acc * (1.0 / acc[MLA_V:MLA_V + 1, :])).T)
        for pr in range(nh // 2):
            pair = jnp.where(lane < MLA_V, outs[2 * pr], pltpu.roll(outs[2 * pr + 1], MLA_V, axis=1))
            o_ref[rows(qi), pr * LANES:(pr + 1) * LANES] = pair.astype(o_ref.dtype)
        return carry

    for h in range(ATT_AHEAD):
        scores(0, 0, h)
    lax.fori_loop(0, nq, q_block, 0)


def _attn(q3, k3, vt):
    B, S, _ = q3.shape
    tq = min(ATT_BLOCK, S)
    assert S % tq == 0
    nh = ATT_HEADS
    return pl.pallas_call(
        functools.partial(_attn_kernel, tq=tq),
        grid=(B, MLA_HEADS // nh),
        in_specs=[
            pl.BlockSpec((None, S, nh * LANES), lambda b, h: (b, 0, h)),
            pl.BlockSpec((None, S, nh * LANES), lambda b, h: (b, 0, h)),
            pl.BlockSpec((nh * LANES, S), lambda b, h: (h, b)),
        ],
        out_specs=pl.BlockSpec((None, S, nh * MLA_V), lambda b, h: (b, 0, h)),
        out_shape=jax.ShapeDtypeStruct((B, S, MLA_W), BF16),
        scratch_shapes=[pltpu.VMEM((nh, tq, tq), F32),
                        pltpu.VMEM((nh, SUBLANES, tq), F32),
                        pltpu.VMEM((nh, LANES, tq), F32)],
        compiler_params=pltpu.CompilerParams(
            dimension_semantics=("arbitrary", "arbitrary"), vmem_limit_bytes=VMEM_LIMIT),
        name="attn",
    )(q3, k3, vt)


def _tail_kernel(x_ref, yg_ref, om_ref, p_ref, monw_ref, wout_ref, mlpnw_ref, wup_ref, wdown_ref,
                 wpp_ref, postnw_ref, gatenw_ref, wpg_ref, finnw_ref, o_ref):
    ymla = _rms(om_ref[...].astype(F32), monw_ref[...]).astype(BF16)
    h = x_ref[...] + _dot(yg_ref[...], wout_ref[0:GDN_V_W, :]) + _dot(ymla, wout_ref[GDN_V_W:, :])
    u = _rms(h, mlpnw_ref[...]).astype(BF16)
    ff_blk = 1024
    acc = jnp.zeros_like(h)
    for j in range(D_FF // ff_blk):
        cols = slice(j * ff_blk, (j + 1) * ff_blk)
        hid = jnp.maximum(_dot(u, wup_ref[:, cols]), 0.0)
        acc = acc + _dot((hid * hid).astype(BF16), wdown_ref[cols, :])
    h = h + acc
    e = _rms(_dot(p_ref[...].astype(BF16), wpp_ref[...]), postnw_ref[...])
    gate = _sigmoid(_dot(_rms(h, gatenw_ref[...]).astype(BF16), wpg_ref[...]))
    h = h + gate * e
    o_ref[...] = _rms(h, finnw_ref[...])


def _tail(x2, yg, om, p2, mla_out_norm_w, w_out, mlp_norm_w, w_up, w_down, w_ple_proj,
          ple_post_norm_w, ple_gate_norm_w, w_ple_gate, final_norm_w):
    T = x2.shape[0]
    tm = min(TOK_TILE, T)
    tok = lambda w: pl.BlockSpec((tm, w), lambda i: (i, 0))
    const = lambda a: pl.BlockSpec(a.shape, lambda i: (0, 0), pipeline_mode=pl.Buffered(1))
    return pl.pallas_call(
        _tail_kernel,
        grid=(T // tm,),
        in_specs=[
            tok(D_MODEL), tok(GDN_V_W), tok(MLA_W), tok(PLE_DIM),
            const(mla_out_norm_w), const(w_out), const(mlp_norm_w), const(w_up), const(w_down),
            const(w_ple_proj), const(ple_post_norm_w), const(ple_gate_norm_w), const(w_ple_gate),
            const(final_norm_w),
        ],
        out_specs=tok(D_MODEL),
        out_shape=jax.ShapeDtypeStruct((T, D_MODEL), F32),
        compiler_params=pltpu.CompilerParams(
            dimension_semantics=("arbitrary",), vmem_limit_bytes=VMEM_LIMIT),
        name="tail",
    )(x2, yg, om, p2, mla_out_norm_w, w_out, mlp_norm_w, w_up, w_down, w_ple_proj,
      ple_post_norm_w, ple_gate_norm_w, w_ple_gate, final_norm_w)


def _pack_w_in(w):
    o_b = QKV_W + GDN_V_W
    o_cq = o_b + 2 * GDN_HEADS
    n_lat = MLA_Q_RANK + MLA_KV_RANK + MLA_ROPE
    w = w.astype(BF16)
    pad = jnp.zeros((D_MODEL, LAT_W - n_lat - 2 * GDN_HEADS), BF16)
    return jnp.concatenate([w[:, :o_b], w[:, o_cq:o_cq + n_lat], w[:, o_b:o_cq], pad], axis=1)


def _gate_lane_vector(per_head):
    return jnp.zeros((1, LANES), F32).at[0, A_OFF:A_OFF + GDN_HEADS].set(per_head.astype(F32))


def _pack_mla_weights(w_q_b, w_kv_b):
    wq = w_q_b.reshape(MLA_Q_RANK, MLA_HEADS, MLA_NOPE + MLA_ROPE)
    wq = jnp.pad(wq, ((0, 0), (0, 0), (0, LANES - MLA_NOPE - MLA_ROPE)))
    wkv = w_kv_b.reshape(MLA_KV_RANK, MLA_HEADS, MLA_NOPE + MLA_V)
    wk = jnp.pad(wkv[:, :, :MLA_NOPE], ((0, 0), (0, 0), (0, LANES - MLA_NOPE)))
    wv = jnp.pad(wkv[:, :, MLA_NOPE:], ((0, 0), (0, 0), (0, LANES - MLA_V)))
    half = MLA_ROPE // 2
    zeros = lambda n: jnp.zeros((MLA_Q_RANK, MLA_HEADS, n), wq.dtype)
    wq_sw = jnp.concatenate([zeros(MLA_NOPE), wq[:, :, MLA_NOPE + half:MLA_NOPE + MLA_ROPE],
                             wq[:, :, MLA_NOPE:MLA_NOPE + half], zeros(LANES - MLA_NOPE - MLA_ROPE)], axis=2)
    return (wq.reshape(MLA_Q_RANK, MLA_HEADS * LANES).astype(BF16),
            wq_sw.reshape(MLA_Q_RANK, MLA_HEADS * LANES).astype(BF16),
            wk.reshape(MLA_KV_RANK, MLA_HEADS * LANES).astype(BF16),
            wv.reshape(MLA_KV_RANK, MLA_HEADS * LANES).T.astype(BF16))


def kernel(x, p, positions, mix_norm_w, w_in, conv_w, A_log, dt_bias, gdn_norm_w, q_norm_w, w_q_b,
           kv_norm_w, w_kv_b, mla_out_norm_w, w_out, mlp_norm_w, w_up, w_down, w_ple_proj,
           ple_post_norm_w, ple_gate_norm_w, w_ple_gate, final_norm_w):
    B, S, _ = x.shape
    T = B * S
    assert w_in.shape[0] == 1, "one layer"
    row = lambda a: a.reshape(1, -1).astype(F32)
    x2 = x.reshape(T, D_MODEL)

    qkv, zg, lat = _inproj(x2, row(mix_norm_w[0]), _pack_w_in(w_in[0]), conv_w[0].astype(F32),
                           _gate_lane_vector(A_log[0]), _gate_lane_vector(dt_bias[0]), S)

    factors = _gdn_prep(qkv.reshape(B, S, QKV_W), lat.reshape(B, S, LAT_W))
    y_gdn = _gdn_scan(*factors, zg.reshape(B, S, GDN_V_W), row(jnp.tile(gdn_norm_w[0], GDN_HEADS)))

    half = MLA_ROPE // 2
    inv_freq = (ROPE_THETA ** (-jnp.arange(0, MLA_ROPE, 2, dtype=F32) / MLA_ROPE)).reshape(half, 1)
    q_att, k_att, v_att = _mla_prep(positions.reshape(1, T), inv_freq, lat, row(q_norm_w[0]),
                                    row(kv_norm_w[0]), *_pack_mla_weights(w_q_b[0], w_kv_b[0]))
    o_mla = _attn(q_att.reshape(B, S, -1), k_att.reshape(B, S, -1), v_att)

    out = _tail(x2, y_gdn.reshape(T, GDN_V_W), o_mla.reshape(T, MLA_W), p[0].reshape(T, PLE_DIM),
                row(mla_out_norm_w[0]), w_out[0].astype(BF16), row(mlp_norm_w[0]),
                w_up[0].astype(BF16), w_down[0].astype(BF16), w_ple_proj[0].astype(BF16),
                row(ple_post_norm_w[0]), row(ple_gate_norm_w[0]), w_ple_gate[0].astype(BF16),
                row(final_norm_w))
    return out.reshape(B, S, D_MODEL)
```

```python
import functools

import jax
import jax.numpy as jnp
from jax import lax
from jax.experimental import pallas as pl
from jax.experimental.pallas import tpu as pltpu

F32 = jnp.float32
BF16 = jnp.bfloat16

D_MODEL = 1024
PLE_DIM = 256
GDN_HEADS = 8
GDN_DK = 64
GDN_DV = 64
GDN_QK_W = GDN_HEADS * GDN_DK
GDN_V_W = GDN_HEADS * GDN_DV
GDN_CONV = 4
GDN_CHUNK = 64
MLA_HEADS = 8
MLA_NOPE = 64
MLA_ROPE = 32
MLA_V = 64
MLA_W = MLA_HEADS * MLA_V
MLA_Q_RANK = 256
MLA_KV_RANK = 128
ROPE_THETA = 10000.0
D_FF = 4 * D_MODEL
EPS = 1e-6

LANES = 128
SUBLANES = 8
HEAD_PAIRS = GDN_HEADS // 2
QKV_W = 2 * GDN_QK_W + GDN_V_W
LAT_W = 512
IN_W_PAD = QKV_W + GDN_V_W + LAT_W
KPE_OFF, B_OFF, A_OFF = 0, MLA_ROPE, MLA_ROPE + GDN_HEADS
VMEM_LIMIT = 56 * 1024 * 1024

TOK_TILE = 512
CUM_ROWS = 256
CONV_COLS = 512
GDN_ROWS = 1024
INV_BASE = 8
SCAN_ROWS = 512
SCAN_BATCH = 4
ATT_BLOCK = 512
ATT_HEADS = 4
ATT_UNROLL = 4
ATT_AHEAD = 3
NEG_BIG = -1e30
LOG2E = 1.4426950408889634


def _dot(a, b):
    return jnp.dot(a, b, preferred_element_type=F32)


def _dot_nt(a, b):
    return lax.dot_general(a, b, (((1,), (1,)), ((), ())), preferred_element_type=F32)


def _dot_tn(a, b):
    return lax.dot_general(a, b, (((0,), (0,)), ((), ())), preferred_element_type=F32)


def _split2(x):
    hi = x.astype(BF16)
    lo = (x - hi.astype(F32)).astype(BF16)
    return hi, lo


def _split3(x):
    hi = x.astype(BF16)
    r1 = x - hi.astype(F32)
    mid = r1.astype(BF16)
    lo = (r1 - mid.astype(F32)).astype(BF16)
    return hi, mid, lo


def _dot_exact_rhs(a_bf16, x):
    return _dot(jnp.concatenate([a_bf16] * 3, axis=1), jnp.concatenate(_split3(x), axis=0))


def _dot_exact_lhs(x, b_bf16):
    return _dot(jnp.concatenate(_split3(x), axis=1), jnp.concatenate([b_bf16] * 3, axis=0))


def _rms(x, w):
    ms = jnp.mean(x * x, axis=-1, keepdims=True)
    return x * lax.rsqrt(ms + EPS) * w


def _sigmoid(x):
    return 1.0 / (1.0 + jnp.exp(-x))


def _silu(x):
    h = 0.5 * x
    return h + h * jnp.tanh(h)


def _softplus(x):
    return jnp.maximum(x, 0.0) + jnp.log(1.0 + jnp.exp(-jnp.abs(x)))


def _chunk_tril(n):
    r = lax.broadcasted_iota(jnp.int32, (n, n), 0)
    c = lax.broadcasted_iota(jnp.int32, (n, n), 1)
    return jnp.where(((r // GDN_CHUNK) == (c // GDN_CHUNK)) & (c <= r), 1.0, 0.0).astype(BF16)


def _inproj_kernel(x_ref, nw_ref, w_ref, cw_ref, alog_ref, dtb_ref, qkv_ref, z_ref, lat_ref, halo_ref,
                   *, tiles_per_seq):
    tm = x_ref.shape[0]
    first = (pl.program_id(0) % tiles_per_seq) == 0
    u = _rms(x_ref[...], nw_ref[...]).astype(BF16)

    def conv_chunk(c):
        cols = slice(c * CONV_COLS, (c + 1) * CONV_COLS)
        raw = _dot(u, w_ref[:, cols])
        prev = jnp.where(first, 0.0, halo_ref[:, cols])
        halo_ref[:, cols] = raw[tm - SUBLANES:, :]
        ext = jnp.concatenate([prev, raw], axis=0)
        cw = cw_ref[:, cols]
        acc = raw * cw[GDN_CONV - 1:GDN_CONV, :]
        for kk in range(GDN_CONV - 1):
            shifted = pltpu.roll(ext, GDN_CONV - 1 - kk, axis=0)[SUBLANES:, :]
            acc = acc + shifted * cw[kk:kk + 1, :]
        qkv_ref[:, cols] = _silu(acc)

    def gate_z():
        zf = _dot(u, w_ref[:, QKV_W:QKV_W + GDN_V_W])
        z_ref[...] = _silu(zf).astype(z_ref.dtype)

    def latent_head():
        lat_ref[:, :LAT_W - LANES] = _dot(u, w_ref[:, QKV_W + GDN_V_W:IN_W_PAD - LANES])

    def latent_gates():
        l3 = _dot(u, w_ref[:, IN_W_PAD - LANES:IN_W_PAD])
        lane = lax.broadcasted_iota(jnp.int32, (1, LANES), 1)
        is_b = (lane >= B_OFF) & (lane < B_OFF + GDN_HEADS)
        is_a = (lane >= A_OFF) & (lane < A_OFF + GDN_HEADS)
        g_raw = jnp.where(is_a, -(jnp.exp(alog_ref[...]) * _softplus(l3 + dtb_ref[...])), 0.0)
        hi, mid, lo = (t.astype(F32) for t in _split3(g_raw))
        packed = (hi + pltpu.roll(mid, GDN_HEADS, axis=1) + pltpu.roll(lo, 2 * GDN_HEADS, axis=1)).astype(BF16)
        tril = _chunk_tril(CUM_ROWS)
        cum = jnp.concatenate([_dot(tril, packed[r:r + CUM_ROWS, :]) for r in range(0, tm, CUM_ROWS)], axis=0)
        g_cum = cum + pltpu.roll(cum, LANES - GDN_HEADS, axis=1) + pltpu.roll(cum, LANES - 2 * GDN_HEADS, axis=1)
        lat_ref[:, LAT_W - LANES:] = jnp.where(is_b, _sigmoid(l3), jnp.where(is_a, g_cum, l3))

    fillers = [latent_gates, gate_z, latent_head]
    for c in range(QKV_W // CONV_COLS):
        if fillers:
            fillers.pop(0)()
        conv_chunk(c)
    for f in fillers:
        f()


def _inproj(x2, mix_norm_w, w_in_p, conv_w, alog_vec, dtb_vec, seq_len):
    T = x2.shape[0]
    tm = min(TOK_TILE, seq_len)
    assert seq_len % tm == 0 and tm % GDN_CHUNK == 0
    nt = T // tm
    const = lambda a: pl.BlockSpec(a.shape, lambda i: (0, 0))
    return pl.pallas_call(
        functools.partial(_inproj_kernel, tiles_per_seq=seq_len // tm),
        grid=(nt,),
        in_specs=[
            pl.BlockSpec((tm, D_MODEL), lambda i: (i, 0)),
            const(mix_norm_w), const(w_in_p), const(conv_w), const(alog_vec), const(dtb_vec),
        ],
        out_specs=[
            pl.BlockSpec((tm, QKV_W), lambda i: (i, 0)),
            pl.BlockSpec((tm, GDN_V_W), lambda i: (i, 0)),
            pl.BlockSpec((tm, LAT_W), lambda i: (i, 0)),
        ],
        out_shape=[
            jax.ShapeDtypeStruct((T, QKV_W), F32),
            jax.ShapeDtypeStruct((T, GDN_V_W), BF16),
            jax.ShapeDtypeStruct((T, LAT_W), F32),
        ],
        scratch_shapes=[pltpu.VMEM((SUBLANES, QKV_W), F32)],
        compiler_params=pltpu.CompilerParams(
            dimension_semantics=("arbitrary",), vmem_limit_bytes=VMEM_LIMIT),
        name="inproj",
    )(x2, mix_norm_w, w_in_p, conv_w, alog_vec, dtb_vec)


def _gdn_prep_kernel(q_ref, k_ref, v_ref, lat_ref, u_ref, w_ref, qd_ref, kd_ref, a_ref, sd_ref):
    C = GDN_CHUNK
    S = q_ref.shape[0]
    R = min(GDN_ROWS, S)
    G = R // C
    hp = pl.program_id(1)

    lane = lax.broadcasted_iota(jnp.int32, (1, LANES), 1)
    head1 = lane >= C
    m0 = jnp.where(head1, 0.0, 1.0).astype(BF16)
    m1 = jnp.where(head1, 1.0, 0.0).astype(BF16)
    r128 = lax.broadcasted_iota(jnp.int32, (LANES, LANES), 0)
    c128 = lax.broadcasted_iota(jnp.int32, (LANES, LANES), 1)
    ones_bd = jnp.where((r128 // C) == (c128 // C), 1.0, 0.0).astype(BF16)
    rC = lax.broadcasted_iota(jnp.int32, (C, LANES), 0)
    cC = lax.broadcasted_iota(jnp.int32, (C, LANES), 1) % C
    eye2 = rC == cC
    tril2 = rC >= cC
    strict2 = rC > cC
    eye2f = jnp.where(eye2, 1.0, 0.0)
    ones_cc = jnp.ones((C, C), BF16)
    sel_r = lax.broadcasted_iota(jnp.int32, (LANES, 2 * LANES), 0)
    sel_c = lax.broadcasted_iota(jnp.int32, (LANES, 2 * LANES), 1)
    sel_src = jnp.where(sel_c < LANES, B_OFF, A_OFF) + 2 * hp + ((sel_c % LANES) // C)
    sel = jnp.where(sel_r == sel_src, 1.0, 0.0).astype(BF16)

    def bd16(m):
        return jnp.concatenate([m * m0, m * m1], axis=0)

    def bd_parts(m):
        hi, lo = _split2(m)
        return bd16(hi), bd16(lo)

    def dot_hi(a, b_hi, b_lo):
        a_hi, a_lo = _split2(a)
        return _dot(jnp.concatenate([a_hi, a_hi, a_lo], axis=1), jnp.concatenate([b_hi, b_lo, b_hi], axis=0))

    def merge_dot(a, b):
        return _dot(a.astype(BF16), bd16(b.astype(BF16)))

    diag_blk = (rC // INV_BASE) == (cC // INV_BASE)
    off_blks = []
    size = INV_BASE
    while size < C:
        off_blks.append(((rC // (2 * size)) == (cC // (2 * size))) & ((rC // size) % 2 == 1) & ((cC // size) % 2 == 0))
        size *= 2

    ch = lambda arr, g: arr[g * C:(g + 1) * C, :]
    rng = range(G)

    def front(t):
        rows = pl.ds(t * R, R)
        q = q_ref[rows, :]
        k = k_ref[rows, :]
        v = v_ref[rows, :]
        q = q * lax.rsqrt(_dot((q * q).astype(BF16), ones_bd) + EPS) * (GDN_DK ** -0.5)
        k = k * lax.rsqrt(_dot((k * k).astype(BF16), ones_bd) + EPS)
        bg = _dot_exact_lhs(lat_ref[rows, :], sel)
        beta = bg[:, :LANES]
        gc = bg[:, LANES:]
        eg = jnp.exp(gc)
        kb = k * beta
        vb16 = (v * beta).astype(BF16)
        kbg16 = (kb * eg).astype(BF16)
        qd_ref[rows, :] = (q * eg).astype(BF16)
        k16 = k.astype(BF16)
        q16 = q.astype(BF16)

        yield
        kkqk = [_dot_nt(jnp.concatenate([ch(k16, g), ch(q16, g)], axis=0), bd16(ch(k16, g))) for g in rng]
        g_row = [_dot_exact_rhs(ones_cc, jnp.where(eye2, ch(gc, g), 0.0)) for g in rng]
        dm = [jnp.where(tril2, jnp.exp(jnp.where(tril2, ch(gc, g) - g_row[g], 0.0)), 0.0) for g in rng]
        yield
        for g in rng:
            a_ref[pl.ds(t * R + g * C, C), :] = (kkqk[g][C:] * dm[g]).astype(BF16)
            g_last = ch(gc, g)[C - 1:C, :]
            kd_ref[pl.ds(t * R + g * C, C), :] = (ch(k, g) * jnp.exp(g_last - ch(gc, g))).astype(BF16)
            sd_ref[pl.ds((t * G + g) * SUBLANES, SUBLANES), :] = jnp.broadcast_to(
                jnp.exp(g_last), (SUBLANES, LANES))
        yield
        low = [jnp.where(strict2, kkqk[g][:C] * dm[g] * ch(beta, g), 0.0) for g in rng]
        x = [jnp.where(diag_blk, -low[g], 0.0) for g in rng]
        p = [eye2f + x[g] for g in rng]
        y = [dot_hi(x[g], *bd_parts(x[g])) for g in rng]
        yield
        rhs = []
        for g in rng:
            p_hi, p_lo = bd_parts(p[g])
            y_hi, y_lo = bd_parts(y[g])
            rhs.append((jnp.concatenate([p_hi, y_hi], axis=1), jnp.concatenate([p_lo, y_lo], axis=1)))
        res = [dot_hi(y[g], *rhs[g]) for g in rng]
        p = [p[g] + res[g][:, :LANES] for g in rng]
        y = [res[g][:, LANES:] for g in rng]
        yield
        p = [p[g] + dot_hi(y[g], *bd_parts(p[g])) for g in rng]
        handoff[t] = (low, p, vb16, kbg16)

    def back(t):
        low, p, vb16, kbg16 = handoff[t]
        for off_blk in off_blks:
            m1 = [merge_dot(jnp.where(off_blk, low[g], 0.0), p[g]) for g in rng]
            yield
            p = [p[g] - merge_dot(p[g], m1[g]) for g in rng]
            yield
        uw = [_dot(p[g].astype(BF16), jnp.concatenate([bd16(ch(vb16, g)), bd16(ch(kbg16, g))], axis=1))
              for g in rng]
        for g in rng:
            dst = pl.ds(t * R + g * C, C)
            u_ref[dst, :] = uw[g][:, :LANES]
            w_ref[dst, :] = uw[g][:, LANES:].astype(BF16)

    def interleave(*gens):
        live = list(gens)
        while live:
            for gen in list(live):
                try:
                    next(gen)
                except StopIteration:
                    live.remove(gen)

    handoff = {}
    n_tiles = S // R
    interleave(front(0))
    for t in range(1, n_tiles):
        interleave(front(t), back(t - 1))
    interleave(back(n_tiles - 1))


def _gdn_prep(qkv3, lat3):
    B, S, _ = qkv3.shape
    HP = HEAD_PAIRS
    seq_blk = lambda off: pl.BlockSpec((None, S, LANES), lambda b, h, off=off: (b, 0, off + h))
    n_sd = S // GDN_CHUNK * SUBLANES
    return pl.pallas_call(
        _gdn_prep_kernel,
        grid=(B, HP),
        in_specs=[
            seq_blk(0), seq_blk(HP), seq_blk(2 * HP),
            pl.BlockSpec((None, S, LANES), lambda b, h: (b, 0, LAT_W // LANES - 1)),
        ],
        out_specs=[seq_blk(0)] * 5 + [pl.BlockSpec((None, n_sd, LANES), lambda b, h: (b, 0, h))],
        out_shape=[
            jax.ShapeDtypeStruct((B, S, GDN_V_W), F32),
            jax.ShapeDtypeStruct((B, S, GDN_V_W), BF16),
            jax.ShapeDtypeStruct((B, S, GDN_V_W), BF16),
            jax.ShapeDtypeStruct((B, S, GDN_V_W), BF16),
            jax.ShapeDtypeStruct((B, S, GDN_V_W), BF16),
            jax.ShapeDtypeStruct((B, n_sd, GDN_V_W), F32),
        ],
        compiler_params=pltpu.CompilerParams(
            dimension_semantics=("arbitrary", "arbitrary"), vmem_limit_bytes=VMEM_LIMIT),
        name="gdn_prep",
    )(qkv3, qkv3, qkv3, lat3)


def _gdn_scan_kernel(u_ref, w_ref, qd_ref, kd_ref, a_ref, sd_ref, z_ref, nw_ref, o_ref, state_ref):
    C = GDN_CHUNK
    nb, rt, _ = u_ref.shape

    @pl.when(pl.program_id(1) == 0)
    def _():
        state_ref[...] = jnp.zeros_like(state_ref)

    lane = lax.broadcasted_iota(jnp.int32, (1, LANES), 1)
    head1 = lane >= C
    m0 = jnp.where(head1, 0.0, 1.0).astype(BF16)
    m1 = jnp.where(head1, 1.0, 0.0).astype(BF16)
    r128 = lax.broadcasted_iota(jnp.int32, (LANES, LANES), 0)
    c128 = lax.broadcasted_iota(jnp.int32, (LANES, LANES), 1)
    bdmask = (r128 // C) == (c128 // C)
    ones_bd = jnp.where(bdmask, 1.0, 0.0).astype(BF16)

    chains = [(bb, hp) for bb in range(nb) for hp in range(HEAD_PAIRS)]
    blk = lambda hp: slice(hp * LANES, (hp + 1) * LANES)

    def bd16(m):
        return jnp.concatenate([m * m0, m * m1], axis=0)

    def chunk(n, carry):
        rows = pl.ds(pl.multiple_of(n * C, C), C)
        sd_rows = pl.ds(pl.multiple_of(n * SUBLANES, SUBLANES), SUBLANES)
        state = [state_ref[i] for i in range(len(chains))]
        res = [_dot(jnp.concatenate([w_ref[bb, rows, blk(hp)], qd_ref[bb, rows, blk(hp)]], axis=0),
                    state[i].astype(BF16)) for i, (bb, hp) in enumerate(chains)]
        v_new = [(u_ref[bb, rows, blk(hp)] - res[i][:C]).astype(BF16) for i, (bb, hp) in enumerate(chains)]
        upd = [_dot_tn(kd_ref[bb, rows, blk(hp)], v_new[i]) for i, (bb, hp) in enumerate(chains)]
        for i, (bb, hp) in enumerate(chains):
            sd = sd_ref[bb, sd_rows, blk(hp)][0:1, :]
            state_ref[i] = state[i] * sd + jnp.where(bdmask, upd[i], 0.0)
        o = [res[i][C:] + _dot(a_ref[bb, rows, blk(hp)], bd16(v_new[i])) for i, (bb, hp) in enumerate(chains)]
        for i, (bb, hp) in enumerate(chains):
            ms = _dot((o[i] * o[i]).astype(BF16), ones_bd) * (1.0 / GDN_DV)
            y = o[i] * lax.rsqrt(ms + EPS) * nw_ref[:, blk(hp)] * z_ref[bb, rows, blk(hp)].astype(F32)
            o_ref[bb, rows, blk(hp)] = y.astype(o_ref.dtype)
        return carry

    lax.fori_loop(0, rt // C, chunk, 0)


def _gdn_scan(u, w, qd, kd, a, sd, zg, norm_w8):
    B, S, W = u.shape
    rt = min(SCAN_ROWS, S)
    nb = min(SCAN_BATCH, B)
    assert B % nb == 0 and S % rt == 0
    n_sd = rt // GDN_CHUNK * SUBLANES
    seq = pl.BlockSpec((nb, rt, W), lambda b, t: (b, t, 0))
    return pl.pallas_call(
        _gdn_scan_kernel,
        grid=(B // nb, S // rt),
        in_specs=[seq, seq, seq, seq, seq,
                  pl.BlockSpec((nb, n_sd, W), lambda b, t: (b, t, 0)),
                  seq,
                  pl.BlockSpec((1, W), lambda b, t: (0, 0))],
        out_specs=seq,
        out_shape=jax.ShapeDtypeStruct((B, S, W), BF16),
        scratch_shapes=[pltpu.VMEM((nb * HEAD_PAIRS, LANES, LANES), F32)],
        compiler_params=pltpu.CompilerParams(
            dimension_semantics=("arbitrary", "arbitrary"), vmem_limit_bytes=VMEM_LIMIT),
        name="gdn_scan",
    )(u, w, qd, kd, a, sd, zg, norm_w8)


def _mla_prep_kernel(pos_ref, freq_ref, lat_ref, qnw_ref, kvnw_ref, wq_ref, wqs_ref, wk_ref, wv_ref,
                     q_ref, k_ref, v_ref):
    half = MLA_ROPE // 2
    lo, mid, hi = MLA_NOPE, MLA_NOPE + half, MLA_NOPE + MLA_ROPE
    scale = (MLA_NOPE + MLA_ROPE) ** -0.5 * LOG2E
    ang = freq_ref[...] * pos_ref[...].astype(F32)
    f_i = lax.broadcasted_iota(jnp.int32, (3 * half, LANES), 0) % half
    l_i = lax.broadcasted_iota(jnp.int32, (3 * half, LANES), 1)
    in_lo = (l_i >= lo) & (l_i < mid) & (l_i - lo == f_i)
    in_hi = (l_i >= mid) & (l_i < hi) & (l_i - mid == f_i)
    expand_cos = jnp.where(in_lo | in_hi, 1.0, 0.0).astype(BF16)
    expand_sin = jnp.where(in_lo, -1.0, jnp.where(in_hi, 1.0, 0.0)).astype(BF16)
    cos_tab = _dot_tn(jnp.concatenate(_split3(jnp.cos(ang)), axis=0), expand_cos)
    sin_tab = _dot_tn(jnp.concatenate(_split3(jnp.sin(ang)), axis=0), expand_sin)
    lane = lax.broadcasted_iota(jnp.int32, (1, LANES), 1)
    nope = jnp.where(lane < MLA_NOPE, 1.0, 0.0)

    lat = lat_ref[...]
    cq = _rms(lat[:, 0:MLA_Q_RANK], qnw_ref[...]).astype(BF16)
    ckv = _rms(lat[:, MLA_Q_RANK:MLA_Q_RANK + MLA_KV_RANK], kvnw_ref[...]).astype(BF16)
    q = _dot(cq, wq_ref[...])
    q_sw = _dot(cq, wqs_ref[...])
    k_nope = _dot(ckv, wk_ref[...])
    row_all = lax.broadcasted_iota(jnp.int32, (MLA_HEADS * LANES, 1), 0)
    ones_row = jnp.where(row_all % LANES == MLA_V, 1.0, 0.0)
    v_ref[...] = (_dot_nt(wv_ref[...], ckv) + ones_row).astype(v_ref.dtype)

    kpe = pltpu.roll(lat[:, LAT_W - LANES:], MLA_NOPE - KPE_OFF, axis=1)
    kpe_sw = jnp.where(lane < mid, pltpu.roll(kpe, LANES - half, axis=1), pltpu.roll(kpe, half, axis=1))
    k_rope = kpe * cos_tab + kpe_sw * sin_tab
    c_q = (cos_tab + nope) * scale
    s_q = sin_tab * scale
    for h in range(MLA_HEADS):
        blk = slice(h * LANES, (h + 1) * LANES)
        q_ref[:, blk] = (q[:, blk] * c_q + q_sw[:, blk] * s_q).astype(q_ref.dtype)
        k_ref[:, blk] = (k_nope[:, blk] + k_rope).astype(k_ref.dtype)


def _mla_prep(pos_row, inv_freq, lat, q_norm_w, kv_norm_w, wq_p, wqs_p, wk_p, wv_p):
    T = lat.shape[0]
    tm = min(TOK_TILE, T)
    full = lambda a: pl.BlockSpec(a.shape, lambda i: (0,) * a.ndim)
    return pl.pallas_call(
        _mla_prep_kernel,
        grid=(T // tm,),
        in_specs=[
            pl.BlockSpec((1, tm), lambda i: (0, i)),
            full(inv_freq),
            pl.BlockSpec((tm, LAT_W), lambda i: (i, 0)),
            full(q_norm_w), full(kv_norm_w), full(wq_p), full(wqs_p), full(wk_p), full(wv_p),
        ],
        out_specs=[
            pl.BlockSpec((tm, MLA_HEADS * LANES), lambda i: (i, 0)),
            pl.BlockSpec((tm, MLA_HEADS * LANES), lambda i: (i, 0)),
            pl.BlockSpec((MLA_HEADS * LANES, tm), lambda i: (0, i)),
        ],
        out_shape=[
            jax.ShapeDtypeStruct((T, MLA_HEADS * LANES), BF16),
            jax.ShapeDtypeStruct((T, MLA_HEADS * LANES), BF16),
            jax.ShapeDtypeStruct((MLA_HEADS * LANES, T), BF16),
        ],
        compiler_params=pltpu.CompilerParams(
            dimension_semantics=("arbitrary",), vmem_limit_bytes=VMEM_LIMIT),
        name="mla_prep",
    )(pos_row, inv_freq, lat, q_norm_w, kv_norm_w, wq_p, wqs_p, wk_p, wv_p)


def _attn_kernel(q_ref, k_ref, vt_ref, o_ref, s_ref, m_ref, acc_ref, *, tq):
    seq = q_ref.shape[0]
    nq = seq // tq
    nh = q_ref.shape[1] // LANES
    lane = lax.broadcasted_iota(jnp.int32, (1, LANES), 1)
    causal = lax.broadcasted_iota(jnp.int32, (tq, tq), 0) <= lax.broadcasted_iota(jnp.int32, (tq, tq), 1)
    blks = [slice(h * LANES, (h + 1) * LANES) for h in range(nh)]
    rows = lambda i: pl.ds(pl.multiple_of(i * tq, tq), tq)

    def scores(qi, j, h):
        s_ref[h] = _dot_nt(k_ref[rows(j), blks[h]], q_ref[rows(qi), blks[h]])

    def consume(j, h, masked):
        m_prev = m_ref[h][0:1, :]
        s = s_ref[h]
        if masked:
            s = jnp.where(causal, s, NEG_BIG)
        m_new = jnp.maximum(m_prev, jnp.max(s, axis=0, keepdims=True))
        alpha = jnp.exp2(m_prev - m_new)
        p = jnp.exp2(s - m_new).astype(BF16)
        m_ref[h] = jnp.broadcast_to(m_new, (SUBLANES, tq))
        acc_ref[h] = acc_ref[h] * alpha + _dot(vt_ref[blks[h], rows(j)], p)

    def consume_and_prefetch(qi, j, h, masked):
        consume(j, h, masked)
        nxt = h + ATT_AHEAD
        if nxt < nh:
            scores(qi, j, nxt)
        elif not masked:
            scores(qi, j + 1, nxt - nh)
        else:
            scores(jnp.minimum(qi + 1, nq - 1), 0, nxt - nh)

    def q_block(qi, carry):
        def blocks(j0, count):
            for j in range(count):
                for h in range(nh):
                    consume_and_prefetch(qi, j0 + j, h, False)

        def unrolled(i, c):
            blocks(ATT_UNROLL * i, ATT_UNROLL)
            return c

        m_ref[...] = jnp.full(m_ref.shape, NEG_BIG, F32)
        acc_ref[...] = jnp.zeros(acc_ref.shape, F32)
        n_full = qi
        lax.fori_loop(0, n_full // ATT_UNROLL, unrolled, 0)
        done = n_full - n_full % ATT_UNROLL
        size = ATT_UNROLL // 2
        while size >= 1:
            take = (n_full % (2 * size)) >= size

            @pl.when(take)
            def _(done=done, size=size):
                blocks(done, size)

            done = done + jnp.where(take, size, 0)
            size //= 2
        for h in range(nh):
            consume_and_prefetch(qi, n_full, h, True)
        outs = []
        for h in range(nh):
            acc = acc_ref[h]
            outs.append((acc * (1.0 / acc[MLA_V:MLA_V + 1, :])).T)
        for pr in range(nh // 2):
            pair = jnp.where(lane < MLA_V, outs[2 * pr], pltpu.roll(outs[2 * pr + 1], MLA_V, axis=1))
            o_ref[rows(qi), pr * LANES:(pr + 1) * LANES] = pair.astype(o_ref.dtype)
        return carry

    for h in range(ATT_AHEAD):
        scores(0, 0, h)
    lax.fori_loop(0, nq, q_block, 0)


def _attn(q3, k3, vt):
    B, S, _ = q3.shape
    tq = min(ATT_BLOCK, S)
    assert S % tq == 0
    nh = ATT_HEADS
    return pl.pallas_call(
        functools.partial(_attn_kernel, tq=tq),
        grid=(B, MLA_HEADS // nh),
        in_specs=[
            pl.BlockSpec((None, S, nh * LANES), lambda b, h: (b, 0, h)),
            pl.BlockSpec((None, S, nh * LANES), lambda b, h: (b, 0, h)),
            pl.BlockSpec((nh * LANES, S), lambda b, h: (h, b)),
        ],
        out_specs=pl.BlockSpec((None, S, nh * MLA_V), lambda b, h: (b, 0, h)),
        out_shape=jax.ShapeDtypeStruct((B, S, MLA_W), BF16),
        scratch_shapes=[pltpu.VMEM((nh, tq, tq), F32),
                        pltpu.VMEM((nh, SUBLANES, tq), F32),
                        pltpu.VMEM((nh, LANES, tq), F32)],
        compiler_params=pltpu.CompilerParams(
            dimension_semantics=("arbitrary", "arbitrary"), vmem_limit_bytes=VMEM_LIMIT),
        name="attn",
    )(q3, k3, vt)


def _tail_kernel(x_ref, yg_ref, om_ref, p_ref, monw_ref, wout_ref, mlpnw_ref, wup_ref, wdown_ref,
                 wpp_ref, postnw_ref, gatenw_ref, wpg_ref, finnw_ref, o_ref):
    ymla = _rms(om_ref[...].astype(F32), monw_ref[...]).astype(BF16)
    h = x_ref[...] + _dot(yg_ref[...], wout_ref[0:GDN_V_W, :]) + _dot(ymla, wout_ref[GDN_V_W:, :])
    u = _rms(h, mlpnw_ref[...]).astype(BF16)
    ff_blk = 1024
    acc = jnp.zeros_like(h)
    for j in range(D_FF // ff_blk):
        cols = slice(j * ff_blk, (j + 1) * ff_blk)
        hid = jnp.maximum(_dot(u, wup_ref[:, cols]), 0.0)
        acc = acc + _dot((hid * hid).astype(BF16), wdown_ref[cols, :])
    h = h + acc
    e = _rms(_dot(p_ref[...].astype(BF16), wpp_ref[...]), postnw_ref[...])
    gate = _sigmoid(_dot(_rms(h, gatenw_ref[...]).astype(BF16), wpg_ref[...]))
    h = h + gate * e
    o_ref[...] = _rms(h, finnw_ref[...])


def _tail(x2, yg, om, p2, mla_out_norm_w, w_out, mlp_norm_w, w_up, w_down, w_ple_proj,
          ple_post_norm_w, ple_gate_norm_w, w_ple_gate, final_norm_w):
    T = x2.shape[0]
    tm = min(TOK_TILE, T)
    tok = lambda w: pl.BlockSpec((tm, w), lambda i: (i, 0))
    const = lambda a: pl.BlockSpec(a.shape, lambda i: (0, 0), pipeline_mode=pl.Buffered(1))
    return pl.pallas_call(
        _tail_kernel,
        grid=(T // tm,),
        in_specs=[
            tok(D_MODEL), tok(GDN_V_W), tok(MLA_W), tok(PLE_DIM),
            const(mla_out_norm_w), const(w_out), const(mlp_norm_w), const(w_up), const(w_down),
            const(w_ple_proj), const(ple_post_norm_w), const(ple_gate_norm_w), const(w_ple_gate),
            const(final_norm_w),
        ],
        out_specs=tok(D_MODEL),
        out_shape=jax.ShapeDtypeStruct((T, D_MODEL), F32),
        compiler_params=pltpu.CompilerParams(
            dimension_semantics=("arbitrary",), vmem_limit_bytes=VMEM_LIMIT),
        name="tail",
    )(x2, yg, om, p2, mla_out_norm_w, w_out, mlp_norm_w, w_up, w_down, w_ple_proj,
      ple_post_norm_w, ple_gate_norm_w, w_ple_gate, final_norm_w)


def _pack_w_in(w):
    o_b = QKV_W + GDN_V_W
    o_cq = o_b + 2 * GDN_HEADS
    n_lat = MLA_Q_RANK + MLA_KV_RANK + MLA_ROPE
    w = w.astype(BF16)
    pad = jnp.zeros((D_MODEL, LAT_W - n_lat - 2 * GDN_HEADS), BF16)
    return jnp.concatenate([w[:, :o_b], w[:, o_cq:o_cq + n_lat], w[:, o_b:o_cq], pad], axis=1)


def _gate_lane_vector(per_head):
    return jnp.zeros((1, LANES), F32).at[0, A_OFF:A_OFF + GDN_HEADS].set(per_head.astype(F32))


def _pack_mla_weights(w_q_b, w_kv_b):
    wq = w_q_b.reshape(MLA_Q_RANK, MLA_HEADS, MLA_NOPE + MLA_ROPE)
    wq = jnp.pad(wq, ((0, 0), (0, 0), (0, LANES - MLA_NOPE - MLA_ROPE)))
    wkv = w_kv_b.reshape(MLA_KV_RANK, MLA_HEADS, MLA_NOPE + MLA_V)
    wk = jnp.pad(wkv[:, :, :MLA_NOPE], ((0, 0), (0, 0), (0, LANES - MLA_NOPE)))
    wv = jnp.pad(wkv[:, :, MLA_NOPE:], ((0, 0), (0, 0), (0, LANES - MLA_V)))
    half = MLA_ROPE // 2
    zeros = lambda n: jnp.zeros((MLA_Q_RANK, MLA_HEADS, n), wq.dtype)
    wq_sw = jnp.concatenate([zeros(MLA_NOPE), wq[:, :, MLA_NOPE + half:MLA_NOPE + MLA_ROPE],
                             wq[:, :, MLA_NOPE:MLA_NOPE + half], zeros(LANES - MLA_NOPE - MLA_ROPE)], axis=2)
    return (wq.reshape(MLA_Q_RANK, MLA_HEADS * LANES).astype(BF16),
            wq_sw.reshape(MLA_Q_RANK, MLA_HEADS * LANES).astype(BF16),
            wk.reshape(MLA_KV_RANK, MLA_HEADS * LANES).astype(BF16),
            wv.reshape(MLA_KV_RANK, MLA_HEADS * LANES).T.astype(BF16))


def kernel(x, p, positions, mix_norm_w, w_in, conv_w, A_log, dt_bias, gdn_norm_w, q_norm_w, w_q_b,
           kv_norm_w, w_kv_b, mla_out_norm_w, w_out, mlp_norm_w, w_up, w_down, w_ple_proj,
           ple_post_norm_w, ple_gate_norm_w, w_ple_gate, final_norm_w):
    B, S, _ = x.shape
    T = B * S
    assert w_in.shape[0] == 1, "one layer"
    row = lambda a: a.reshape(1, -1).astype(F32)
    x2 = x.reshape(T, D_MODEL)

    qkv, zg, lat = _inproj(x2, row(mix_norm_w[0]), _pack_w_in(w_in[0]), conv_w[0].astype(F32),
                           _gate_lane_vector(A_log[0]), _gate_lane_vector(dt_bias[0]), S)

    factors = _gdn_prep(qkv.reshape(B, S, QKV_W), lat.reshape(B, S, LAT_W))
    y_gdn = _gdn_scan(*factors, zg.reshape(B, S, GDN_V_W), row(jnp.tile(gdn_norm_w[0], GDN_HEADS)))

    half = MLA_ROPE // 2
    inv_freq = (ROPE_THETA ** (-jnp.arange(0, MLA_ROPE, 2, dtype=F32) / MLA_ROPE)).reshape(half, 1)
    q_att, k_att, v_att = _mla_prep(positions.reshape(1, T), inv_freq, lat, row(q_norm_w[0]),
                                    row(kv_norm_w[0]), *_pack_mla_weights(w_q_b[0], w_kv_b[0]))
    o_mla = _attn(q_att.reshape(B, S, -1), k_att.reshape(B, S, -1), v_att)

    out = _tail(x2, y_gdn.reshape(T, GDN_V_W), o_mla.reshape(T, MLA_W), p[0].reshape(T, PLE_DIM),
                row(mla_out_norm_w[0]), w_out[0].astype(BF16), row(mlp_norm_w[0]),
                w_up[0].astype(BF16), w_down[0].astype(BF16), w_ple_proj[0].astype(BF16),
                row(ple_post_norm_w[0]), row(ple_gate_norm_w[0]), w_ple_gate[0].astype(BF16),
                row(final_norm_w))
    return out.reshape(B, S, D_MODEL)
```

```python
import functools

import jax
import jax.numpy as jnp
from jax import lax
from jax.experimental import pallas as pl
from jax.experimental.pallas import tpu as pltpu

F32 = jnp.float32
BF16 = jnp.bfloat16

D_MODEL = 1024
PLE_DIM = 256
GDN_HEADS = 8
GDN_DK = 64
GDN_DV = 64
GDN_QK_W = GDN_HEADS * GDN_DK
GDN_V_W = GDN_HEADS * GDN_DV
GDN_CONV = 4
GDN_CHUNK = 64
MLA_HEADS = 8
MLA_NOPE = 64
MLA_ROPE = 32
MLA_V = 64
MLA_W = MLA_HEADS * MLA_V
MLA_Q_RANK = 256
MLA_KV_RANK = 128
ROPE_THETA = 10000.0
D_FF = 4 * D_MODEL
EPS = 1e-6

LANES = 128
SUBLANES = 8
HEAD_PAIRS = GDN_HEADS // 2
QKV_W = 2 * GDN_QK_W + GDN_V_W
LAT_W = 512
IN_W_PAD = QKV_W + GDN_V_W + LAT_W
KPE_OFF, B_OFF, A_OFF = 0, MLA_ROPE, MLA_ROPE + GDN_HEADS
VMEM_LIMIT = 56 * 1024 * 1024

TOK_TILE = 512
CUM_ROWS = 256
CONV_COLS = 512
GDN_ROWS = 1024
INV_BASE = 8
SCAN_ROWS = 512
SCAN_BATCH = 4
ATT_BLOCK = 512
ATT_HEADS = 4
ATT_UNROLL = 4
ATT_AHEAD = 3
NEG_BIG = -1e30
LOG2E = 1.4426950408889634


def _dot(a, b):
    return jnp.dot(a, b, preferred_element_type=F32)


def _dot_nt(a, b):
    return lax.dot_general(a, b, (((1,), (1,)), ((), ())), preferred_element_type=F32)


def _dot_tn(a, b):
    return lax.dot_general(a, b, (((0,), (0,)), ((), ())), preferred_element_type=F32)


def _split2(x):
    hi = x.astype(BF16)
    lo = (x - hi.astype(F32)).astype(BF16)
    return hi, lo


def _split3(x):
    hi = x.astype(BF16)
    r1 = x - hi.astype(F32)
    mid = r1.astype(BF16)
    lo = (r1 - mid.astype(F32)).astype(BF16)
    return hi, mid, lo


def _dot_exact_rhs(a_bf16, x):
    return _dot(jnp.concatenate([a_bf16] * 3, axis=1), jnp.concatenate(_split3(x), axis=0))


def _dot_exact_lhs(x, b_bf16):
    return _dot(jnp.concatenate(_split3(x), axis=1), jnp.concatenate([b_bf16] * 3, axis=0))


def _rms(x, w):
    ms = jnp.mean(x * x, axis=-1, keepdims=True)
    return x * lax.rsqrt(ms + EPS) * w


def _sigmoid(x):
    return 1.0 / (1.0 + jnp.exp(-x))


def _silu(x):
    h = 0.5 * x
    return h + h * jnp.tanh(h)


def _softplus(x):
    return jnp.maximum(x, 0.0) + jnp.log(1.0 + jnp.exp(-jnp.abs(x)))


def _chunk_tril(n):
    r = lax.broadcasted_iota(jnp.int32, (n, n), 0)
    c = lax.broadcasted_iota(jnp.int32, (n, n), 1)
    return jnp.where(((r // GDN_CHUNK) == (c // GDN_CHUNK)) & (c <= r), 1.0, 0.0).astype(BF16)


def _inproj_kernel(x_ref, nw_ref, w_ref, cw_ref, alog_ref, dtb_ref, qkv_ref, z_ref, lat_ref, halo_ref,
                   *, tiles_per_seq):
    tm = x_ref.shape[0]
    first = (pl.program_id(0) % tiles_per_seq) == 0
    u = _rms(x_ref[...], nw_ref[...]).astype(BF16)

    def conv_chunk(c):
        cols = slice(c * CONV_COLS, (c + 1) * CONV_COLS)
        raw = _dot(u, w_ref[:, cols])
        prev = jnp.where(first, 0.0, halo_ref[:, cols])
        halo_ref[:, cols] = raw[tm - SUBLANES:, :]
        ext = jnp.concatenate([prev, raw], axis=0)
        cw = cw_ref[:, cols]
        acc = raw * cw[GDN_CONV - 1:GDN_CONV, :]
        for kk in range(GDN_CONV - 1):
            shifted = pltpu.roll(ext, GDN_CONV - 1 - kk, axis=0)[SUBLANES:, :]
            acc = acc + shifted * cw[kk:kk + 1, :]
        qkv_ref[:, cols] = _silu(acc)

    def gate_z():
        zf = _dot(u, w_ref[:, QKV_W:QKV_W + GDN_V_W])
        z_ref[...] = _silu(zf).astype(z_ref.dtype)

    def latent_head():
        lat_ref[:, :LAT_W - LANES] = _dot(u, w_ref[:, QKV_W + GDN_V_W:IN_W_PAD - LANES])

    def latent_gates():
        l3 = _dot(u, w_ref[:, IN_W_PAD - LANES:IN_W_PAD])
        lane = lax.broadcasted_iota(jnp.int32, (1, LANES), 1)
        is_b = (lane >= B_OFF) & (lane < B_OFF + GDN_HEADS)
        is_a = (lane >= A_OFF) & (lane < A_OFF + GDN_HEADS)
        g_raw = jnp.where(is_a, -(jnp.exp(alog_ref[...]) * _softplus(l3 + dtb_ref[...])), 0.0)
        hi, mid, lo = (t.astype(F32) for t in _split3(g_raw))
        packed = (hi + pltpu.roll(mid, GDN_HEADS, axis=1) + pltpu.roll(lo, 2 * GDN_HEADS, axis=1)).astype(BF16)
        tril = _chunk_tril(CUM_ROWS)
        cum = jnp.concatenate([_dot(tril, packed[r:r + CUM_ROWS, :]) for r in range(0, tm, CUM_ROWS)], axis=0)
        g_cum = cum + pltpu.roll(cum, LANES - GDN_HEADS, axis=1) + pltpu.roll(cum, LANES - 2 * GDN_HEADS, axis=1)
        lat_ref[:, LAT_W - LANES:] = jnp.where(is_b, _sigmoid(l3), jnp.where(is_a, g_cum, l3))

    fillers = [latent_gates, gate_z, latent_head]
    for c in range(QKV_W // CONV_COLS):
        if fillers:
            fillers.pop(0)()
        conv_chunk(c)
    for f in fillers:
        f()


def _inproj(x2, mix_norm_w, w_in_p, conv_w, alog_vec, dtb_vec, seq_len):
    T = x2.shape[0]
    tm = min(TOK_TILE, seq_len)
    assert seq_len % tm == 0 and tm % GDN_CHUNK == 0
    nt = T // tm
    const = lambda a: pl.BlockSpec(a.shape, lambda i: (0, 0))
    return pl.pallas_call(
        functools.partial(_inproj_kernel, tiles_per_seq=seq_len // tm),
        grid=(nt,),
        in_specs=[
            pl.BlockSpec((tm, D_MODEL), lambda i: (i, 0)),
            const(mix_norm_w), const(w_in_p), const(conv_w), const(alog_vec), const(dtb_vec),
        ],
        out_specs=[
            pl.BlockSpec((tm, QKV_W), lambda i: (i, 0)),
            pl.BlockSpec((tm, GDN_V_W), lambda i: (i, 0)),
            pl.BlockSpec((tm, LAT_W), lambda i: (i, 0)),
        ],
        out_shape=[
            jax.ShapeDtypeStruct((T, QKV_W), F32),
            jax.ShapeDtypeStruct((T, GDN_V_W), BF16),
            jax.ShapeDtypeStruct((T, LAT_W), F32),
        ],
        scratch_shapes=[pltpu.VMEM((SUBLANES, QKV_W), F32)],
        compiler_params=pltpu.CompilerParams(
            dimension_semantics=("arbitrary",), vmem_limit_bytes=VMEM_LIMIT),
        name="inproj",
    )(x2, mix_norm_w, w_in_p, conv_w, alog_vec, dtb_vec)


def _gdn_prep_kernel(q_ref, k_ref, v_ref, lat_ref, u_ref, w_ref, qd_ref, kd_ref, a_ref, sd_ref):
    C = GDN_CHUNK
    S = q_ref.shape[0]
    R = min(GDN_ROWS, S)
    G = R // C
    hp = pl.program_id(1)

    lane = lax.broadcasted_iota(jnp.int32, (1, LANES), 1)
    head1 = lane >= C
    m0 = jnp.where(head1, 0.0, 1.0).astype(BF16)
    m1 = jnp.where(head1, 1.0, 0.0).astype(BF16)
    r128 = lax.broadcasted_iota(jnp.int32, (LANES, LANES), 0)
    c128 = lax.broadcasted_iota(jnp.int32, (LANES, LANES), 1)
    ones_bd = jnp.where((r128 // C) == (c128 // C), 1.0, 0.0).astype(BF16)
    rC = lax.broadcasted_iota(jnp.int32, (C, LANES), 0)
    cC = lax.broadcasted_iota(jnp.int32, (C, LANES), 1) % C
    eye2 = rC == cC
    tril2 = rC >= cC
    strict2 = rC > cC
    eye2f = jnp.where(eye2, 1.0, 0.0)
    ones_cc = jnp.ones((C, C), BF16)
    sel_r = lax.broadcasted_iota(jnp.int32, (LANES, 2 * LANES), 0)
    sel_c = lax.broadcasted_iota(jnp.int32, (LANES, 2 * LANES), 1)
    sel_src = jnp.where(sel_c < LANES, B_OFF, A_OFF) + 2 * hp + ((sel_c % LANES) // C)
    sel = jnp.where(sel_r == sel_src, 1.0, 0.0).astype(BF16)

    def bd16(m):
        return jnp.concatenate([m * m0, m * m1], axis=0)

    def bd_parts(m):
        hi, lo = _split2(m)
        return bd16(hi), bd16(lo)

    def dot_hi(a, b_hi, b_lo):
        a_hi, a_lo = _split2(a)
        return _dot(jnp.concatenate([a_hi, a_hi, a_lo], axis=1), jnp.concatenate([b_hi, b_lo, b_hi], axis=0))

    def merge_dot(a, b):
        return _dot(a.astype(BF16), bd16(b.astype(BF16)))

    diag_blk = (rC // INV_BASE) == (cC // INV_BASE)
    off_blks = []
    size = INV_BASE
    while size < C:
        off_blks.append(((rC // (2 * size)) == (cC // (2 * size))) & ((rC // size) % 2 == 1) & ((cC // size) % 2 == 0))
        size *= 2

    ch = lambda arr, g: arr[g * C:(g + 1) * C, :]
    rng = range(G)

    def front(t):
        rows = pl.ds(t * R, R)
        q = q_ref[rows, :]
        k = k_ref[rows, :]
        v = v_ref[rows, :]
        q = q * lax.rsqrt(_dot((q * q).astype(BF16), ones_bd) + EPS) * (GDN_DK ** -0.5)
        k = k * lax.rsqrt(_dot((k * k).astype(BF16), ones_bd) + EPS)
        bg = _dot_exact_lhs(lat_ref[rows, :], sel)
        beta = bg[:, :LANES]
        gc = bg[:, LANES:]
        eg = jnp.exp(gc)
        kb = k * beta
        vb16 = (v * beta).astype(BF16)
        kbg16 = (kb * eg).astype(BF16)
        qd_ref[rows, :] = (q * eg).astype(BF16)
        k16 = k.astype(BF16)
        q16 = q.astype(BF16)

        yield
        kkqk = [_dot_nt(jnp.concatenate([ch(k16, g), ch(q16, g)], axis=0), bd16(ch(k16, g))) for g in rng]
        g_row = [_dot_exact_rhs(ones_cc, jnp.where(eye2, ch(gc, g), 0.0)) for g in rng]
        dm = [jnp.where(tril2, jnp.exp(jnp.where(tril2, ch(gc, g) - g_row[g], 0.0)), 0.0) for g in rng]
        yield
        for g in rng:
            a_ref[pl.ds(t * R + g * C, C), :] = (kkqk[g][C:] * dm[g]).astype(BF16)
            g_last = ch(gc, g)[C - 1:C, :]
            kd_ref[pl.ds(t * R + g * C, C), :] = (ch(k, g) * jnp.exp(g_last - ch(gc, g))).astype(BF16)
            sd_ref[pl.ds((t * G + g) * SUBLANES, SUBLANES), :] = jnp.broadcast_to(
                jnp.exp(g_last), (SUBLANES, LANES))
        yield
        low = [jnp.where(strict2, kkqk[g][:C] * dm[g] * ch(beta, g), 0.0) for g in rng]
        x = [jnp.where(diag_blk, -low[g], 0.0) for g in rng]
        p = [eye2f + x[g] for g in rng]
        y = [dot_hi(x[g], *bd_parts(x[g])) for g in rng]
        yield
        rhs = []
        for g in rng:
            p_hi, p_lo = bd_parts(p[g])
            y_hi, y_lo = bd_parts(y[g])
            rhs.append((jnp.concatenate([p_hi, y_hi], axis=1), jnp.concatenate([p_lo, y_lo], axis=1)))
        res = [dot_hi(y[g], *rhs[g]) for g in rng]
        p = [p[g] + res[g][:, :LANES] for g in rng]
        y = [res[g][:, LANES:] for g in rng]
        yield
        p = [p[g] + dot_hi(y[g], *bd_parts(p[g])) for g in rng]
        handoff[t] = (low, p, vb16, kbg16)

    def back(t):
        low, p, vb16, kbg16 = handoff[t]
        for off_blk in off_blks:
            m1 = [merge_dot(jnp.where(off_blk, low[g], 0.0), p[g]) for g in rng]
            yield
            p = [p[g] - merge_dot(p[g], m1[g]) for g in rng]
            yield
        uw = [_dot(p[g].astype(BF16), jnp.concatenate([bd16(ch(vb16, g)), bd16(ch(kbg16, g))], axis=1))
              for g in rng]
        for g in rng:
            dst = pl.ds(t * R + g * C, C)
            u_ref[dst, :] = uw[g][:, :LANES]
            w_ref[dst, :] = uw[g][:, LANES:].astype(BF16)

    def interleave(*gens):
        live = list(gens)
        while live:
            for gen in list(live):
                try:
                    next(gen)
                except StopIteration:
                    live.remove(gen)

    handoff = {}
    n_tiles = S // R
    interleave(front(0))
    for t in range(1, n_tiles):
        interleave(front(t), back(t - 1))
    interleave(back(n_tiles - 1))


def _gdn_prep(qkv3, lat3):
    B, S, _ = qkv3.shape
    HP = HEAD_PAIRS
    seq_blk = lambda off: pl.BlockSpec((None, S, LANES), lambda b, h, off=off: (b, 0, off + h))
    n_sd = S // GDN_CHUNK * SUBLANES
    return pl.pallas_call(
        _gdn_prep_kernel,
        grid=(B, HP),
        in_specs=[
            seq_blk(0), seq_blk(HP), seq_blk(2 * HP),
            pl.BlockSpec((None, S, LANES), lambda b, h: (b, 0, LAT_W // LANES - 1)),
        ],
        out_specs=[seq_blk(0)] * 5 + [pl.BlockSpec((None, n_sd, LANES), lambda b, h: (b, 0, h))],
        out_shape=[
            jax.ShapeDtypeStruct((B, S, GDN_V_W), F32),
            jax.ShapeDtypeStruct((B, S, GDN_V_W), BF16),
            jax.ShapeDtypeStruct((B, S, GDN_V_W), BF16),
            jax.ShapeDtypeStruct((B, S, GDN_V_W), BF16),
            jax.ShapeDtypeStruct((B, S, GDN_V_W), BF16),
            jax.ShapeDtypeStruct((B, n_sd, GDN_V_W), F32),
        ],
        compiler_params=pltpu.CompilerParams(
            dimension_semantics=("arbitrary", "arbitrary"), vmem_limit_bytes=VMEM_LIMIT),
        name="gdn_prep",
    )(qkv3, qkv3, qkv3, lat3)


def _gdn_scan_kernel(u_ref, w_ref, qd_ref, kd_ref, a_ref, sd_ref, z_ref, nw_ref, o_ref, state_ref):
    C = GDN_CHUNK
    nb, rt, _ = u_ref.shape

    @pl.when(pl.program_id(1) == 0)
    def _():
        state_ref[...] = jnp.zeros_like(state_ref)

    lane = lax.broadcasted_iota(jnp.int32, (1, LANES), 1)
    head1 = lane >= C
    m0 = jnp.where(head1, 0.0, 1.0).astype(BF16)
    m1 = jnp.where(head1, 1.0, 0.0).astype(BF16)
    r128 = lax.broadcasted_iota(jnp.int32, (LANES, LANES), 0)
    c128 = lax.broadcasted_iota(jnp.int32, (LANES, LANES), 1)
    bdmask = (r128 // C) == (c128 // C)
    ones_bd = jnp.where(bdmask, 1.0, 0.0).astype(BF16)

    chains = [(bb, hp) for bb in range(nb) for hp in range(HEAD_PAIRS)]
    blk = lambda hp: slice(hp * LANES, (hp + 1) * LANES)

    def bd16(m):
        return jnp.concatenate([m * m0, m * m1], axis=0)

    def chunk(n, carry):
        rows = pl.ds(pl.multiple_of(n * C, C), C)
        sd_rows = pl.ds(pl.multiple_of(n * SUBLANES, SUBLANES), SUBLANES)
        state = [state_ref[i] for i in range(len(chains))]
        res = [_dot(jnp.concatenate([w_ref[bb, rows, blk(hp)], qd_ref[bb, rows, blk(hp)]], axis=0),
                    state[i].astype(BF16)) for i, (bb, hp) in enumerate(chains)]
        v_new = [(u_ref[bb, rows, blk(hp)] - res[i][:C]).astype(BF16) for i, (bb, hp) in enumerate(chains)]
        upd = [_dot_tn(kd_ref[bb, rows, blk(hp)], v_new[i]) for i, (bb, hp) in enumerate(chains)]
        for i, (bb, hp) in enumerate(chains):
            sd = sd_ref[bb, sd_rows, blk(hp)][0:1, :]
            state_ref[i] = state[i] * sd + jnp.where(bdmask, upd[i], 0.0)
        o = [res[i][C:] + _dot(a_ref[bb, rows, blk(hp)], bd16(v_new[i])) for i, (bb, hp) in enumerate(chains)]
        for i, (bb, hp) in enumerate(chains):
            ms = _dot((o[i] * o[i]).astype(BF16), ones_bd) * (1.0 / GDN_DV)
            y = o[i] * lax.rsqrt(ms + EPS) * nw_ref[:, blk(hp)] * z_ref[bb, rows, blk(hp)].astype(F32)
            o_ref[bb, rows, blk(hp)] = y.astype(o_ref.dtype)
        return carry

    lax.fori_loop(0, rt // C, chunk, 0)


def _gdn_scan(u, w, qd, kd, a, sd, zg, norm_w8):
    B, S, W = u.shape
    rt = min(SCAN_ROWS, S)
    nb = min(SCAN_BATCH, B)
    assert B % nb == 0 and S % rt == 0
    n_sd = rt // GDN_CHUNK * SUBLANES
    seq = pl.BlockSpec((nb, rt, W), lambda b, t: (b, t, 0))
    return pl.pallas_call(
        _gdn_scan_kernel,
        grid=(B // nb, S // rt),
        in_specs=[seq, seq, seq, seq, seq,
                  pl.BlockSpec((nb, n_sd, W), lambda b, t: (b, t, 0)),
                  seq,
                  pl.BlockSpec((1, W), lambda b, t: (0, 0))],
        out_specs=seq,
        out_shape=jax.ShapeDtypeStruct((B, S, W), BF16),
        scratch_shapes=[pltpu.VMEM((nb * HEAD_PAIRS, LANES, LANES), F32)],
        compiler_params=pltpu.CompilerParams(
            dimension_semantics=("arbitrary", "arbitrary"), vmem_limit_bytes=VMEM_LIMIT),
        name="gdn_scan",
    )(u, w, qd, kd, a, sd, zg, norm_w8)


def _mla_prep_kernel(pos_ref, freq_ref, lat_ref, qnw_ref, kvnw_ref, wq_ref, wqs_ref, wk_ref, wv_ref,
                     q_ref, k_ref, v_ref):
    half = MLA_ROPE // 2
    lo, mid, hi = MLA_NOPE, MLA_NOPE + half, MLA_NOPE + MLA_ROPE
    scale = (MLA_NOPE + MLA_ROPE) ** -0.5 * LOG2E
    ang = freq_ref[...] * pos_ref[...].astype(F32)
    f_i = lax.broadcasted_iota(jnp.int32, (3 * half, LANES), 0) % half
    l_i = lax.broadcasted_iota(jnp.int32, (3 * half, LANES), 1)
    in_lo = (l_i >= lo) & (l_i < mid) & (l_i - lo == f_i)
    in_hi = (l_i >= mid) & (l_i < hi) & (l_i - mid == f_i)
    expand_cos = jnp.where(in_lo | in_hi, 1.0, 0.0).astype(BF16)
    expand_sin = jnp.where(in_lo, -1.0, jnp.where(in_hi, 1.0, 0.0)).astype(BF16)
    cos_tab = _dot_tn(jnp.concatenate(_split3(jnp.cos(ang)), axis=0), expand_cos)
    sin_tab = _dot_tn(jnp.concatenate(_split3(jnp.sin(ang)), axis=0), expand_sin)
    lane = lax.broadcasted_iota(jnp.int32, (1, LANES), 1)
    nope = jnp.where(lane < MLA_NOPE, 1.0, 0.0)

    lat = lat_ref[...]
    cq = _rms(lat[:, 0:MLA_Q_RANK], qnw_ref[...]).astype(BF16)
    ckv = _rms(lat[:, MLA_Q_RANK:MLA_Q_RANK + MLA_KV_RANK], kvnw_ref[...]).astype(BF16)
    q = _dot(cq, wq_ref[...])
    q_sw = _dot(cq, wqs_ref[...])
    k_nope = _dot(ckv, wk_ref[...])
    row_all = lax.broadcasted_iota(jnp.int32, (MLA_HEADS * LANES, 1), 0)
    ones_row = jnp.where(row_all % LANES == MLA_V, 1.0, 0.0)
    v_ref[...] = (_dot_nt(wv_ref[...], ckv) + ones_row).astype(v_ref.dtype)

    kpe = pltpu.roll(lat[:, LAT_W - LANES:], MLA_NOPE - KPE_OFF, axis=1)
    kpe_sw = jnp.where(lane < mid, pltpu.roll(kpe, LANES - half, axis=1), pltpu.roll(kpe, half, axis=1))
    k_rope = kpe * cos_tab + kpe_sw * sin_tab
    c_q = (cos_tab + nope) * scale
    s_q = sin_tab * scale
    for h in range(MLA_HEADS):
        blk = slice(h * LANES, (h + 1) * LANES)
        q_ref[:, blk] = (q[:, blk] * c_q + q_sw[:, blk] * s_q).astype(q_ref.dtype)
        k_ref[:, blk] = (k_nope[:, blk] + k_rope).astype(k_ref.dtype)


def _mla_prep(pos_row, inv_freq, lat, q_norm_w, kv_norm_w, wq_p, wqs_p, wk_p, wv_p):
    T = lat.shape[0]
    tm = min(TOK_TILE, T)
    full = lambda a: pl.BlockSpec(a.shape, lambda i: (0,) * a.ndim)
    return pl.pallas_call(
        _mla_prep_kernel,
        grid=(T // tm,),
        in_specs=[
            pl.BlockSpec((1, tm), lambda i: (0, i)),
            full(inv_freq),
            pl.BlockSpec((tm, LAT_W), lambda i: (i, 0)),
            full(q_norm_w), full(kv_norm_w), full(wq_p), full(wqs_p), full(wk_p), full(wv_p),
        ],
        out_specs=[
            pl.BlockSpec((tm, MLA_HEADS * LANES), lambda i: (i, 0)),
            pl.BlockSpec((tm, MLA_HEADS * LANES), lambda i: (i, 0)),
            pl.BlockSpec((MLA_HEADS * LANES, tm), lambda i: (0, i)),
        ],
        out_shape=[
            jax.ShapeDtypeStruct((T, MLA_HEADS * LANES), BF16),
            jax.ShapeDtypeStruct((T, MLA_HEADS * LANES), BF16),
            jax.ShapeDtypeStruct((MLA_HEADS * LANES, T), BF16),
        ],
        compiler_params=pltpu.CompilerParams(
            dimension_semantics=("arbitrary",), vmem_limit_bytes=VMEM_LIMIT),
        name="mla_prep",
    )(pos_row, inv_freq, lat, q_norm_w, kv_norm_w, wq_p, wqs_p, wk_p, wv_p)


def _attn_kernel(q_ref, k_ref, vt_ref, o_ref, s_ref, m_ref, acc_ref, *, tq):
    seq = q_ref.shape[0]
    nq = seq // tq
    nh = q_ref.shape[1] // LANES
    lane = lax.broadcasted_iota(jnp.int32, (1, LANES), 1)
    causal = lax.broadcasted_iota(jnp.int32, (tq, tq), 0) <= lax.broadcasted_iota(jnp.int32, (tq, tq), 1)
    blks = [slice(h * LANES, (h + 1) * LANES) for h in range(nh)]
    rows = lambda i: pl.ds(pl.multiple_of(i * tq, tq), tq)

    def scores(qi, j, h):
        s_ref[h] = _dot_nt(k_ref[rows(j), blks[h]], q_ref[rows(qi), blks[h]])

    def consume(j, h, masked):
        m_prev = m_ref[h][0:1, :]
        s = s_ref[h]
        if masked:
            s = jnp.where(causal, s, NEG_BIG)
        m_new = jnp.maximum(m_prev, jnp.max(s, axis=0, keepdims=True))
        alpha = jnp.exp2(m_prev - m_new)
        p = jnp.exp2(s - m_new).astype(BF16)
        m_ref[h] = jnp.broadcast_to(m_new, (SUBLANES, tq))
        acc_ref[h] = acc_ref[h] * alpha + _dot(vt_ref[blks[h], rows(j)], p)

    def consume_and_prefetch(qi, j, h, masked):
        consume(j, h, masked)
        nxt = h + ATT_AHEAD
        if nxt < nh:
            scores(qi, j, nxt)
        elif not masked:
            scores(qi, j + 1, nxt - nh)
        else:
            scores(jnp.minimum(qi + 1, nq - 1), 0, nxt - nh)

    def q_block(qi, carry):
        def blocks(j0, count):
            for j in range(count):
                for h in range(nh):
                    consume_and_prefetch(qi, j0 + j, h, False)

        def unrolled(i, c):
            blocks(ATT_UNROLL * i, ATT_UNROLL)
            return c

        m_ref[...] = jnp.full(m_ref.shape, NEG_BIG, F32)
        acc_ref[...] = jnp.zeros(acc_ref.shape, F32)
        n_full = qi
        lax.fori_loop(0, n_full // ATT_UNROLL, unrolled, 0)
        done = n_full - n_full % ATT_UNROLL
        size = ATT_UNROLL // 2
        while size >= 1:
            take = (n_full % (2 * size)) >= size

            @pl.when(take)
            def _(done=done, size=size):
                blocks(done, size)

            done = done + jnp.where(take, size, 0)
            size //= 2
        for h in range(nh):
            consume_and_prefetch(qi, n_full, h, True)
        outs = []
        for h in range(nh):
            acc = acc_ref[h]
            outs.append((acc * (1.0 / acc[MLA_V:MLA_V + 1, :])).T)
        for pr in range(nh // 2):
            pair = jnp.where(lane < MLA_V, outs[2 * pr], pltpu.roll(outs[2 * pr + 1], MLA_V, axis=1))
            o_ref[rows(qi), pr * LANES:(pr + 1) * LANES] = pair.astype(o_ref.dtype)
        return carry

    for h in range(ATT_AHEAD):
        scores(0, 0, h)
    lax.fori_loop(0, nq, q_block, 0)


def _attn(q3, k3, vt):
    B, S, _ = q3.shape
    tq = min(ATT_BLOCK, S)
    assert S % tq == 0
    nh = ATT_HEADS
    return pl.pallas_call(
        functools.partial(_attn_kernel, tq=tq),
        grid=(B, MLA_HEADS // nh),
        in_specs=[
            pl.BlockSpec((None, S, nh * LANES), lambda b, h: (b, 0, h)),
            pl.BlockSpec((None, S, nh * LANES), lambda b, h: (b, 0, h)),
            pl.BlockSpec((nh * LANES, S), lambda b, h: (h, b)),
        ],
        out_specs=pl.BlockSpec((None, S, nh * MLA_V), lambda b, h: (b, 0, h)),
        out_shape=jax.ShapeDtypeStruct((B, S, MLA_W), BF16),
        scratch_shapes=[pltpu.VMEM((nh, tq, tq), F32),
                        pltpu.VMEM((nh, SUBLANES, tq), F32),
                        pltpu.VMEM((nh, LANES, tq), F32)],
        compiler_params=pltpu.CompilerParams(
            dimension_semantics=("arbitrary", "arbitrary"), vmem_limit_bytes=VMEM_LIMIT),
        name="attn",
    )(q3, k3, vt)


def _tail_kernel(x_ref, yg_ref, om_ref, p_ref, monw_ref, wout_ref, mlpnw_ref, wup_ref, wdown_ref,
                 wpp_ref, postnw_ref, gatenw_ref, wpg_ref, finnw_ref, o_ref):
    ymla = _rms(om_ref[...].astype(F32), monw_ref[...]).astype(BF16)
    h = x_ref[...] + _dot(yg_ref[...], wout_ref[0:GDN_V_W, :]) + _dot(ymla, wout_ref[GDN_V_W:, :])
    u = _rms(h, mlpnw_ref[...]).astype(BF16)
    ff_blk = 1024
    acc = jnp.zeros_like(h)
    for j in range(D_FF // ff_blk):
        cols = slice(j * ff_blk, (j + 1) * ff_blk)
        hid = jnp.maximum(_dot(u, wup_ref[:, cols]), 0.0)
        acc = acc + _dot((hid * hid).astype(BF16), wdown_ref[cols, :])
    h = h + acc
    e = _rms(_dot(p_ref[...].astype(BF16), wpp_ref[...]), postnw_ref[...])
    gate = _sigmoid(_dot(_rms(h, gatenw_ref[...]).astype(BF16), wpg_ref[...]))
    h = h + gate * e
    o_ref[...] = _rms(h, finnw_ref[...])


def _tail(x2, yg, om, p2, mla_out_norm_w, w_out, mlp_norm_w, w_up, w_down, w_ple_proj,
          ple_post_norm_w, ple_gate_norm_w, w_ple_gate, final_norm_w):
    T = x2.shape[0]
    tm = min(TOK_TILE, T)
    tok = lambda w: pl.BlockSpec((tm, w), lambda i: (i, 0))
    const = lambda a: pl.BlockSpec(a.shape, lambda i: (0, 0), pipeline_mode=pl.Buffered(1))
    return pl.pallas_call(
        _tail_kernel,
        grid=(T // tm,),
        in_specs=[
            tok(D_MODEL), tok(GDN_V_W), tok(MLA_W), tok(PLE_DIM),
            const(mla_out_norm_w), const(w_out), const(mlp_norm_w), const(w_up), const(w_down),
            const(w_ple_proj), const(ple_post_norm_w), const(ple_gate_norm_w), const(w_ple_gate),
            const(final_norm_w),
        ],
        out_specs=tok(D_MODEL),
        out_shape=jax.ShapeDtypeStruct((T, D_MODEL), F32),
        compiler_params=pltpu.CompilerParams(
            dimension_semantics=("arbitrary",), vmem_limit_bytes=VMEM_LIMIT,
            allow_input_fusion=[i in (5, 7, 8, 9, 12) for i in range(14)]),
        name="tail",
    )(x2, yg, om, p2, mla_out_norm_w, w_out, mlp_norm_w, w_up, w_down, w_ple_proj,
      ple_post_norm_w, ple_gate_norm_w, w_ple_gate, final_norm_w)


def _pack_w_in(w):
    o_b = QKV_W + GDN_V_W
    o_cq = o_b + 2 * GDN_HEADS
    n_lat = MLA_Q_RANK + MLA_KV_RANK + MLA_ROPE
    w = w.astype(BF16)
    pad = jnp.zeros((D_MODEL, LAT_W - n_lat - 2 * GDN_HEADS), BF16)
    return jnp.concatenate([w[:, :o_b], w[:, o_cq:o_cq + n_lat], w[:, o_b:o_cq], pad], axis=1)


def _gate_lane_vector(per_head):
    return jnp.zeros((1, LANES), F32).at[0, A_OFF:A_OFF + GDN_HEADS].set(per_head.astype(F32))


def _pack_mla_weights(w_q_b, w_kv_b):
    wq = w_q_b.reshape(MLA_Q_RANK, MLA_HEADS, MLA_NOPE + MLA_ROPE)
    wq = jnp.pad(wq, ((0, 0), (0, 0), (0, LANES - MLA_NOPE - MLA_ROPE)))
    wkv = w_kv_b.reshape(MLA_KV_RANK, MLA_HEADS, MLA_NOPE + MLA_V)
    wk = jnp.pad(wkv[:, :, :MLA_NOPE], ((0, 0), (0, 0), (0, LANES - MLA_NOPE)))
    wv = jnp.pad(wkv[:, :, MLA_NOPE:], ((0, 0), (0, 0), (0, LANES - MLA_V)))
    half = MLA_ROPE // 2
    zeros = lambda n: jnp.zeros((MLA_Q_RANK, MLA_HEADS, n), wq.dtype)
    wq_sw = jnp.concatenate([zeros(MLA_NOPE), wq[:, :, MLA_NOPE + half:MLA_NOPE + MLA_ROPE],
                             wq[:, :, MLA_NOPE:MLA_NOPE + half], zeros(LANES - MLA_NOPE - MLA_ROPE)], axis=2)
    return (wq.reshape(MLA_Q_RANK, MLA_HEADS * LANES).astype(BF16),
            wq_sw.reshape(MLA_Q_RANK, MLA_HEADS * LANES).astype(BF16),
            wk.reshape(MLA_KV_RANK, MLA_HEADS * LANES).astype(BF16),
            wv.reshape(MLA_KV_RANK, MLA_HEADS * LANES).T.astype(BF16))


def kernel(x, p, positions, mix_norm_w, w_in, conv_w, A_log, dt_bias, gdn_norm_w, q_norm_w, w_q_b,
           kv_norm_w, w_kv_b, mla_out_norm_w, w_out, mlp_norm_w, w_up, w_down, w_ple_proj,
           ple_post_norm_w, ple_gate_norm_w, w_ple_gate, final_norm_w):
    B, S, _ = x.shape
    T = B * S
    assert w_in.shape[0] == 1, "one layer"
    row = lambda a: a.reshape(1, -1).astype(F32)
    x2 = x.reshape(T, D_MODEL)

    qkv, zg, lat = _inproj(x2, row(mix_norm_w[0]), _pack_w_in(w_in[0]), conv_w[0].astype(F32),
                           _gate_lane_vector(A_log[0]), _gate_lane_vector(dt_bias[0]), S)

    factors = _gdn_prep(qkv.reshape(B, S, QKV_W), lat.reshape(B, S, LAT_W))
    y_gdn = _gdn_scan(*factors, zg.reshape(B, S, GDN_V_W), row(jnp.tile(gdn_norm_w[0], GDN_HEADS)))

    half = MLA_ROPE // 2
    inv_freq = (ROPE_THETA ** (-jnp.arange(0, MLA_ROPE, 2, dtype=F32) / MLA_ROPE)).reshape(half, 1)
    q_att, k_att, v_att = _mla_prep(positions.reshape(1, T), inv_freq, lat, row(q_norm_w[0]),
                                    row(kv_norm_w[0]), *_pack_mla_weights(w_q_b[0], w_kv_b[0]))
    o_mla = _attn(q_att.reshape(B, S, -1), k_att.reshape(B, S, -1), v_att)

    out = _tail(x2, y_gdn.reshape(T, GDN_V_W), o_mla.reshape(T, MLA_W), p[0].reshape(T, PLE_DIM),
                row(mla_out_norm_w[0]), w_out[0].astype(BF16), row(mlp_norm_w[0]),
                w_up[0].astype(BF16), w_down[0].astype(BF16), w_ple_proj[0].astype(BF16),
                row(ple_post_norm_w[0]), row(ple_gate_norm_w[0]), w_ple_gate[0].astype(BF16),
                row(final_norm_w))
    return out.reshape(B, S, D_MODEL)
```
